```python
import math
import jax, jax.numpy as jnp
from jax import lax
import numpy as np

D_MODEL = 2048
BATCH = 2
SEQ = 4096
DEPTH = 4

PLE_DIM = 256

RWKV_HEADS = 12
RWKV_HEAD_DIM = 64
RWKV_DIM = RWKV_HEADS * RWKV_HEAD_DIM
RWKV_DECAY_LORA = 64
RWKV_AAA_LORA = 64
RWKV_GATE_LORA = 128
RWKV_GN_EPS = 64e-5
RWKV_IN_SIZES = (RWKV_DIM, RWKV_DIM, RWKV_DIM, RWKV_DECAY_LORA, RWKV_AAA_LORA, RWKV_GATE_LORA)
RWKV_IN_DIM = sum(RWKV_IN_SIZES)

DIL_PATTERNS = ((128, 1), (512, 4), (2048, 16))
DIL_N_GROUPS = 3
DIL_HEADS_PER_GROUP = 4
DIL_HEADS = DIL_N_GROUPS * DIL_HEADS_PER_GROUP
DIL_HEAD_DIM = 128
DIL_QKV_DIM = DIL_HEADS * DIL_HEAD_DIM
DIL_OUT_DIM = DIL_HEADS_PER_GROUP * DIL_HEAD_DIM
DIL_BLOCK = 128
ALIBI_MAX_BIAS = 8.0

MLA_HEADS = 6
MLA_NOPE_DIM = 128
MLA_ROPE_DIM = 64
MLA_V_DIM = 128
MLA_Q_LORA = 512
MLA_KV_LORA = 256
MLA_OUT_DIM = MLA_HEADS * MLA_V_DIM
ROPE_THETA = 10000.0
Q_BLOCK = 128

MIX_DIM = RWKV_DIM + DIL_OUT_DIM + MLA_OUT_DIM
IN_SIZES = (RWKV_IN_DIM, DIL_QKV_DIM, DIL_QKV_DIM, DIL_QKV_DIM, MLA_Q_LORA, MLA_KV_LORA, MLA_ROPE_DIM)
IN_DIM = sum(IN_SIZES)

N_EXPERTS = 32
N_EXPERT_GROUPS = 8
EXPERTS_PER_GROUP = 4
TOP_K = 2
D_EXPERT = 512
MOE_BLOCK = 128

DEEPNORM_ALPHA = (2 * DEPTH) ** 0.25
DEEPNORM_BETA = (8 * DEPTH) ** -0.25
LN_EPS = 1e-5
RMS_EPS = 1e-6
NEG_INF = -1e30

kernel_name = "hybrid_rwkv7_dilated_mla_grouped_moe_deepnorm"


def _split(z, sizes):
    return jnp.split(z, np.cumsum(sizes)[:-1].tolist(), axis=-1)


def _layernorm(x, g, b):
    xf = x.astype(jnp.float32)
    mu = xf.mean(-1, keepdims=True)
    var = jnp.square(xf - mu).mean(-1, keepdims=True)
    return ((xf - mu) * lax.rsqrt(var + LN_EPS) * g + b).astype(x.dtype)


def _rmsnorm(x, g):
    xf = x.astype(jnp.float32)
    return (xf * lax.rsqrt(jnp.mean(xf * xf, -1, keepdims=True) + RMS_EPS) * g).astype(x.dtype)


def _rope(t, pos):
    half = MLA_ROPE_DIM // 2
    inv = ROPE_THETA ** (-jnp.arange(half, dtype=jnp.float32) / half)
    ang = pos.astype(jnp.float32)[:, None] * inv[None, :]
    cos, sin = jnp.cos(ang)[None, :, None, :], jnp.sin(ang)[None, :, None, :]
    t1, t2 = t[..., :half].astype(jnp.float32), t[..., half:].astype(jnp.float32)
    return jnp.concatenate([t1 * cos - t2 * sin, t1 * sin + t2 * cos], axis=-1)


def _rwkv7(z, mu, w0, w_up, a0, a_up, g_up, k_k, k_a, r_k, gn_g, gn_b):
    B, S, _ = z.shape
    H, E = RWKV_HEADS, RWKV_HEAD_DIM
    z_prev = jnp.pad(z[:, :-1], ((0, 0), (1, 0), (0, 0)))
    z = z + (z_prev - z) * mu
    r, k, v, wd, ad, gd = _split(z, RWKV_IN_SIZES)
    w = -jax.nn.softplus(-(w0 + jnp.tanh(wd) @ w_up)) - 0.5
    decay = jnp.exp(-jnp.exp(w.astype(jnp.float32)))
    a = jax.nn.sigmoid(a0 + ad @ a_up)
    g = jax.nn.sigmoid(gd) @ g_up
    heads = lambda t: t.reshape(B, S, H, E).astype(jnp.float32)
    kk = heads(k * k_k)
    kk = kk / jnp.maximum(jnp.sqrt(jnp.sum(kk * kk, -1, keepdims=True)), 1e-12)
    k = k * (1 + (a - 1) * k_a)
    rh, kh, vh, ah, wh = heads(r), heads(k), heads(v), heads(a), heads(decay)

    def step(state, inp):
        r_t, w_t, k_t, v_t, a_t, b_t = inp
        sa = jnp.einsum('bhij,bhj->bhi', state, a_t)
        state = (state * w_t[:, :, None, :] + sa[..., None] * b_t[:, :, None, :]
                 + v_t[..., None] * k_t[:, :, None, :])
        return state, jnp.einsum('bhij,bhj->bhi', state, r_t)

    tm = lambda t: jnp.moveaxis(t, 1, 0)
    s0 = jnp.zeros((B, H, E, E), jnp.float32)
    _, y = lax.scan(step, s0, (tm(rh), tm(wh), tm(kh), tm(vh), tm(-kk), tm(kk * ah)))
    y = jnp.moveaxis(y, 0, 1)
    ym = y.mean(-1, keepdims=True)
    yv = jnp.square(y - ym).mean(-1, keepdims=True)
    y = ((y - ym) * lax.rsqrt(yv + RWKV_GN_EPS)).reshape(B, S, RWKV_DIM) * gn_g + gn_b
    bonus = jnp.sum(rh * kh * r_k, -1, keepdims=True) * vh
    y = y + bonus.reshape(B, S, RWKV_DIM)
    return (y * g).astype(z.dtype)


def _dilated_group(q, k, v, window, dilation, slopes):
    B, S, Hg, E = q.shape
    n_neigh = window // dilation
    L = S // dilation
    nb = -(-L // DIL_BLOCK)
    Lp = nb * DIL_BLOCK

    def sub(t):
        t = t.reshape(B, L, dilation, Hg, E).transpose(0, 2, 1, 3, 4)
        t = jnp.pad(t, ((0, 0), (0, 0), (0, Lp - L), (0, 0), (0, 0)))
        return t.reshape(B, dilation, nb, DIL_BLOCK, Hg, E)

    def band(t):
        prev = jnp.pad(t, ((0, 0), (0, 0), (1, 0), (0, 0), (0, 0), (0, 0)))[:, :, :-1]
        return jnp.concatenate([prev, t], axis=3)

    qb, kw, vw = sub(q), band(sub(k)), band(sub(v))
    s = jnp.einsum('bdnqhe,bdnkhe->bdnhqk', qb, kw) * (E ** -0.5)
    qi = jnp.arange(DIL_BLOCK)
    ki = jnp.arange(2 * DIL_BLOCK)
    rel = qi[:, None] + DIL_BLOCK - ki[None, :]
    key_idx = jnp.arange(nb)[:, None] * DIL_BLOCK - DIL_BLOCK + ki[None, :]
    valid = (rel >= 0)[None] & (rel <= n_neigh)[None] & (key_idx >= 0)[:, None, :]
    dist = (rel * dilation).astype(jnp.float32)
    s = s - slopes[:, None, None] * dist[None]
    s = jnp.where(valid[:, None], s, NEG_INF)
    m = s.max(-1, keepdims=True)
    e = jnp.exp(s - m)
    den = e.sum(-1, keepdims=True)
    o = jnp.einsum('bdnhqk,bdnkhe->bdnqhe', e / den, vw)
    lse = (m + jnp.log(den))[..., 0]

    def unsub(t):
        t = t[:, :, :L]
        return jnp.swapaxes(t, 1, 2).reshape((B, S) + t.shape[3:])

    o = unsub(o.reshape(B, dilation, Lp, Hg, E))
    lse = unsub(jnp.swapaxes(lse, 3, 4).reshape(B, dilation, Lp, Hg))
    return o, lse


def _dilated_attention(zq, zk, zv):
    B, S, _ = zq.shape
    shp = (B, S, DIL_N_GROUPS, DIL_HEADS_PER_GROUP, DIL_HEAD_DIM)
    q = zq.reshape(shp).astype(jnp.float32)
    k = zk.reshape(shp).astype(jnp.float32)
    v = zv.reshape(shp).astype(jnp.float32)
    slopes = jnp.exp2(-ALIBI_MAX_BIAS * jnp.arange(1, DIL_HEADS + 1, dtype=jnp.float32) / DIL_HEADS)
    slopes = slopes.reshape(DIL_N_GROUPS, DIL_HEADS_PER_GROUP)
    outs, lses = [], []
    for gi, (window, dilation) in enumerate(DIL_PATTERNS):
        o, l = _dilated_group(q[:, :, gi], k[:, :, gi], v[:, :, gi], window, dilation, slopes[gi])
        outs.append(o)
        lses.append(l)
    wts = jax.nn.softmax(jnp.stack(lses), axis=0)
    out = jnp.sum(wts[..., None] * jnp.stack(outs), axis=0)
    return out.reshape(B, S, DIL_OUT_DIM).astype(zq.dtype)


def _mla(zq, zkv, zkr, qa_g, w_uq, kva_g, w_ukv):
    B, S, _ = zq.shape
    H, QD = MLA_HEADS, MLA_NOPE_DIM + MLA_ROPE_DIM
    pos = jnp.arange(S)
    q = (_rmsnorm(zq, qa_g) @ w_uq).reshape(B, S, H, QD)
    kv = (_rmsnorm(zkv, kva_g) @ w_ukv).reshape(B, S, H, MLA_NOPE_DIM + MLA_V_DIM)
    k_nope, v = kv[..., :MLA_NOPE_DIM], kv[..., MLA_NOPE_DIM:]
    k_rope = _rope(zkr[:, :, None, :], pos)
    qf = jnp.concatenate([q[..., :MLA_NOPE_DIM].astype(jnp.float32), _rope(q[..., MLA_NOPE_DIM:], pos)], -1)
    kf = jnp.concatenate([k_nope.astype(jnp.float32), jnp.broadcast_to(k_rope, (B, S, H, MLA_ROPE_DIM))], -1)
    vf = v.astype(jnp.float32)
    scale = QD ** -0.5
    nq = S // Q_BLOCK
    qb = jnp.moveaxis(qf.reshape(B, nq, Q_BLOCK, H, QD), 1, 0)

    def block(args):
        q_blk, bi = args
        s = jnp.einsum('bqhe,bkhe->bhqk', q_blk, kf) * scale
        qpos = bi * Q_BLOCK + jnp.arange(Q_BLOCK)
        s = jnp.where(pos[None, :] <= qpos[:, None], s, NEG_INF)
        return jnp.einsum('bhqk,bkhe->bqhe', jax.nn.softmax(s, axis=-1), vf)

    o = lax.map(block, (qb, jnp.arange(nq)))
    return jnp.moveaxis(o, 0, 1).reshape(B, S, MLA_OUT_DIM).astype(zq.dtype)


def _moe(x, router_w, router_b, w_gate, w_up, w_down):
    B, S, D = x.shape
    N = B * S
    xt = x.reshape(N, D)
    scores = jax.nn.sigmoid((xt @ router_w).astype(jnp.float32))
    sel = scores + router_b.astype(jnp.float32)
    grp_score = lax.top_k(sel.reshape(N, N_EXPERT_GROUPS, EXPERTS_PER_GROUP), TOP_K)[0].sum(-1)
    g_best = jnp.argmax(grp_score, axis=-1)
    in_group = (jnp.arange(N_EXPERTS) // EXPERTS_PER_GROUP)[None, :] == g_best[:, None]
    _, idx = lax.top_k(jnp.where(in_group, sel, -jnp.inf), TOP_K)
    gate = jnp.take_along_axis(scores, idx, axis=-1)
    gate = gate / gate.sum(-1, keepdims=True)

    A = N * TOP_K
    e_flat = idx.reshape(A)
    tok = jnp.repeat(jnp.arange(N, dtype=jnp.int32), TOP_K)
    order = jnp.argsort(e_flat)
    e_s, tok_s, gate_s = e_flat[order], tok[order], gate.reshape(A)[order]
    counts = jnp.zeros((N_EXPERTS,), jnp.int32).at[e_flat].add(1)
    padded = (counts + MOE_BLOCK - 1) // MOE_BLOCK * MOE_BLOCK
    pad_end = jnp.cumsum(padded)
    pad_start = pad_end - padded
    start = jnp.cumsum(counts) - counts
    dest = pad_start[e_s] + jnp.arange(A, dtype=jnp.int32) - start[e_s]
    nb = (A + N_EXPERTS * (MOE_BLOCK - 1) + MOE_BLOCK - 1) // MOE_BLOCK
    P = nb * MOE_BLOCK
    buf_tok = jnp.full((P,), N, jnp.int32).at[dest].set(tok_s)
    xb = jnp.concatenate([xt, jnp.zeros((1, D), xt.dtype)], axis=0)[buf_tok].reshape(nb, MOE_BLOCK, D)
    blk_e = jnp.minimum(jnp.searchsorted(pad_end, jnp.arange(nb, dtype=jnp.int32) * MOE_BLOCK, side='right'),
                        N_EXPERTS - 1)

    def expert_block(args):
        xblk, e = args
        h = jax.nn.silu(xblk @ w_gate[e]) * (xblk @ w_up[e])
        return h @ w_down[e]

    yb = lax.map(expert_block, (xb, blk_e)).reshape(P, D)
    y = jax.ops.segment_sum(yb[dest] * gate_s[:, None].astype(yb.dtype), tok_s, num_segments=N)
    return y.reshape(B, S, D).astype(x.dtype)


def setup_inputs(seed: int = 0) -> dict:
    key = jax.random.key(seed)
    ks = iter(jax.random.split(key, 40))
    L, D = DEPTH, D_MODEL

    def nrm(shape, scale):
        return jax.random.normal(next(ks), shape, jnp.float32) * scale

    def uni(shape, lo, hi):
        return jax.random.uniform(next(ks), shape, jnp.float32, lo, hi)

    return {
        "x": nrm((BATCH, SEQ, D), 1.0),
        "p": nrm((DEPTH, BATCH, SEQ, PLE_DIM), 1.0),
        "w_in": nrm((L, D, IN_DIM), D ** -0.5),
        "rwkv_mu": uni((L, RWKV_IN_DIM), 0.0, 1.0),
        "rwkv_w0": uni((L, RWKV_DIM), -6.0, 1.0),
        "rwkv_w_up": nrm((L, RWKV_DECAY_LORA, RWKV_DIM), 0.1 * RWKV_DECAY_LORA ** -0.5),
        "rwkv_a0": nrm((L, RWKV_DIM), 0.1),
        "rwkv_a_up": nrm((L, RWKV_AAA_LORA, RWKV_DIM), 0.5 * RWKV_AAA_LORA ** -0.5),
        "rwkv_g_up": nrm((L, RWKV_GATE_LORA, RWKV_DIM), RWKV_GATE_LORA ** -0.5),
        "rwkv_k_k": 0.85 + nrm((L, RWKV_DIM), 0.05),
        "rwkv_k_a": 1.0 + nrm((L, RWKV_DIM), 0.05),
        "rwkv_r_k": nrm((L, RWKV_HEADS, RWKV_HEAD_DIM), 0.1),
        "rwkv_gn_g": 1.0 + nrm((L, RWKV_DIM), 0.05),
        "rwkv_gn_b": nrm((L, RWKV_DIM), 0.02),
        "mla_qa_g": 1.0 + nrm((L, MLA_Q_LORA), 0.05),
        "mla_w_uq": nrm((L, MLA_Q_LORA, MLA_HEADS * (MLA_NOPE_DIM + MLA_ROPE_DIM)), MLA_Q_LORA ** -0.5),
        "mla_kva_g": 1.0 + nrm((L, MLA_KV_LORA), 0.05),
        "mla_w_ukv": nrm((L, MLA_KV_LORA, MLA_HEADS * (MLA_NOPE_DIM + MLA_V_DIM)), MLA_KV_LORA ** -0.5),
        "w_out": nrm((L, MIX_DIM, D), DEEPNORM_BETA * MIX_DIM ** -0.5),
        "ln1_g": 1.0 + nrm((L, D), 0.05),
        "ln1_b": nrm((L, D), 0.02),
        "router_w": nrm((D, N_EXPERTS), D ** -0.5),
        "router_b": nrm((N_EXPERTS,), 0.01),
        "moe_w_gate": nrm((L, N_EXPERTS, D, D_EXPERT), D ** -0.5),
        "moe_w_up": nrm((L, N_EXPERTS, D, D_EXPERT), D ** -0.5),
        "moe_w_down": nrm((L, N_EXPERTS, D_EXPERT, D), DEEPNORM_BETA * D_EXPERT ** -0.5),
        "ln2_g": 1.0 + nrm((L, D), 0.05),
        "ln2_b": nrm((L, D), 0.02),
        "ple_w_proj": nrm((L, PLE_DIM, D), PLE_DIM ** -0.5),
        "ple_w_gate": nrm((L, D, D), D ** -0.5),
    }


def reference(x, p, w_in, rwkv_mu, rwkv_w0, rwkv_w_up, rwkv_a0, rwkv_a_up, rwkv_g_up,
              rwkv_k_k, rwkv_k_a, rwkv_r_k, rwkv_gn_g, rwkv_gn_b,
              mla_qa_g, mla_w_uq, mla_kva_g, mla_w_ukv, w_out, ln1_g, ln1_b,
              router_w, router_b, moe_w_gate, moe_w_up, moe_w_down, ln2_g, ln2_b,
              ple_w_proj, ple_w_gate):
    for i in range(DEPTH):
        z = x @ w_in[i]
        z_rwkv, z_dq, z_dk, z_dv, z_qa, z_kva, z_kr = _split(z, IN_SIZES)
        y_a = _rwkv7(z_rwkv, rwkv_mu[i], rwkv_w0[i], rwkv_w_up[i], rwkv_a0[i], rwkv_a_up[i],
                     rwkv_g_up[i], rwkv_k_k[i], rwkv_k_a[i], rwkv_r_k[i], rwkv_gn_g[i], rwkv_gn_b[i])
        y_b = _dilated_attention(z_dq, z_dk, z_dv)
        y_c = _mla(z_qa, z_kva, z_kr, mla_qa_g[i], mla_w_uq[i], mla_kva_g[i], mla_w_ukv[i])
        mix = jnp.concatenate([y_a, y_b, y_c], axis=-1) @ w_out[i]
        x = _layernorm(DEEPNORM_ALPHA * x + mix, ln1_g[i], ln1_b[i])
        ffn = _moe(x, router_w, router_b, moe_w_gate[i], moe_w_up[i], moe_w_down[i])
        x = _layernorm(DEEPNORM_ALPHA * x + ffn, ln2_g[i], ln2_b[i])
        x = x + jax.nn.sigmoid(x @ ple_w_gate[i]) * (p[i] @ ple_w_proj[i])
    return x
```

```python
import functools
import math

import numpy as np
import jax
import jax.numpy as jnp
from jax import lax
from jax.experimental import pallas as pl
from jax.experimental.pallas import tpu as pltpu

F32 = jnp.float32
BF16 = jnp.bfloat16
HIGHEST = lax.Precision.HIGHEST

PLE_DIM = 256
RWKV_HEADS = 12
RWKV_HEAD_DIM = 64
RWKV_DIM = RWKV_HEADS * RWKV_HEAD_DIM
RWKV_LORA_PAD = 128
RWKV_GATE_LORA = 128
RWKV_IN_DIM = 3 * RWKV_DIM + RWKV_LORA_PAD + RWKV_GATE_LORA
RWKV_GN_EPS = 64e-5
RWKV_CHUNK = 64
DIL_PATTERNS = ((128, 1), (512, 4), (2048, 16))
DIL_GROUPS = 3
DIL_HEADS_PER_GROUP = 4
DIL_HEAD_DIM = 128
DIL_QKV_DIM = DIL_GROUPS * DIL_HEADS_PER_GROUP * DIL_HEAD_DIM
DIL_OUT_DIM = DIL_HEADS_PER_GROUP * DIL_HEAD_DIM
DIL_BLOCK = 128
ALIBI_MAX_BIAS = 8.0
MLA_HEADS = 6
MLA_NOPE_DIM = 128
MLA_ROPE_DIM = 64
MLA_V_DIM = 128
MLA_Q_LORA = 512
MLA_KV_LORA = 256
MLA_QK_DIM = MLA_NOPE_DIM + MLA_ROPE_DIM
MLA_OUT_DIM = MLA_HEADS * MLA_V_DIM
ROPE_THETA = 10000.0
N_EXPERTS = 32
N_EXPERT_GROUPS = 8
EXPERTS_PER_GROUP = 4
TOP_K = 2
D_EXPERT = 512
MOE_BLOCK = 128
LN_EPS = 1e-5
RMS_EPS = 1e-6
NEG_INF = -1e30

ZO_DQ = 0
ZO_DK = DIL_QKV_DIM
ZO_DV = 2 * DIL_QKV_DIM
ZO_QA = 3 * DIL_QKV_DIM
ZO_KVA = ZO_QA + MLA_Q_LORA
ZO_KR = ZO_KVA + MLA_KV_LORA
ZO_DIM = ZO_KR + 256

V7X_VMEM_LIMIT_BYTES = 48 * 1024 * 1024


def _params(*sem):
    return pltpu.CompilerParams(dimension_semantics=sem, vmem_limit_bytes=V7X_VMEM_LIMIT_BYTES)


def _sigmoid(x):
    return 1.0 / (1.0 + jnp.exp(-x))


def _mm_body(x_ref, w_ref, o_ref):
    o_ref[...] = jnp.dot(x_ref[...], w_ref[...], preferred_element_type=F32).astype(o_ref.dtype)


def _matmul(x, w, out_dtype, tm, tn):
    m, k = x.shape
    n = w.shape[1]
    return pl.pallas_call(
        _mm_body,
        grid=(m // tm, n // tn),
        in_specs=[pl.BlockSpec((tm, k), lambda i, j: (i, 0)),
                  pl.BlockSpec((k, tn), lambda i, j: (0, j))],
        out_specs=pl.BlockSpec((tm, tn), lambda i, j: (i, j)),
        out_shape=jax.ShapeDtypeStruct((m, n), out_dtype),
        compiler_params=_params("parallel", "arbitrary"),
        name="in_proj",
    )(x, w)


def _rwkv_prep_body(z_ref, zp_ref, mu_ref, w0_ref, a0_ref, kk_ref, ka_ref, wup_ref, aup_ref, gup_ref,
                    r_o, k_o, v_o, kk_o, a_o, lw_o, g_o):
    i = pl.program_id(1)
    z = z_ref[0]
    last = zp_ref[0][7:8, :]
    last = jnp.where(i == 0, jnp.zeros_like(last), last)
    row = lax.broadcasted_iota(jnp.int32, z.shape, 0)
    zprev = jnp.where(row == 0, last, pltpu.roll(z, 1, axis=0))
    zs = z + (zprev - z) * mu_ref[...]
    d = RWKV_DIM
    r = zs[:, 0:d]
    k = zs[:, d:2 * d]
    v = zs[:, 2 * d:3 * d]
    lora = zs[:, 3 * d:3 * d + RWKV_LORA_PAD]
    gd = zs[:, 3 * d + RWKV_LORA_PAD:]
    u = w0_ref[...] + jnp.dot(jnp.tanh(lora), wup_ref[...], precision=HIGHEST, preferred_element_type=F32)
    nu = -u
    softplus = jnp.maximum(nu, 0.0) + jnp.log(1.0 + jnp.exp(-jnp.abs(nu)))
    w = -softplus - 0.5
    a = _sigmoid(a0_ref[...] + jnp.dot(lora, aup_ref[...], precision=HIGHEST, preferred_element_type=F32))
    g = jnp.dot(_sigmoid(gd), gup_ref[...], precision=HIGHEST, preferred_element_type=F32)
    r_o[0] = r
    k_o[0] = k * (1.0 + (a - 1.0) * ka_ref[...])
    v_o[0] = v
    kk_o[0] = k * kk_ref[...]
    a_o[0] = a
    lw_o[0] = -jnp.exp(w)
    g_o[0] = g


def _rwkv_prep(z, mu, w0, a0, k_k, k_a, w_up_pad, a_up_pad, g_up, t_blk=256):
    b, s, zin = z.shape
    d = RWKV_DIM
    vec = lambda n: pl.BlockSpec((1, n), lambda bi, i: (0, 0))
    mat = lambda r: pl.BlockSpec((r, d), lambda bi, i: (0, 0))
    out = jax.ShapeDtypeStruct((b, s, d), F32)
    ospec = pl.BlockSpec((1, t_blk, d), lambda bi, i: (bi, i, 0))
    return pl.pallas_call(
        _rwkv_prep_body,
        grid=(b, s // t_blk),
        in_specs=[pl.BlockSpec((1, t_blk, zin), lambda bi, i: (bi, i, 0)),
                  pl.BlockSpec((1, 8, zin), lambda bi, i: (bi, jnp.maximum(i * (t_blk // 8) - 1, 0), 0)),
                  vec(zin), vec(d), vec(d), vec(d), vec(d),
                  mat(RWKV_LORA_PAD), mat(RWKV_LORA_PAD), mat(RWKV_GATE_LORA)],
        out_specs=[ospec] * 7,
        out_shape=[out] * 7,
        compiler_params=_params("parallel", "arbitrary"),
        name="rwkv_prep",
    )(z, z, mu, w0, a0, k_k, k_a, w_up_pad, a_up_pad, g_up)


def _bnt(a, b, prec):
    return lax.dot_general(a, b, (((2,), (2,)), ((0,), (0,))), precision=prec, preferred_element_type=F32)


def _bnn(a, b, prec):
    return lax.dot_general(a, b, (((2,), (1,)), ((0,), (0,))), precision=prec, preferred_element_type=F32)


def _btn(a, b, prec):
    return lax.dot_general(a, b, (((1,), (1,)), ((0,), (0,))), precision=prec, preferred_element_type=F32)


def _rwkv_scan_body(r_ref, k_ref, v_ref, kk_ref, a_ref, lw_ref, g_ref, rk_ref, gng_ref, gnb_ref,
                    y_ref, st_ref, *, prec):
    c = pl.program_id(1)

    @pl.when(c == 0)
    def _():
        st_ref[...] = jnp.zeros_like(st_ref)

    t = RWKV_CHUNK
    r, k, v, a, lw = r_ref[0], k_ref[0], v_ref[0], a_ref[0], lw_ref[0]
    kk = kk_ref[0]
    kk = kk / jnp.maximum(jnp.sqrt(jnp.sum(kk * kk, axis=-1, keepdims=True)), 1e-12)
    h = r.shape[0]

    ti = lax.broadcasted_iota(jnp.int32, (t, t), 0)
    si = lax.broadcasted_iota(jnp.int32, (t, t), 1)
    incl = (si <= ti)
    strict = (si < ti)
    tri = jnp.broadcast_to(incl.astype(F32)[None], (h, t, t))
    lp = _bnn(tri, lw, HIGHEST)
    lp_prev = lp - lw
    lp_end = lp[:, t - 1:t, :]
    p_incl = jnp.exp(lp)
    p_inv = jnp.exp(-lp)
    at = -kk * jnp.exp(lp_prev)
    bt = kk * a * p_inv
    kt = k * p_inv
    rt = r * p_incl
    to_end = jnp.exp(lp_end - lp)
    b_end = kk * a * to_end
    k_end = k * to_end

    a_ab = jnp.where(strict[None], _bnt(at, bt, prec), 0.0)
    a_ak = jnp.where(strict[None], _bnt(at, kt, prec), 0.0)
    a_rb = jnp.where(incl[None], _bnt(rt, bt, prec), 0.0)
    a_rk = jnp.where(incl[None], _bnt(rt, kt, prec), 0.0)

    sub = 16
    same_blk = ((ti // sub) == (si // sub))[None]
    eye = (ti == si).astype(F32)[None]
    ld = jnp.where(same_blk, a_ab, 0.0)
    lo = a_ab - ld
    dinv = eye + ld
    pw = ld
    for _ in range(3):
        pw = _bnn(pw, pw, prec)
        dinv = dinv + _bnn(dinv, pw, prec)
    n1 = _bnn(dinv, lo, prec)
    n2 = _bnn(n1, n1, prec)
    tinv = eye + n1 + n2 + _bnn(n1, n2, prec)
    tinv = _bnn(tinv, dinv, prec)

    akv = _bnn(a_ak, v, prec)
    wt = _bnn(tinv, at, prec)
    u0 = _bnn(tinv, akv, prec)
    q = rt + _bnn(a_rb, wt, prec)
    y0 = _bnn(a_rb, u0, prec) + _bnn(a_rk, v, prec)
    ei = lax.broadcasted_iota(jnp.int32, (RWKV_HEAD_DIM, RWKV_HEAD_DIM), 0)
    ej = lax.broadcasted_iota(jnp.int32, (RWKV_HEAD_DIM, RWKV_HEAD_DIM), 1)
    diag_end = jnp.where((ei == ej)[None], jnp.exp(lp_end), 0.0)
    m_t = diag_end + _btn(b_end, wt, prec)
    c_t = _btn(b_end, u0, prec) + _btn(k_end, v, prec)

    s0 = st_ref[...]
    y = _bnn(q, s0, prec) + y0
    st_ref[...] = _bnn(m_t, s0, prec) + c_t

    ym = jnp.mean(y, axis=-1, keepdims=True)
    yc = y - ym
    yv = jnp.mean(yc * yc, axis=-1, keepdims=True)
    y = yc * lax.rsqrt(yv + RWKV_GN_EPS) * gng_ref[...] + gnb_ref[...]
    bonus = jnp.sum(r * k * rk_ref[...], axis=-1, keepdims=True) * v
    y_ref[0] = ((y + bonus) * g_ref[0]).astype(y_ref.dtype)


def _rwkv_scan(r, k, v, kk, a, lw, g, r_k, gn_g, gn_b, prec=HIGHEST):
    b, h, s, e = r.shape
    t = RWKV_CHUNK
    seq = pl.BlockSpec((1, h, t, e), lambda bi, c: (bi, 0, c, 0))
    par = pl.BlockSpec((h, 1, e), lambda bi, c: (0, 0, 0))
    return pl.pallas_call(
        functools.partial(_rwkv_scan_body, prec=prec),
        grid=(b, s // t),
        in_specs=[seq] * 7 + [par] * 3,
        out_specs=seq,
        out_shape=jax.ShapeDtypeStruct((b, h, s, e), F32),
        scratch_shapes=[pltpu.VMEM((h, e, e), F32)],
        compiler_params=_params("parallel", "arbitrary"),
        name="rwkv_scan",
    )(r, k, v, kk, a, lw, g, r_k, gn_g, gn_b)


def _dil_body(slopes_ref, q_ref, kc_ref, kp_ref, vc_ref, vp_ref, o_ref, lse_ref, *, blocks_per_seq0):
    g = pl.program_id(0)
    i = pl.program_id(1)
    nb = blocks_per_seq0 >> (2 * g)
    prev_lim = jnp.where((i & (nb - 1)) == 0, -2 * DIL_BLOCK, 0)
    head = (i // nb) & (DIL_HEADS_PER_GROUP - 1)
    slope = slopes_ref[g * DIL_HEADS_PER_GROUP + head]
    dil = (1 << (2 * g)).astype(F32)
    q = q_ref[0, 0]
    scale = DIL_HEAD_DIM ** -0.5
    nt = (((1,), (1,)), ((), ()))
    s_c = lax.dot_general(q, kc_ref[0, 0], nt, preferred_element_type=F32) * scale
    s_p = lax.dot_general(q, kp_ref[0, 0], nt, preferred_element_type=F32) * scale
    qi = lax.broadcasted_iota(jnp.int32, (DIL_BLOCK, DIL_BLOCK), 0)
    ki = lax.broadcasted_iota(jnp.int32, (DIL_BLOCK, DIL_BLOCK), 1)
    rel_c = qi - ki
    rel_p = rel_c + DIL_BLOCK
    bias = slope * dil
    s_c = jnp.where(rel_c >= 0, s_c - bias * rel_c.astype(F32), NEG_INF)
    s_p = jnp.where(rel_c <= prev_lim, s_p - bias * rel_p.astype(F32), NEG_INF)
    m = jnp.maximum(jnp.max(s_c, axis=-1, keepdims=True), jnp.max(s_p, axis=-1, keepdims=True))
    e_c = jnp.exp(s_c - m)
    e_p = jnp.exp(s_p - m)
    den = jnp.sum(e_c, axis=-1, keepdims=True) + jnp.sum(e_p, axis=-1, keepdims=True)
    acc = (jnp.dot(e_c.astype(BF16), vc_ref[0, 0], preferred_element_type=F32)
           + jnp.dot(e_p.astype(BF16), vp_ref[0, 0], preferred_element_type=F32))
    o_ref[0, 0] = acc / den
    lse_ref[0, 0] = jnp.broadcast_to(m + jnp.log(den), (DIL_BLOCK, DIL_HEAD_DIM))


def _dilated_blocks(slopes, q, k, v, blocks_per_seq0):
    ng, nblk = q.shape[0], q.shape[1]
    blk = (1, 1, DIL_BLOCK, DIL_HEAD_DIM)
    cur = pl.BlockSpec(blk, lambda g, i, s: (g, i, 0, 0))
    prev = pl.BlockSpec(blk, lambda g, i, s: (g, jnp.maximum(i - 1, 0), 0, 0))
    out = jax.ShapeDtypeStruct((ng, nblk, DIL_BLOCK, DIL_HEAD_DIM), F32)
    return pl.pallas_call(
        functools.partial(_dil_body, blocks_per_seq0=blocks_per_seq0),
        grid_spec=pltpu.PrefetchScalarGridSpec(
            num_scalar_prefetch=1, grid=(ng, nblk),
            in_specs=[cur, cur, prev, cur, prev],
            out_specs=[cur, cur]),
        out_shape=[out, out],
        compiler_params=_params("parallel", "arbitrary"),
        name="dilated_attn",
    )(slopes, q, k, k, v, v)


def _dil_merge_body(o_ref, l_ref, out_ref):
    l = l_ref[...]
    m = jnp.max(l, axis=0, keepdims=True)
    w = jnp.exp(l - m)
    out_ref[...] = (jnp.sum(w * o_ref[...], axis=0) / jnp.sum(w, axis=0)).astype(out_ref.dtype)


def _dil_merge(o, lse, tm=512):
    ng, n, d = o.shape
    spec = pl.BlockSpec((ng, tm, d), lambda i: (0, i, 0))
    return pl.pallas_call(
        _dil_merge_body,
        grid=(n // tm,),
        in_specs=[spec, spec],
        out_specs=pl.BlockSpec((tm, d), lambda i: (i, 0)),
        out_shape=jax.ShapeDtypeStruct((n, d), BF16),
        compiler_params=_params("parallel"),
        name="dilated_merge",
    )(o, lse)


def _rms(x_bf16, g):
    x = x_bf16.astype(F32)
    return (x * lax.rsqrt(jnp.mean(x * x, axis=-1, keepdims=True) + RMS_EPS) * g).astype(BF16)


def _mla_q_body(x_ref, g_ref, w_ref, cos_ref, sin_ref, q_ref):
    acc = jnp.dot(_rms(x_ref[...], g_ref[...]), w_ref[...], preferred_element_type=F32)
    scale = MLA_QK_DIM ** -0.5
    cos, sin = cos_ref[...], sin_ref[...]
    nope_w = MLA_HEADS * MLA_NOPE_DIM
    for h in range(MLA_HEADS):
        q_ref[0, h, :, 0:MLA_NOPE_DIM] = (acc[:, h * 128:(h + 1) * 128] * scale).astype(BF16)
        base = nope_w + h * 256
        rope = acc[:, base:base + 128] * cos + acc[:, base + 128:base + 256] * sin
        q_ref[0, h, :, MLA_NOPE_DIM:MLA_QK_DIM] = (rope[:, 0:MLA_ROPE_DIM] * scale).astype(BF16)


def _mla_q(zo, g, w, cos, sin, batch, tm=512):
    n = zo.shape[0]
    s = n // batch
    nblk = s // tm
    return pl.pallas_call(
        _mla_q_body,
        grid=(batch, nblk),
        in_specs=[pl.BlockSpec((tm, MLA_Q_LORA), lambda b, i: (b * nblk + i, ZO_QA // MLA_Q_LORA)),
                  pl.BlockSpec((1, MLA_Q_LORA), lambda b, i: (0, 0)),
                  pl.BlockSpec(w.shape, lambda b, i: (0, 0)),
                  pl.BlockSpec((tm, 128), lambda b, i: (i, 0)),
                  pl.BlockSpec((tm, 128), lambda b, i: (i, 0))],
        out_specs=pl.BlockSpec((1, MLA_HEADS, tm, MLA_QK_DIM), lambda b, i: (b, 0, i, 0)),
        out_shape=jax.ShapeDtypeStruct((batch, MLA_HEADS, s, MLA_QK_DIM), BF16),
        compiler_params=_params("parallel", "parallel"),
        name="mla_q_proj",
    )(zo, g, w, cos, sin)


def _mla_kv_body(x_ref, kr_ref, g_ref, w_ref, cos_ref, sin_ref, k_ref, v_ref):
    acc = jnp.dot(_rms(x_ref[...], g_ref[...]), w_ref[...], preferred_element_type=F32)
    kr = kr_ref[...].astype(F32)
    rope = (kr[:, 0:128] * cos_ref[...] + kr[:, 128:256] * sin_ref[...])[:, 0:MLA_ROPE_DIM].astype(BF16)
    for h in range(MLA_HEADS):
        k_ref[0, h, :, 0:MLA_NOPE_DIM] = acc[:, h * 256:h * 256 + 128].astype(BF16)
        k_ref[0, h, :, MLA_NOPE_DIM:MLA_QK_DIM] = rope
        v_ref[0, h] = acc[:, h * 256 + 128:(h + 1) * 256].astype(BF16)


def _mla_kv(zo, g, w, cos, sin, batch, tm=512):
    n = zo.shape[0]
    s = n // batch
    nblk = s // tm
    return pl.pallas_call(
        _mla_kv_body,
        grid=(batch, nblk),
        in_specs=[pl.BlockSpec((tm, MLA_KV_LORA), lambda b, i: (b * nblk + i, ZO_KVA // MLA_KV_LORA)),
                  pl.BlockSpec((tm, 256), lambda b, i: (b * nblk + i, ZO_KR // 256)),
                  pl.BlockSpec((1, MLA_KV_LORA), lambda b, i: (0, 0)),
                  pl.BlockSpec(w.shape, lambda b, i: (0, 0)),
                  pl.BlockSpec((tm, 128), lambda b, i: (i, 0)),
                  pl.BlockSpec((tm, 128), lambda b, i: (i, 0))],
        out_specs=[pl.BlockSpec((1, MLA_HEADS, tm, MLA_QK_DIM), lambda b, i: (b, 0, i, 0)),
                   pl.BlockSpec((1, MLA_HEADS, tm, MLA_V_DIM), lambda b, i: (b, 0, i, 0))],
        out_shape=[jax.ShapeDtypeStruct((batch, MLA_HEADS, s, MLA_QK_DIM), BF16),
                   jax.ShapeDtypeStruct((batch, MLA_HEADS, s, MLA_V_DIM), BF16)],
        compiler_params=_params("parallel", "parallel"),
        name="mla_kv_proj",
    )(zo, zo, g, w, cos, sin)


def _flash_body(q_ref, k_ref, v_ref, o_ref, m_ref, l_ref, acc_ref, *, tq, tk):
    i = pl.program_id(2)
    j = pl.program_id(3)

    @pl.when(j == 0)
    def _():
        m_ref[...] = jnp.full_like(m_ref, NEG_INF)
        l_ref[...] = jnp.zeros_like(l_ref)
        acc_ref[...] = jnp.zeros_like(acc_ref)

    def update(masked):
        s = lax.dot_general(q_ref[0, 0], k_ref[0, 0], (((1,), (1,)), ((), ())), preferred_element_type=F32)
        if masked:
            qi = lax.broadcasted_iota(jnp.int32, (tq, tk), 0)
            ki = lax.broadcasted_iota(jnp.int32, (tq, tk), 1)
            s = jnp.where(ki <= qi, s, NEG_INF)
        m_old = m_ref[...]
        m_new = jnp.maximum(m_old, jnp.max(s, axis=-1, keepdims=True))
        alpha = jnp.exp(m_old - m_new)
        p = jnp.exp(s - m_new)
        l_ref[...] = alpha * l_ref[...] + jnp.sum(p, axis=-1, keepdims=True)
        acc_ref[...] = alpha * acc_ref[...] + jnp.dot(p.astype(BF16), v_ref[0, 0], preferred_element_type=F32)
        m_ref[...] = m_new

    @pl.when(j < i)
    def _():
        update(False)

    @pl.when(j == i)
    def _():
        update(True)
        o_ref[0] = (acc_ref[...] / l_ref[...]).astype(o_ref.dtype)


def _flash(q, k, v, tq=512):
    b, h, s, dq = q.shape
    dv = v.shape[-1]
    tk = tq
    nq = s // tq
    return pl.pallas_call(
        functools.partial(_flash_body, tq=tq, tk=tk),
        grid=(b, h, nq, nq),
        in_specs=[pl.BlockSpec((1, 1, tq, dq), lambda bi, hi, i, j: (bi, hi, i, 0)),
                  pl.BlockSpec((1, 1, tk, dq), lambda bi, hi, i, j: (bi, hi, jnp.minimum(j, i), 0)),
                  pl.BlockSpec((1, 1, tk, dv), lambda bi, hi, i, j: (bi, hi, jnp.minimum(j, i), 0))],
        out_specs=pl.BlockSpec((1, tq, dv), lambda bi, hi, i, j: (bi, i, hi)),
        out_shape=jax.ShapeDtypeStruct((b, s, h * dv), BF16),
        scratch_shapes=[pltpu.VMEM((tq, 1), F32), pltpu.VMEM((tq, 1), F32), pltpu.VMEM((tq, dv), F32)],
        compiler_params=_params("parallel", "parallel", "parallel", "arbitrary"),
        name="mla_flash",
    )(q, k, v)


def _layernorm(h, g, b):
    mu = jnp.mean(h, axis=-1, keepdims=True)
    hc = h - mu
    var = jnp.mean(hc * hc, axis=-1, keepdims=True)
    return hc * lax.rsqrt(var + LN_EPS) * g + b


def _out_ln_body(mix_ref, w_ref, x_ref, g_ref, b_ref, o_ref, *, alpha):
    acc = jnp.dot(mix_ref[...], w_ref[...], preferred_element_type=F32)
    o_ref[...] = _layernorm(alpha * x_ref[...] + acc, g_ref[...], b_ref[...])


def _out_ln(mix, w, x, g, b, alpha, tm=256):
    n, d = x.shape
    k = mix.shape[1]
    row = lambda c: pl.BlockSpec((tm, c), lambda i: (i, 0))
    vec = pl.BlockSpec((1, d), lambda i: (0, 0))
    return pl.pallas_call(
        functools.partial(_out_ln_body, alpha=alpha),
        grid=(n // tm,),
        in_specs=[row(k), pl.BlockSpec((k, d), lambda i: (0, 0)), row(d), vec, vec],
        out_specs=row(d),
        out_shape=jax.ShapeDtypeStruct((n, d), F32),
        compiler_params=_params("parallel"),
        name="out_proj_ln",
    )(mix, w, x, g, b)


def _router_body(x_ref, w_ref, b_ref, idx_ref, gate_ref):
    logits = lax.dot_general(w_ref[...], x_ref[...], (((1,), (1,)), ((), ())),
                             precision=HIGHEST, preferred_element_type=F32)
    scores = _sigmoid(logits)
    sel = scores + b_ref[...]
    ng = N_EXPERT_GROUPS
    sel_j = [sel[j * ng:(j + 1) * ng] for j in range(EXPERTS_PER_GROUP)]
    sc_j = [scores[j * ng:(j + 1) * ng] for j in range(EXPERTS_PER_GROUP)]
    grp = None
    for p in range(EXPERTS_PER_GROUP):
        for q in range(p + 1, EXPERTS_PER_GROUP):
            pair = sel_j[p] + sel_j[q]
            grp = pair if grp is None else jnp.maximum(grp, pair)
    gi = lax.broadcasted_iota(jnp.int32, grp.shape, 0)
    gmax = jnp.max(grp, axis=0, keepdims=True)
    gbest = jnp.min(jnp.where(grp == gmax, gi, ng), axis=0, keepdims=True)
    pick = gi == gbest
    mem = [jnp.sum(jnp.where(pick, t, 0.0), axis=0, keepdims=True) for t in sel_j]
    msc = [jnp.sum(jnp.where(pick, t, 0.0), axis=0, keepdims=True) for t in sc_j]

    def first_argmax(vals, exclude):
        best = jnp.full_like(vals[0], -jnp.inf)
        bi = jnp.zeros(vals[0].shape, jnp.int32)
        bs = jnp.zeros_like(vals[0])
        for j in range(EXPERTS_PER_GROUP):
            ok = vals[j] > best
            if exclude is not None:
                ok = ok & (exclude != j)
            best = jnp.where(ok, vals[j], best)
            bi = jnp.where(ok, j, bi)
            bs = jnp.where(ok, msc[j], bs)
        return bi, bs

    i1, s1 = first_argmax(mem, None)
    i2, s2 = first_argmax(mem, i1)
    tot = s1 + s2
    base = gbest * EXPERTS_PER_GROUP
    idx_ref[0:1, :] = base + i1
    idx_ref[1:2, :] = base + i2
    gate_ref[0:1, :] = s1 / tot
    gate_ref[1:2, :] = s2 / tot


def _router(x, w_t, bias, tm=512):
    n, d = x.shape
    return pl.pallas_call(
        _router_body,
        grid=(n // tm,),
        in_specs=[pl.BlockSpec((tm, d), lambda i: (i, 0)),
                  pl.BlockSpec((N_EXPERTS, d), lambda i: (0, 0)),
                  pl.BlockSpec((N_EXPERTS, 1), lambda i: (0, 0))],
        out_specs=[pl.BlockSpec((TOP_K, tm), lambda i: (0, i)), pl.BlockSpec((TOP_K, tm), lambda i: (0, i))],
        out_shape=[jax.ShapeDtypeStruct((TOP_K, n), jnp.int32), jax.ShapeDtypeStruct((TOP_K, n), F32)],
        compiler_params=_params("parallel"),
        name="moe_router",
    )(x, w_t, bias)


def _expert_body(blk_e_ref, nvalid_ref, src_ref, x_hbm, gate_ref, wg_ref, wu_ref, wd_ref, out_hbm,
                 xg_ref, yo_ref, sem_in, sem_out, *, n_tokens):
    i = pl.program_id(0)
    nv = nvalid_ref[i]
    base = i * MOE_BLOCK

    def in_copy(r):
        a = src_ref[base + r]
        tok = jnp.where(a >= n_tokens, a - n_tokens, a)
        return pltpu.make_async_copy(x_hbm.at[pl.ds(tok, 1)], xg_ref.at[pl.ds(r, 1)], sem_in)

    def out_copy(r):
        a = src_ref[base + r]
        return pltpu.make_async_copy(yo_ref.at[pl.ds(r, 1)], out_hbm.at[pl.ds(a, 1)], sem_out)

    @pl.when(nv > 0)
    def _():
        @pl.when(nv < MOE_BLOCK)
        def _():
            xg_ref[...] = jnp.zeros_like(xg_ref)

        lax.fori_loop(0, nv, lambda r, c: (in_copy(r).start(), c)[1], 0)
        lax.fori_loop(0, nv, lambda r, c: (in_copy(r).wait(), c)[1], 0)
        xb = xg_ref[...].astype(BF16)
        gt = jnp.dot(xb, wg_ref[0], preferred_element_type=F32)
        up = jnp.dot(xb, wu_ref[0], preferred_element_type=F32)
        hid = (gt * _sigmoid(gt) * up).astype(BF16)
        yo_ref[...] = jnp.dot(hid, wd_ref[0], preferred_element_type=F32) * gate_ref[0]
        lax.fori_loop(0, nv, lambda r, c: (out_copy(r).start(), c)[1], 0)
        lax.fori_loop(0, nv, lambda r, c: (out_copy(r).wait(), c)[1], 0)


def _experts(blk_e, nvalid, src, x, gate_buf, wg, wu, wd):
    n, d = x.shape
    nb = blk_e.shape[0]
    wspec = lambda shp: pl.BlockSpec((1,) + shp, lambda i, be, nv, sr: (be[i], 0, 0))
    return pl.pallas_call(
        functools.partial(_expert_body, n_tokens=n),
        grid_spec=pltpu.PrefetchScalarGridSpec(
            num_scalar_prefetch=3, grid=(nb,),
            in_specs=[pl.BlockSpec(memory_space=pl.ANY),
                      pl.BlockSpec((1, MOE_BLOCK, 1), lambda i, be, nv, sr: (i, 0, 0)),
                      wspec((d, D_EXPERT)), wspec((d, D_EXPERT)), wspec((D_EXPERT, d))],
            out_specs=pl.BlockSpec(memory_space=pl.ANY),
            scratch_shapes=[pltpu.VMEM((MOE_BLOCK, d), F32), pltpu.VMEM((MOE_BLOCK, d), F32),
                            pltpu.SemaphoreType.DMA(()), pltpu.SemaphoreType.DMA(())]),
        out_shape=jax.ShapeDtypeStruct((TOP_K * n, d), F32),
        compiler_params=_params("arbitrary"),
        name="moe_experts",
    )(blk_e, nvalid, src, x, gate_buf, wg, wu, wd)


def _dispatch(idx, gate):
    n = idx.shape[1]
    a_tot = TOP_K * n
    e_flat = idx.reshape(a_tot)
    onehot = (e_flat[:, None] == jnp.arange(N_EXPERTS, dtype=jnp.int32)[None, :]).astype(jnp.int32)
    csum = jnp.cumsum(onehot, axis=0)
    rank = jnp.sum(onehot * csum, axis=1) - 1
    counts = csum[-1]
    padded = (counts + MOE_BLOCK - 1) // MOE_BLOCK * MOE_BLOCK
    pad_end = jnp.cumsum(padded)
    pad_start = pad_end - padded
    dest = pad_start[e_flat] + rank
    nb = (a_tot + N_EXPERTS * (MOE_BLOCK - 1) + MOE_BLOCK - 1) // MOE_BLOCK
    p_rows = nb * MOE_BLOCK
    src = jnp.zeros((p_rows,), jnp.int32).at[dest].set(jnp.arange(a_tot, dtype=jnp.int32))
    gate_buf = jnp.zeros((p_rows,), F32).at[dest].set(gate.reshape(a_tot)).reshape(nb, MOE_BLOCK, 1)
    blk_start = jnp.arange(nb, dtype=jnp.int32) * MOE_BLOCK
    blk_e = jnp.minimum(jnp.searchsorted(pad_end, blk_start, side='right'), N_EXPERTS - 1).astype(jnp.int32)
    nvalid = jnp.clip(pad_start[blk_e] + counts[blk_e] - blk_start, 0, MOE_BLOCK).astype(jnp.int32)
    return blk_e, nvalid, src, gate_buf


def _final_body(x_ref, ya_ref, yb_ref, g_ref, b_ref, p_ref, wp_ref, wg_ref, o_ref, *, alpha):
    x2 = _layernorm(alpha * x_ref[...] + (ya_ref[...] + yb_ref[...]), g_ref[...], b_ref[...])
    gate = jnp.dot(x2.astype(BF16), wg_ref[...], preferred_element_type=F32)
    proj = jnp.dot(p_ref[...], wp_ref[...], preferred_element_type=F32)
    o_ref[...] = x2 + _sigmoid(gate) * proj


def _final(x, y2, g, b, p, wp, wg, alpha, tm=256):
    n, d = x.shape
    nblk = n // tm
    row = pl.BlockSpec((tm, d), lambda i: (i, 0))
    vec = pl.BlockSpec((1, d), lambda i: (0, 0))
    return pl.pallas_call(
        functools.partial(_final_body, alpha=alpha),
        grid=(nblk,),
        in_specs=[row, row, pl.BlockSpec((tm, d), lambda i: (i + nblk, 0)), vec, vec,
                  pl.BlockSpec((tm, PLE_DIM), lambda i: (i, 0)),
                  pl.BlockSpec((PLE_DIM, d), lambda i: (0, 0)),
                  pl.BlockSpec((d, d), lambda i: (0, 0))],
        out_specs=row,
        out_shape=jax.ShapeDtypeStruct((n, d), F32),
        compiler_params=_params("parallel"),
        name="moe_combine_ln_ple",
    )(x, y2, y2, g, b, p, wp, wg)


def _rot_cols(w):
    half = MLA_ROPE_DIM // 2
    return jnp.concatenate([-w[..., half:], w[..., :half]], axis=-1)


def _pad_lanes(w, width=128):
    return jnp.pad(w, [(0, 0)] * (w.ndim - 1) + [(0, width - w.shape[-1])])


def _rope_tables(s):
    half = MLA_ROPE_DIM // 2
    inv = ROPE_THETA ** (-jnp.arange(half, dtype=F32) / half)
    ang = jnp.arange(s, dtype=F32)[:, None] * inv[None, :]
    cos = jnp.concatenate([jnp.cos(ang), jnp.cos(ang)], axis=-1)
    sin = jnp.concatenate([jnp.sin(ang), jnp.sin(ang)], axis=-1)
    return _pad_lanes(cos), _pad_lanes(sin)


def _dil_split(t, batch, s, dil):
    l = s // dil
    t = t.reshape(batch, l, dil, DIL_HEADS_PER_GROUP, DIL_HEAD_DIM).transpose(0, 2, 3, 1, 4)
    return t.reshape(batch * dil * DIL_HEADS_PER_GROUP * (l // DIL_BLOCK), DIL_BLOCK, DIL_HEAD_DIM)


def _dil_join(t, batch, s, dil):
    l = s // dil
    t = t.reshape(batch, dil, DIL_HEADS_PER_GROUP, l, DIL_HEAD_DIM).transpose(0, 3, 1, 2, 4)
    return t.reshape(batch * s, DIL_OUT_DIM)


def kernel(x, p, w_in, rwkv_mu, rwkv_w0, rwkv_w_up, rwkv_a0, rwkv_a_up, rwkv_g_up, rwkv_k_k, rwkv_k_a, rwkv_r_k, rwkv_gn_g, rwkv_gn_b, mla_qa_g, mla_w_uq, mla_kva_g, mla_w_ukv, w_out, ln1_g, ln1_b, router_w, router_b, moe_w_gate, moe_w_up, moe_w_down, ln2_g, ln2_b, ple_w_proj, ple_w_gate):
    batch, s, d = x.shape
    depth = w_in.shape[0]
    n = batch * s
    alpha = (2 * depth) ** 0.25
    assert s % (DIL_PATTERNS[-1][1] * DIL_BLOCK) == 0 and s % 512 == 0
    hh, he = RWKV_HEADS, RWKV_HEAD_DIM

    cos, sin = _rope_tables(s)
    slopes = jnp.exp2(-ALIBI_MAX_BIAS * jnp.arange(1, 13, dtype=F32) / 12)
    perm = np.array([4 * g + j for j in range(EXPERTS_PER_GROUP) for g in range(N_EXPERT_GROUPS)])
    router_wt = router_w.T[perm]
    router_bt = router_b[perm].reshape(N_EXPERTS, 1)

    xf = x.reshape(n, d)
    for li in range(depth):
        wi = w_in[li]
        off = np.cumsum([0, RWKV_IN_DIM, DIL_QKV_DIM, DIL_QKV_DIM, DIL_QKV_DIM, MLA_Q_LORA, MLA_KV_LORA, MLA_ROPE_DIM])
        w_kr = wi[:, off[6]:off[7]]
        w_rest = jnp.concatenate([wi[:, off[1]:off[6]], _pad_lanes(w_kr), _pad_lanes(_rot_cols(w_kr))], axis=1)
        xb = xf.astype(BF16)
        z_r = _matmul(xb, wi[:, :RWKV_IN_DIM].astype(BF16), F32, 1024, 512)
        z_o = _matmul(xb, w_rest.astype(BF16), BF16, 1024, 512)

        w_up_pad = jnp.pad(rwkv_w_up[li], ((0, 64), (0, 0)))
        a_up_pad = jnp.pad(rwkv_a_up[li], ((64, 0), (0, 0)))
        row = lambda t: t.reshape(1, -1)
        prep = _rwkv_prep(z_r.reshape(batch, s, RWKV_IN_DIM), row(rwkv_mu[li]), row(rwkv_w0[li]), row(rwkv_a0[li]),
                          row(rwkv_k_k[li]), row(rwkv_k_a[li]), w_up_pad, a_up_pad, rwkv_g_up[li])
        heads = lambda t: t.reshape(batch, s, hh, he).transpose(0, 2, 1, 3)
        hpar = lambda t: t.reshape(hh, 1, he)
        y_a = _rwkv_scan(*[heads(t) for t in prep], hpar(rwkv_r_k[li]), hpar(rwkv_gn_g[li]), hpar(rwkv_gn_b[li]))
        y_a = y_a.transpose(0, 2, 1, 3).reshape(n, RWKV_DIM).astype(BF16)

        def grouped(col0):
            return jnp.stack([_dil_split(z_o[:, col0 + gi * DIL_OUT_DIM:col0 + (gi + 1) * DIL_OUT_DIM], batch, s, dil)
                              for gi, (_, dil) in enumerate(DIL_PATTERNS)])
        o_b, lse_b = _dilated_blocks(slopes, grouped(ZO_DQ), grouped(ZO_DK), grouped(ZO_DV), s // DIL_BLOCK)
        join = lambda t: jnp.stack([_dil_join(t[gi], batch, s, dil) for gi, (_, dil) in enumerate(DIL_PATTERNS)])
        y_b = _dil_merge(join(o_b), join(lse_b))

        wq = mla_w_uq[li].reshape(MLA_Q_LORA, MLA_HEADS, MLA_QK_DIM)
        wq_rope = wq[:, :, MLA_NOPE_DIM:]
        wq_all = jnp.concatenate(
            [wq[:, :, :MLA_NOPE_DIM].reshape(MLA_Q_LORA, -1),
             jnp.concatenate([_pad_lanes(wq_rope), _pad_lanes(_rot_cols(wq_rope))], axis=-1).reshape(MLA_Q_LORA, -1)],
            axis=1).astype(BF16)
        q_c = _mla_q(z_o, row(mla_qa_g[li]), wq_all, cos, sin, batch)
        k_c, v_c = _mla_kv(z_o, row(mla_kva_g[li]), mla_w_ukv[li].astype(BF16), cos, sin, batch)
        y_c = _flash(q_c, k_c, v_c).reshape(n, MLA_OUT_DIM)

        mix = jnp.concatenate([y_a, y_b, y_c], axis=-1)
        x1 = _out_ln(mix, w_out[li].astype(BF16), xf, row(ln1_g[li]), row(ln1_b[li]), alpha)

        idx, gate = _router(x1, router_wt, router_bt)
        blk_e, nvalid, src, gate_buf = _dispatch(idx, gate)
        y2 = _experts(blk_e, nvalid, src, x1, gate_buf, moe_w_gate[li].astype(BF16), moe_w_up[li].astype(BF16),
                      moe_w_down[li].astype(BF16))

        xf = _final(x1, y2, row(ln2_g[li]), row(ln2_b[li]), p[li].reshape(n, PLE_DIM).astype(BF16),
                    ple_w_proj[li].astype(BF16), ple_w_gate[li].astype(BF16), alpha)
    return xf.reshape(batch, s, d)
```

```python
import functools
import math

import numpy as np
import jax
import jax.numpy as jnp
from jax import lax
from jax.experimental import pallas as pl
from jax.experimental.pallas import tpu as pltpu

F32 = jnp.float32
BF16 = jnp.bfloat16
HIGHEST = lax.Precision.HIGHEST

PLE_DIM = 256
RWKV_HEADS = 12
RWKV_HEAD_DIM = 64
RWKV_DIM = RWKV_HEADS * RWKV_HEAD_DIM
RWKV_LORA_PAD = 128
RWKV_GATE_LORA = 128
RWKV_IN_DIM = 3 * RWKV_DIM + RWKV_LORA_PAD + RWKV_GATE_LORA
RWKV_GN_EPS = 64e-5
RWKV_CHUNK = 64
DIL_PATTERNS = ((128, 1), (512, 4), (2048, 16))
DIL_GROUPS = 3
DIL_HEADS_PER_GROUP = 4
DIL_HEAD_DIM = 128
DIL_QKV_DIM = DIL_GROUPS * DIL_HEADS_PER_GROUP * DIL_HEAD_DIM
DIL_OUT_DIM = DIL_HEADS_PER_GROUP * DIL_HEAD_DIM
DIL_BLOCK = 128
ALIBI_MAX_BIAS = 8.0
MLA_HEADS = 6
MLA_NOPE_DIM = 128
MLA_ROPE_DIM = 64
MLA_V_DIM = 128
MLA_Q_LORA = 512
MLA_KV_LORA = 256
MLA_QK_DIM = MLA_NOPE_DIM + MLA_ROPE_DIM
MLA_OUT_DIM = MLA_HEADS * MLA_V_DIM
ROPE_THETA = 10000.0
N_EXPERTS = 32
N_EXPERT_GROUPS = 8
EXPERTS_PER_GROUP = 4
TOP_K = 2
D_EXPERT = 512
MOE_BLOCK = 128
LN_EPS = 1e-5
RMS_EPS = 1e-6
NEG_INF = -1e30

ZO_DQ = 0
ZO_DK = DIL_QKV_DIM
ZO_DV = 2 * DIL_QKV_DIM
ZO_QA = 3 * DIL_QKV_DIM
ZO_KVA = ZO_QA + MLA_Q_LORA
ZO_KR = ZO_KVA + MLA_KV_LORA
ZO_DIM = ZO_KR + 256

V7X_VMEM_LIMIT_BYTES = 48 * 1024 * 1024


def _params(*sem):
    return pltpu.CompilerParams(dimension_semantics=sem, vmem_limit_bytes=V7X_VMEM_LIMIT_BYTES)


def _sigmoid(x):
    return 1.0 / (1.0 + jnp.exp(-x))


def _mm_body(x_ref, w_ref, o_ref):
    o_ref[...] = jnp.dot(x_ref[...], w_ref[...], preferred_element_type=F32).astype(o_ref.dtype)


def _matmul(x, w, out_dtype, tm, tn):
    m, k = x.shape
    n = w.shape[1]
    return pl.pallas_call(
        _mm_body,
        grid=(m // tm, n // tn),
        in_specs=[pl.BlockSpec((tm, k), lambda i, j: (i, 0)),
                  pl.BlockSpec((k, tn), lambda i, j: (0, j))],
        out_specs=pl.BlockSpec((tm, tn), lambda i, j: (i, j)),
        out_shape=jax.ShapeDtypeStruct((m, n), out_dtype),
        compiler_params=_params("parallel", "arbitrary"),
        name="in_proj",
    )(x, w)


def _rwkv_prep_body(z_ref, zp_ref, mu_ref, w0_ref, a0_ref, kk_ref, ka_ref, wup_ref, aup_ref, gup_ref,
                    r_o, k_o, v_o, kk_o, a_o, lw_o, g_o):
    i = pl.program_id(1)
    z = z_ref[0]
    last = zp_ref[0][7:8, :]
    last = jnp.where(i == 0, jnp.zeros_like(last), last)
    row = lax.broadcasted_iota(jnp.int32, z.shape, 0)
    zprev = jnp.where(row == 0, last, pltpu.roll(z, 1, axis=0))
    zs = z + (zprev - z) * mu_ref[...]
    d = RWKV_DIM
    r = zs[:, 0:d]
    k = zs[:, d:2 * d]
    v = zs[:, 2 * d:3 * d]
    lora = zs[:, 3 * d:3 * d + RWKV_LORA_PAD]
    gd = zs[:, 3 * d + RWKV_LORA_PAD:]
    u = w0_ref[...] + jnp.dot(jnp.tanh(lora), wup_ref[...], precision=HIGHEST, preferred_element_type=F32)
    nu = -u
    softplus = jnp.maximum(nu, 0.0) + jnp.log(1.0 + jnp.exp(-jnp.abs(nu)))
    w = -softplus - 0.5
    a = _sigmoid(a0_ref[...] + jnp.dot(lora, aup_ref[...], precision=HIGHEST, preferred_element_type=F32))
    g = jnp.dot(_sigmoid(gd), gup_ref[...], precision=HIGHEST, preferred_element_type=F32)
    r_o[0] = r
    k_o[0] = k * (1.0 + (a - 1.0) * ka_ref[...])
    v_o[0] = v
    kk_o[0] = k * kk_ref[...]
    a_o[0] = a
    lw_o[0] = -jnp.exp(w)
    g_o[0] = g


def _rwkv_prep(z, mu, w0, a0, k_k, k_a, w_up_pad, a_up_pad, g_up, t_blk=256):
    b, s, zin = z.shape
    d = RWKV_DIM
    vec = lambda n: pl.BlockSpec((1, n), lambda bi, i: (0, 0))
    mat = lambda r: pl.BlockSpec((r, d), lambda bi, i: (0, 0))
    out = jax.ShapeDtypeStruct((b, s, d), F32)
    ospec = pl.BlockSpec((1, t_blk, d), lambda bi, i: (bi, i, 0))
    return pl.pallas_call(
        _rwkv_prep_body,
        grid=(b, s // t_blk),
        in_specs=[pl.BlockSpec((1, t_blk, zin), lambda bi, i: (bi, i, 0)),
                  pl.BlockSpec((1, 8, zin), lambda bi, i: (bi, jnp.maximum(i * (t_blk // 8) - 1, 0), 0)),
                  vec(zin), vec(d), vec(d), vec(d), vec(d),
                  mat(RWKV_LORA_PAD), mat(RWKV_LORA_PAD), mat(RWKV_GATE_LORA)],
        out_specs=[ospec] * 7,
        out_shape=[out] * 7,
        compiler_params=_params("parallel", "arbitrary"),
        name="rwkv_prep",
    )(z, z, mu, w0, a0, k_k, k_a, w_up_pad, a_up_pad, g_up)


def _bdot(a, b, dims, prec):
    if prec is None:
        a, b = a.astype(BF16), b.astype(BF16)
    return lax.dot_general(a, b, (dims, ((0,), (0,))), precision=prec, preferred_element_type=F32)


def _bnt(a, b, prec):
    return _bdot(a, b, ((2,), (2,)), prec)


def _bnn(a, b, prec):
    return _bdot(a, b, ((2,), (1,)), prec)


def _btn(a, b, prec):
    return _bdot(a, b, ((1,), (1,)), prec)


def _rwkv_scan_body(r_ref, k_ref, v_ref, kk_ref, a_ref, lw_ref, g_ref, rk_ref, gng_ref, gnb_ref,
                    y_ref, st_ref, *, prec, prec_state):
    c = pl.program_id(1)

    @pl.when(c == 0)
    def _():
        st_ref[...] = jnp.zeros_like(st_ref)

    t = RWKV_CHUNK
    r, k, v, a, lw = r_ref[0], k_ref[0], v_ref[0], a_ref[0], lw_ref[0]
    kk = kk_ref[0]
    kk = kk / jnp.maximum(jnp.sqrt(jnp.sum(kk * kk, axis=-1, keepdims=True)), 1e-12)
    h = r.shape[0]

    ti = lax.broadcasted_iota(jnp.int32, (t, t), 0)
    si = lax.broadcasted_iota(jnp.int32, (t, t), 1)
    incl = (si <= ti)
    strict = (si < ti)
    tri = jnp.broadcast_to(incl.astype(F32)[None], (h, t, t))
    lp = _bnn(tri, lw, HIGHEST)
    lp_prev = lp - lw
    lp_end = lp[:, t - 1:t, :]
    p_incl = jnp.exp(lp)
    p_inv = jnp.exp(-lp)
    at = -kk * jnp.exp(lp_prev)
    bt = kk * a * p_inv
    kt = k * p_inv
    rt = r * p_incl
    to_end = jnp.exp(lp_end - lp)
    b_end = kk * a * to_end
    k_end = k * to_end

    a_ab = jnp.where(strict[None], _bnt(at, bt, prec), 0.0)
    a_ak = jnp.where(strict[None], _bnt(at, kt, prec), 0.0)
    a_rb = jnp.where(incl[None], _bnt(rt, bt, prec), 0.0)
    a_rk = jnp.where(incl[None], _bnt(rt, kt, prec), 0.0)

    sub = 16
    same_blk = ((ti // sub) == (si // sub))[None]
    eye = (ti == si).astype(F32)[None]
    ld = jnp.where(same_blk, a_ab, 0.0)
    lo = a_ab - ld
    dinv = eye + ld
    pw = ld
    for _ in range(3):
        pw = _bnn(pw, pw, prec)
        dinv = dinv + _bnn(dinv, pw, prec)
    n1 = _bnn(dinv, lo, prec)
    n2 = _bnn(n1, n1, prec)
    tinv = eye + n1 + n2 + _bnn(n1, n2, prec)
    tinv = _bnn(tinv, dinv, prec)

    akv = _bnn(a_ak, v, prec)
    wt = _bnn(tinv, at, prec)
    u0 = _bnn(tinv, akv, prec)
    q = rt + _bnn(a_rb, wt, prec)
    y0 = _bnn(a_rb, u0, prec) + _bnn(a_rk, v, prec)
    ei = lax.broadcasted_iota(jnp.int32, (RWKV_HEAD_DIM, RWKV_HEAD_DIM), 0)
    ej = lax.broadcasted_iota(jnp.int32, (RWKV_HEAD_DIM, RWKV_HEAD_DIM), 1)
    diag_end = jnp.where((ei == ej)[None], jnp.exp(lp_end), 0.0)
    m_t = diag_end + _btn(b_end, wt, prec)
    c_t = _btn(b_end, u0, prec) + _btn(k_end, v, prec)

    s0 = st_ref[...]
    y = _bnn(q, s0, prec_state) + y0
    st_ref[...] = _bnn(m_t, s0, prec_state) + c_t

    ym = jnp.mean(y, axis=-1, keepdims=True)
    yc = y - ym
    yv = jnp.mean(yc * yc, axis=-1, keepdims=True)
    y = yc * lax.rsqrt(yv + RWKV_GN_EPS) * gng_ref[...] + gnb_ref[...]
    bonus = jnp.sum(r * k * rk_ref[...], axis=-1, keepdims=True) * v
    y_ref[0] = ((y + bonus) * g_ref[0]).astype(y_ref.dtype)


def _rwkv_scan(r, k, v, kk, a, lw, g, r_k, gn_g, gn_b, prec=None, prec_state=None):
    b, h, s, e = r.shape
    t = RWKV_CHUNK
    seq = pl.BlockSpec((1, h, t, e), lambda bi, c: (bi, 0, c, 0))
    par = pl.BlockSpec((h, 1, e), lambda bi, c: (0, 0, 0))
    return pl.pallas_call(
        functools.partial(_rwkv_scan_body, prec=prec, prec_state=prec_state),
        grid=(b, s // t),
        in_specs=[seq] * 7 + [par] * 3,
        out_specs=seq,
        out_shape=jax.ShapeDtypeStruct((b, h, s, e), F32),
        scratch_shapes=[pltpu.VMEM((h, e, e), F32)],
        compiler_params=_params("parallel", "arbitrary"),
        name="rwkv_scan",
    )(r, k, v, kk, a, lw, g, r_k, gn_g, gn_b)


def _dil_body(slopes_ref, q_ref, kc_ref, kp_ref, vc_ref, vp_ref, *rest, group, dil, n_sub, merge):
    head = pl.program_id(2)
    nblk = pl.program_id(3)
    bias = slopes_ref[group * DIL_HEADS_PER_GROUP + head] * float(dil)
    scale = DIL_HEAD_DIM ** -0.5
    nt = (((1,), (1,)), ((), ()))
    qi = lax.broadcasted_iota(jnp.int32, (DIL_BLOCK, DIL_BLOCK), 0)
    ki = lax.broadcasted_iota(jnp.int32, (DIL_BLOCK, DIL_BLOCK), 1)
    rel_c = qi - ki
    dist_c = bias * rel_c.astype(F32)
    dist_p = bias * (rel_c + DIL_BLOCK).astype(F32)
    if merge:
        o0_ref, l0_ref, o1_ref, l1_ref, y_ref = rest
    else:
        o_ref, lse_ref = rest
    for u in range(n_sub):
        rows = slice(u * DIL_BLOCK, (u + 1) * DIL_BLOCK)
        q = q_ref[rows, :]
        if u == 0:
            k_prev, v_prev = kp_ref[...], vp_ref[...]
            prev_lim = jnp.where(nblk == 0, -2 * DIL_BLOCK, 0)
        else:
            prows = slice((u - 1) * DIL_BLOCK, u * DIL_BLOCK)
            k_prev, v_prev = kc_ref[prows, :], vc_ref[prows, :]
            prev_lim = 0
        s_c = lax.dot_general(q, kc_ref[rows, :], nt, preferred_element_type=F32) * scale
        s_p = lax.dot_general(q, k_prev, nt, preferred_element_type=F32) * scale
        s_c = jnp.where(rel_c >= 0, s_c - dist_c, NEG_INF)
        s_p = jnp.where(rel_c <= prev_lim, s_p - dist_p, NEG_INF)
        m = jnp.maximum(jnp.max(s_c, axis=-1, keepdims=True), jnp.max(s_p, axis=-1, keepdims=True))
        e_c = jnp.exp(s_c - m)
        e_p = jnp.exp(s_p - m)
        den = jnp.sum(e_c, axis=-1, keepdims=True) + jnp.sum(e_p, axis=-1, keepdims=True)
        acc = (jnp.dot(e_c.astype(BF16), vc_ref[rows, :], preferred_element_type=F32)
               + jnp.dot(e_p.astype(BF16), v_prev, preferred_element_type=F32))
        o = acc / den
        lse = jnp.broadcast_to(m + jnp.log(den), (DIL_BLOCK, DIL_HEAD_DIM))
        if merge:
            l0, l1 = l0_ref[rows, :], l1_ref[rows, :]
            top = jnp.maximum(jnp.maximum(l0, l1), lse)
            w0, w1, w2 = jnp.exp(l0 - top), jnp.exp(l1 - top), jnp.exp(lse - top)
            y = (w0 * o0_ref[rows, :] + w1 * o1_ref[rows, :] + w2 * o) / (w0 + w1 + w2)
            y_ref[rows, :] = y.astype(y_ref.dtype)
        else:
            o_ref[rows, :] = o
            lse_ref[rows, :] = lse


def _dilated_group(slopes, zo, group, batch, s, prior=None):
    n = batch * s
    dil = DIL_PATTERNS[group][1]
    nb = s // dil // DIL_BLOCK
    n_sub = min(nb, 8)
    steps = nb // n_sub
    zcols = ZO_DIM // DIL_HEAD_DIM
    ocols = DIL_OUT_DIM // DIL_HEAD_DIM
    zv = zo.reshape(n // dil, dil * ZO_DIM)
    rows = n_sub * DIL_BLOCK

    def cur(col0):
        c = col0 // DIL_HEAD_DIM + group * DIL_HEADS_PER_GROUP
        return pl.BlockSpec((rows, DIL_HEAD_DIM), lambda b, r, h, i, sl: (b * steps + i, r * zcols + c + h))

    def prev(col0):
        c = col0 // DIL_HEAD_DIM + group * DIL_HEADS_PER_GROUP
        return pl.BlockSpec((DIL_BLOCK, DIL_HEAD_DIM),
                            lambda b, r, h, i, sl: (jnp.maximum(b * nb + i * n_sub - 1, 0), r * zcols + c + h))

    ospec = pl.BlockSpec((rows, DIL_HEAD_DIM), lambda b, r, h, i, sl: (b * steps + i, r * ocols + h))
    oshape = lambda dt: jax.ShapeDtypeStruct((n // dil, dil * DIL_OUT_DIM), dt)
    merge = prior is not None
    extra = [t.reshape(n // dil, dil * DIL_OUT_DIM) for t in prior] if merge else []
    out = pl.pallas_call(
        functools.partial(_dil_body, group=group, dil=dil, n_sub=n_sub, merge=merge),
        grid_spec=pltpu.PrefetchScalarGridSpec(
            num_scalar_prefetch=1, grid=(batch, dil, DIL_HEADS_PER_GROUP, steps),
            in_specs=[cur(ZO_DQ), cur(ZO_DK), prev(ZO_DK), cur(ZO_DV), prev(ZO_DV)] + [ospec] * len(extra),
            out_specs=ospec if merge else [ospec, ospec]),
        out_shape=oshape(BF16) if merge else [oshape(F32), oshape(F32)],
        compiler_params=_params("parallel", "parallel", "parallel", "arbitrary"),
        name="dilated_attn_g%d" % group,
    )(slopes, zv, zv, zv, zv, zv, *extra)
    if merge:
        return out.reshape(n, DIL_OUT_DIM)
    return [t.reshape(n, DIL_OUT_DIM) for t in out]


def _rms(x_bf16, g):
    x = x_bf16.astype(F32)
    return (x * lax.rsqrt(jnp.mean(x * x, axis=-1, keepdims=True) + RMS_EPS) * g).astype(BF16)


def _mla_q_body(x_ref, g_ref, w_ref, cos_ref, sin_ref, q_ref):
    acc = jnp.dot(_rms(x_ref[...], g_ref[...]), w_ref[...], preferred_element_type=F32)
    scale = MLA_QK_DIM ** -0.5
    cos, sin = cos_ref[...], sin_ref[...]
    nope_w = MLA_HEADS * MLA_NOPE_DIM
    for h in range(MLA_HEADS):
        q_ref[0, h, :, 0:MLA_NOPE_DIM] = (acc[:, h * 128:(h + 1) * 128] * scale).astype(BF16)
        base = nope_w + h * 256
        rope = acc[:, base:base + 128] * cos + acc[:, base + 128:base + 256] * sin
        q_ref[0, h, :, MLA_NOPE_DIM:MLA_QK_DIM] = (rope[:, 0:MLA_ROPE_DIM] * scale).astype(BF16)


def _mla_q(zo, g, w, cos, sin, batch, tm=512):
    n = zo.shape[0]
    s = n // batch
    nblk = s // tm
    return pl.pallas_call(
        _mla_q_body,
        grid=(batch, nblk),
        in_specs=[pl.BlockSpec((tm, MLA_Q_LORA), lambda b, i: (b * nblk + i, ZO_QA // MLA_Q_LORA)),
                  pl.BlockSpec((1, MLA_Q_LORA), lambda b, i: (0, 0)),
                  pl.BlockSpec(w.shape, lambda b, i: (0, 0)),
                  pl.BlockSpec((tm, 128), lambda b, i: (i, 0)),
                  pl.BlockSpec((tm, 128), lambda b, i: (i, 0))],
        out_specs=pl.BlockSpec((1, MLA_HEADS, tm, MLA_QK_DIM), lambda b, i: (b, 0, i, 0)),
        out_shape=jax.ShapeDtypeStruct((batch, MLA_HEADS, s, MLA_QK_DIM), BF16),
        compiler_params=_params("parallel", "parallel"),
        name="mla_q_proj",
    )(zo, g, w, cos, sin)


def _mla_kv_body(x_ref, kr_ref, g_ref, w_ref, cos_ref, sin_ref, k_ref, v_ref):
    acc = jnp.dot(_rms(x_ref[...], g_ref[...]), w_ref[...], preferred_element_type=F32)
    kr = kr_ref[...].astype(F32)
    rope = (kr[:, 0:128] * cos_ref[...] + kr[:, 128:256] * sin_ref[...])[:, 0:MLA_ROPE_DIM].astype(BF16)
    for h in range(MLA_HEADS):
        k_ref[0, h, :, 0:MLA_NOPE_DIM] = acc[:, h * 256:h * 256 + 128].astype(BF16)
        k_ref[0, h, :, MLA_NOPE_DIM:MLA_QK_DIM] = rope
        v_ref[0, h] = acc[:, h * 256 + 128:(h + 1) * 256].astype(BF16)


def _mla_kv(zo, g, w, cos, sin, batch, tm=512):
    n = zo.shape[0]
    s = n // batch
    nblk = s // tm
    return pl.pallas_call(
        _mla_kv_body,
        grid=(batch, nblk),
        in_specs=[pl.BlockSpec((tm, MLA_KV_LORA), lambda b, i: (b * nblk + i, ZO_KVA // MLA_KV_LORA)),
                  pl.BlockSpec((tm, 256), lambda b, i: (b * nblk + i, ZO_KR // 256)),
                  pl.BlockSpec((1, MLA_KV_LORA), lambda b, i: (0, 0)),
                  pl.BlockSpec(w.shape, lambda b, i: (0, 0)),
                  pl.BlockSpec((tm, 128), lambda b, i: (i, 0)),
                  pl.BlockSpec((tm, 128), lambda b, i: (i, 0))],
        out_specs=[pl.BlockSpec((1, MLA_HEADS, tm, MLA_QK_DIM), lambda b, i: (b, 0, i, 0)),
                   pl.BlockSpec((1, MLA_HEADS, tm, MLA_V_DIM), lambda b, i: (b, 0, i, 0))],
        out_shape=[jax.ShapeDtypeStruct((batch, MLA_HEADS, s, MLA_QK_DIM), BF16),
                   jax.ShapeDtypeStruct((batch, MLA_HEADS, s, MLA_V_DIM), BF16)],
        compiler_params=_params("parallel", "parallel"),
        name="mla_kv_proj",
    )(zo, zo, g, w, cos, sin)


def _flash_body(qi_ref, kj_ref, q_ref, k_ref, v_ref, o_ref, m_ref, l_ref, acc_ref, *, tq):
    p = pl.program_id(1)
    i = qi_ref[p]
    j = kj_ref[p]
    heads, dv = acc_ref.shape[0], acc_ref.shape[2]

    @pl.when(j == 0)
    def _():
        m_ref[...] = jnp.full_like(m_ref, NEG_INF)
        l_ref[...] = jnp.zeros_like(l_ref)
        acc_ref[...] = jnp.zeros_like(acc_ref)

    def update(masked):
        for h in range(heads):
            s = lax.dot_general(q_ref[0, h], k_ref[0, h], (((1,), (1,)), ((), ())), preferred_element_type=F32)
            if masked:
                qpos = lax.broadcasted_iota(jnp.int32, (tq, tq), 0)
                kpos = lax.broadcasted_iota(jnp.int32, (tq, tq), 1)
                s = jnp.where(kpos <= qpos, s, NEG_INF)
            m_old = m_ref[h]
            m_new = jnp.maximum(m_old, jnp.max(s, axis=-1, keepdims=True))
            alpha = jnp.exp(m_old - m_new)
            e = jnp.exp(s - m_new)
            l_ref[h] = alpha * l_ref[h] + jnp.sum(e, axis=-1, keepdims=True)
            acc_ref[h] = alpha * acc_ref[h] + jnp.dot(e.astype(BF16), v_ref[0, h], preferred_element_type=F32)
            m_ref[h] = m_new

    @pl.when(j < i)
    def _():
        update(False)

    @pl.when(j == i)
    def _():
        update(True)
        for h in range(heads):
            o_ref[0, :, h * dv:(h + 1) * dv] = (acc_ref[h] / l_ref[h]).astype(o_ref.dtype)


def _flash(q, k, v, tq=512):
    b, h, s, dq = q.shape
    dv = v.shape[-1]
    nq = s // tq
    pairs = [(i, j) for i in range(nq) for j in range(i + 1)]
    qi = jnp.asarray([pr[0] for pr in pairs], jnp.int32)
    kj = jnp.asarray([pr[1] for pr in pairs], jnp.int32)
    return pl.pallas_call(
        functools.partial(_flash_body, tq=tq),
        grid_spec=pltpu.PrefetchScalarGridSpec(
            num_scalar_prefetch=2, grid=(b, len(pairs)),
            in_specs=[pl.BlockSpec((1, h, tq, dq), lambda bi, p, qi, kj: (bi, 0, qi[p], 0)),
                      pl.BlockSpec((1, h, tq, dq), lambda bi, p, qi, kj: (bi, 0, kj[p], 0)),
                      pl.BlockSpec((1, h, tq, dv), lambda bi, p, qi, kj: (bi, 0, kj[p], 0))],
            out_specs=pl.BlockSpec((1, tq, h * dv), lambda bi, p, qi, kj: (bi, qi[p], 0)),
            scratch_shapes=[pltpu.VMEM((h, tq, 1), F32), pltpu.VMEM((h, tq, 1), F32), pltpu.VMEM((h, tq, dv), F32)]),
        out_shape=jax.ShapeDtypeStruct((b, s, h * dv), BF16),
        compiler_params=_params("parallel", "arbitrary"),
        name="mla_flash",
    )(qi, kj, q, k, v)


def _layernorm(h, g, b):
    mu = jnp.mean(h, axis=-1, keepdims=True)
    hc = h - mu
    var = jnp.mean(hc * hc, axis=-1, keepdims=True)
    return hc * lax.rsqrt(var + LN_EPS) * g + b


def _out_ln_body(mix_ref, w_ref, x_ref, g_ref, b_ref, o_ref, *, alpha):
    acc = jnp.dot(mix_ref[...], w_ref[...], preferred_element_type=F32)
    o_ref[...] = _layernorm(alpha * x_ref[...] + acc, g_ref[...], b_ref[...])


def _out_ln(mix, w, x, g, b, alpha, tm=256):
    n, d = x.shape
    k = mix.shape[1]
    row = lambda c: pl.BlockSpec((tm, c), lambda i: (i, 0))
    vec = pl.BlockSpec((1, d), lambda i: (0, 0))
    return pl.pallas_call(
        functools.partial(_out_ln_body, alpha=alpha),
        grid=(n // tm,),
        in_specs=[row(k), pl.BlockSpec((k, d), lambda i: (0, 0)), row(d), vec, vec],
        out_specs=row(d),
        out_shape=jax.ShapeDtypeStruct((n, d), F32),
        compiler_params=_params("parallel"),
        name="out_proj_ln",
    )(mix, w, x, g, b)


def _router_body(x_ref, w_ref, b_ref, idx_ref, gate_ref):
    logits = lax.dot_general(w_ref[...], x_ref[...], (((1,), (1,)), ((), ())),
                             precision=HIGHEST, preferred_element_type=F32)
    scores = _sigmoid(logits)
    sel = scores + b_ref[...]
    ng = N_EXPERT_GROUPS
    sel_j = [sel[j * ng:(j + 1) * ng] for j in range(EXPERTS_PER_GROUP)]
    sc_j = [scores[j * ng:(j + 1) * ng] for j in range(EXPERTS_PER_GROUP)]
    grp = None
    for p in range(EXPERTS_PER_GROUP):
        for q in range(p + 1, EXPERTS_PER_GROUP):
            pair = sel_j[p] + sel_j[q]
            grp = pair if grp is None else jnp.maximum(grp, pair)
    gi = lax.broadcasted_iota(jnp.int32, grp.shape, 0)
    gmax = jnp.max(grp, axis=0, keepdims=True)
    gbest = jnp.min(jnp.where(grp == gmax, gi, ng), axis=0, keepdims=True)
    pick = gi == gbest
    mem = [jnp.sum(jnp.where(pick, t, 0.0), axis=0, keepdims=True) for t in sel_j]
    msc = [jnp.sum(jnp.where(pick, t, 0.0), axis=0, keepdims=True) for t in sc_j]

    def first_argmax(vals, exclude):
        best = jnp.full_like(vals[0], -jnp.inf)
        bi = jnp.zeros(vals[0].shape, jnp.int32)
        bs = jnp.zeros_like(vals[0])
        for j in range(EXPERTS_PER_GROUP):
            ok = vals[j] > best
            if exclude is not None:
                ok = ok & (exclude != j)
            best = jnp.where(ok, vals[j], best)
            bi = jnp.where(ok, j, bi)
            bs = jnp.where(ok, msc[j], bs)
        return bi, bs

    i1, s1 = first_argmax(mem, None)
    i2, s2 = first_argmax(mem, i1)
    tot = s1 + s2
    base = gbest * EXPERTS_PER_GROUP
    idx_ref[0:1, :] = base + i1
    idx_ref[1:2, :] = base + i2
    gate_ref[0:1, :] = s1 / tot
    gate_ref[1:2, :] = s2 / tot


def _router(x, w_t, bias, tm=512):
    n, d = x.shape
    return pl.pallas_call(
        _router_body,
        grid=(n // tm,),
        in_specs=[pl.BlockSpec((tm, d), lambda i: (i, 0)),
                  pl.BlockSpec((N_EXPERTS, d), lambda i: (0, 0)),
                  pl.BlockSpec((N_EXPERTS, 1), lambda i: (0, 0))],
        out_specs=[pl.BlockSpec((TOP_K, tm), lambda i: (0, i)), pl.BlockSpec((TOP_K, tm), lambda i: (0, i))],
        out_shape=[jax.ShapeDtypeStruct((TOP_K, n), jnp.int32), jax.ShapeDtypeStruct((TOP_K, n), F32)],
        compiler_params=_params("parallel"),
        name="moe_router",
    )(x, w_t, bias)


def _expert_body(blk_e_ref, nvalid_ref, src_ref, x_hbm, gate_ref, wg_ref, wu_ref, wd_ref, out_hbm,
                 xg_ref, yo_ref, sem_in, sem_out, *, n_tokens, n_blocks):
    i = pl.program_id(0)
    slot = i & 1

    def gather(blk, sl, start):
        def body(r, c):
            if start:
                a = src_ref[blk * MOE_BLOCK + r]
                tok = jnp.where(a >= n_tokens, a - n_tokens, a)
                pltpu.make_async_copy(x_hbm.at[pl.ds(tok, 1)], xg_ref.at[sl, pl.ds(r, 1)], sem_in.at[sl]).start()
            else:
                pltpu.make_async_copy(x_hbm.at[pl.ds(0, 1)], xg_ref.at[sl, pl.ds(0, 1)], sem_in.at[sl]).wait()
            return c
        lax.fori_loop(0, nvalid_ref[blk], body, 0)

    def scatter(blk, sl, start):
        def body(r, c):
            if start:
                a = src_ref[blk * MOE_BLOCK + r]
                pltpu.make_async_copy(yo_ref.at[sl, pl.ds(r, 1)], out_hbm.at[pl.ds(a, 1)], sem_out.at[sl]).start()
            else:
                pltpu.make_async_copy(yo_ref.at[sl, pl.ds(0, 1)], out_hbm.at[pl.ds(0, 1)], sem_out.at[sl]).wait()
            return c
        lax.fori_loop(0, nvalid_ref[blk], body, 0)

    @pl.when(i == 0)
    def _():
        xg_ref[...] = jnp.zeros_like(xg_ref)
        gather(0, 0, True)

    @pl.when(i + 1 < n_blocks)
    def _():
        gather(i + 1, 1 - slot, True)

    gather(i, slot, False)

    @pl.when(i >= 2)
    def _():
        scatter(i - 2, slot, False)

    @pl.when(nvalid_ref[i] > 0)
    def _():
        xb = xg_ref[slot].astype(BF16)
        gt = jnp.dot(xb, wg_ref[0], preferred_element_type=F32)
        up = jnp.dot(xb, wu_ref[0], preferred_element_type=F32)
        hid = (gt * _sigmoid(gt) * up).astype(BF16)
        yo_ref[slot] = jnp.dot(hid, wd_ref[0], preferred_element_type=F32) * gate_ref[0]

    scatter(i, slot, True)

    @pl.when(i == n_blocks - 1)
    def _():
        scatter(i, slot, False)

        @pl.when(i >= 1)
        def _():
            scatter(i - 1, 1 - slot, False)


def _experts(blk_e, nvalid, src, x, gate_buf, wg, wu, wd):
    n, d = x.shape
    nb = blk_e.shape[0]
    wspec = lambda shp: pl.BlockSpec((1,) + shp, lambda i, be, nv, sr: (be[i], 0, 0))
    return pl.pallas_call(
        functools.partial(_expert_body, n_tokens=n, n_blocks=nb),
        grid_spec=pltpu.PrefetchScalarGridSpec(
            num_scalar_prefetch=3, grid=(nb,),
            in_specs=[pl.BlockSpec(memory_space=pl.ANY),
                      pl.BlockSpec((1, MOE_BLOCK, 1), lambda i, be, nv, sr: (i, 0, 0)),
                      wspec((d, D_EXPERT)), wspec((d, D_EXPERT)), wspec((D_EXPERT, d))],
            out_specs=pl.BlockSpec(memory_space=pl.ANY),
            scratch_shapes=[pltpu.VMEM((2, MOE_BLOCK, d), F32), pltpu.VMEM((2, MOE_BLOCK, d), F32),
                            pltpu.SemaphoreType.DMA((2,)), pltpu.SemaphoreType.DMA((2,))]),
        out_shape=jax.ShapeDtypeStruct((TOP_K * n, d), F32),
        compiler_params=_params("arbitrary"),
        name="moe_experts",
    )(blk_e, nvalid, src, x, gate_buf, wg, wu, wd)


def _dispatch(idx, gate):
    n = idx.shape[1]
    a_tot = TOP_K * n
    e_flat = idx.reshape(a_tot)
    onehot = (e_flat[:, None] == jnp.arange(N_EXPERTS, dtype=jnp.int32)[None, :]).astype(F32)
    chunk = 128
    oh = onehot.reshape(a_tot // chunk, chunk, N_EXPERTS)
    tri = (jnp.arange(chunk)[None, :] <= jnp.arange(chunk)[:, None]).astype(F32)
    within = jnp.einsum('ts,csn->ctn', tri, oh)
    tot = within[:, -1, :]
    csum = (within + (jnp.cumsum(tot, axis=0) - tot)[:, None, :]).reshape(a_tot, N_EXPERTS)
    rank = jnp.sum(onehot * csum, axis=1).astype(jnp.int32) - 1
    counts = csum[-1].astype(jnp.int32)
    padded = (counts + MOE_BLOCK - 1) // MOE_BLOCK * MOE_BLOCK
    pad_end = jnp.cumsum(padded)
    pad_start = pad_end - padded
    dest = pad_start[e_flat] + rank
    nb = (a_tot + N_EXPERTS * (MOE_BLOCK - 1) + MOE_BLOCK - 1) // MOE_BLOCK
    p_rows = nb * MOE_BLOCK
    src = jnp.zeros((p_rows,), jnp.int32).at[dest].set(jnp.arange(a_tot, dtype=jnp.int32))
    gate_buf = jnp.zeros((p_rows,), F32).at[dest].set(gate.reshape(a_tot)).reshape(nb, MOE_BLOCK, 1)
    blk_start = jnp.arange(nb, dtype=jnp.int32) * MOE_BLOCK
    blk_e = jnp.minimum(jnp.searchsorted(pad_end, blk_start, side='right'), N_EXPERTS - 1).astype(jnp.int32)
    nvalid = jnp.clip(pad_start[blk_e] + counts[blk_e] - blk_start, 0, MOE_BLOCK).astype(jnp.int32)
    return blk_e, nvalid, src, gate_buf


def _final_body(x_ref, ya_ref, yb_ref, g_ref, b_ref, p_ref, wp_ref, wg_ref, o_ref, *, alpha):
    x2 = _layernorm(alpha * x_ref[...] + (ya_ref[...] + yb_ref[...]), g_ref[...], b_ref[...])
    gate = jnp.dot(x2.astype(BF16), wg_ref[...], preferred_element_type=F32)
    proj = jnp.dot(p_ref[...], wp_ref[...], preferred_element_type=F32)
    o_ref[...] = x2 + _sigmoid(gate) * proj


def _final(x, y2, g, b, p, wp, wg, alpha, tm=256):
    n, d = x.shape
    nblk = n // tm
    row = pl.BlockSpec((tm, d), lambda i: (i, 0))
    vec = pl.BlockSpec((1, d), lambda i: (0, 0))
    return pl.pallas_call(
        functools.partial(_final_body, alpha=alpha),
        grid=(nblk,),
        in_specs=[row, row, pl.BlockSpec((tm, d), lambda i: (i + nblk, 0)), vec, vec,
                  pl.BlockSpec((tm, PLE_DIM), lambda i: (i, 0)),
                  pl.BlockSpec((PLE_DIM, d), lambda i: (0, 0)),
                  pl.BlockSpec((d, d), lambda i: (0, 0))],
        out_specs=row,
        out_shape=jax.ShapeDtypeStruct((n, d), F32),
        compiler_params=_params("parallel"),
        name="moe_combine_ln_ple",
    )(x, y2, y2, g, b, p, wp, wg)


def _rot_cols(w):
    half = MLA_ROPE_DIM // 2
    return jnp.concatenate([-w[..., half:], w[..., :half]], axis=-1)


def _pad_lanes(w, width=128):
    return jnp.pad(w, [(0, 0)] * (w.ndim - 1) + [(0, width - w.shape[-1])])


def _rope_tables(s):
    half = MLA_ROPE_DIM // 2
    inv = ROPE_THETA ** (-jnp.arange(half, dtype=F32) / half)
    ang = jnp.arange(s, dtype=F32)[:, None] * inv[None, :]
    cos = jnp.concatenate([jnp.cos(ang), jnp.cos(ang)], axis=-1)
    sin = jnp.concatenate([jnp.sin(ang), jnp.sin(ang)], axis=-1)
    return _pad_lanes(cos), _pad_lanes(sin)


def kernel(x, p, w_in, rwkv_mu, rwkv_w0, rwkv_w_up, rwkv_a0, rwkv_a_up, rwkv_g_up, rwkv_k_k, rwkv_k_a, rwkv_r_k, rwkv_gn_g, rwkv_gn_b, mla_qa_g, mla_w_uq, mla_kva_g, mla_w_ukv, w_out, ln1_g, ln1_b, router_w, router_b, moe_w_gate, moe_w_up, moe_w_down, ln2_g, ln2_b, ple_w_proj, ple_w_gate):
    batch, s, d = x.shape
    depth = w_in.shape[0]
    n = batch * s
    alpha = (2 * depth) ** 0.25
    assert s % (DIL_PATTERNS[-1][1] * DIL_BLOCK) == 0 and s % 512 == 0
    hh, he = RWKV_HEADS, RWKV_HEAD_DIM

    cos, sin = _rope_tables(s)
    slopes = jnp.exp2(-ALIBI_MAX_BIAS * jnp.arange(1, 13, dtype=F32) / 12)
    perm = np.array([4 * g + j for j in range(EXPERTS_PER_GROUP) for g in range(N_EXPERT_GROUPS)])
    router_wt = router_w.T[perm]
    router_bt = router_b[perm].reshape(N_EXPERTS, 1)

    xf = x.reshape(n, d)
    for li in range(depth):
        wi = w_in[li]
        off = np.cumsum([0, RWKV_IN_DIM, DIL_QKV_DIM, DIL_QKV_DIM, DIL_QKV_DIM, MLA_Q_LORA, MLA_KV_LORA, MLA_ROPE_DIM])
        w_kr = wi[:, off[6]:off[7]]
        w_rest = jnp.concatenate([wi[:, off[1]:off[6]], _pad_lanes(w_kr), _pad_lanes(_rot_cols(w_kr))], axis=1)
        xb = xf.astype(BF16)
        z_r = _matmul(xb, wi[:, :RWKV_IN_DIM].astype(BF16), F32, 1024, 512)
        z_o = _matmul(xb, w_rest.astype(BF16), BF16, 1024, 512)

        w_up_pad = jnp.pad(rwkv_w_up[li], ((0, 64), (0, 0)))
        a_up_pad = jnp.pad(rwkv_a_up[li], ((64, 0), (0, 0)))
        row = lambda t: t.reshape(1, -1)
        prep = _rwkv_prep(z_r.reshape(batch, s, RWKV_IN_DIM), row(rwkv_mu[li]), row(rwkv_w0[li]), row(rwkv_a0[li]),
                          row(rwkv_k_k[li]), row(rwkv_k_a[li]), w_up_pad, a_up_pad, rwkv_g_up[li])
        heads = lambda t: t.reshape(batch, s, hh, he).transpose(0, 2, 1, 3)
        hpar = lambda t: t.reshape(hh, 1, he)
        y_a = _rwkv_scan(*[heads(t) for t in prep], hpar(rwkv_r_k[li]), hpar(rwkv_gn_g[li]), hpar(rwkv_gn_b[li]))
        y_a = y_a.transpose(0, 2, 1, 3).reshape(n, RWKV_DIM).astype(BF16)

        prior = _dilated_group(slopes, z_o, 0, batch, s) + _dilated_group(slopes, z_o, 1, batch, s)
        y_b = _dilated_group(slopes, z_o, 2, batch, s, prior=prior)

        wq = mla_w_uq[li].reshape(MLA_Q_LORA, MLA_HEADS, MLA_QK_DIM)
        wq_rope = wq[:, :, MLA_NOPE_DIM:]
        wq_all = jnp.concatenate(
            [wq[:, :, :MLA_NOPE_DIM].reshape(MLA_Q_LORA, -1),
             jnp.concatenate([_pad_lanes(wq_rope), _pad_lanes(_rot_cols(wq_rope))], axis=-1).reshape(MLA_Q_LORA, -1)],
            axis=1).astype(BF16)
        q_c = _mla_q(z_o, row(mla_qa_g[li]), wq_all, cos, sin, batch)
        k_c, v_c = _mla_kv(z_o, row(mla_kva_g[li]), mla_w_ukv[li].astype(BF16), cos, sin, batch)
        y_c = _flash(q_c, k_c, v_c).reshape(n, MLA_OUT_DIM)

        mix = jnp.concatenate([y_a, y_b, y_c], axis=-1)
        x1 = _out_ln(mix, w_out[li].astype(BF16), xf, row(ln1_g[li]), row(ln1_b[li]), alpha)

        idx, gate = _router(x1, router_wt, router_bt)
        blk_e, nvalid, src, gate_buf = _dispatch(idx, gate)
        y2 = _experts(blk_e, nvalid, src, x1, gate_buf, moe_w_gate[li].astype(BF16), moe_w_up[li].astype(BF16),
                      moe_w_down[li].astype(BF16))

        xf = _final(x1, y2, row(ln2_g[li]), row(ln2_b[li]), p[li].reshape(n, PLE_DIM).astype(BF16),
                    ple_w_proj[li].astype(BF16), ple_w_gate[li].astype(BF16), alpha)
    return xf.reshape(batch, s, d)
```

```python
import functools
import math

import numpy as np
import jax
import jax.numpy as jnp
from jax import lax
from jax.experimental import pallas as pl
from jax.experimental.pallas import tpu as pltpu

F32 = jnp.float32
BF16 = jnp.bfloat16
HIGHEST = lax.Precision.HIGHEST

PLE_DIM = 256
RWKV_HEADS = 12
RWKV_HEAD_DIM = 64
RWKV_DIM = RWKV_HEADS * RWKV_HEAD_DIM
RWKV_LORA_PAD = 128
RWKV_GATE_LORA = 128
RWKV_IN_DIM = 3 * RWKV_DIM + RWKV_LORA_PAD + RWKV_GATE_LORA
RWKV_GN_EPS = 64e-5
RWKV_CHUNK = 64
DIL_PATTERNS = ((128, 1), (512, 4), (2048, 16))
DIL_GROUPS = 3
DIL_HEADS_PER_GROUP = 4
DIL_HEAD_DIM = 128
DIL_QKV_DIM = DIL_GROUPS * DIL_HEADS_PER_GROUP * DIL_HEAD_DIM
DIL_OUT_DIM = DIL_HEADS_PER_GROUP * DIL_HEAD_DIM
DIL_BLOCK = 128
ALIBI_MAX_BIAS = 8.0
MLA_HEADS = 6
MLA_NOPE_DIM = 128
MLA_ROPE_DIM = 64
MLA_V_DIM = 128
MLA_Q_LORA = 512
MLA_KV_LORA = 256
MLA_QK_DIM = MLA_NOPE_DIM + MLA_ROPE_DIM
MLA_OUT_DIM = MLA_HEADS * MLA_V_DIM
ROPE_THETA = 10000.0
N_EXPERTS = 32
N_EXPERT_GROUPS = 8
EXPERTS_PER_GROUP = 4
TOP_K = 2
D_EXPERT = 512
MOE_BLOCK = 128
ROW_UNROLL = 8
LN_EPS = 1e-5
RMS_EPS = 1e-6
NEG_INF = -1e30

ZO_DQ = 0
ZO_DK = DIL_QKV_DIM
ZO_DV = 2 * DIL_QKV_DIM
ZO_QA = 3 * DIL_QKV_DIM
ZO_KVA = ZO_QA + MLA_Q_LORA
ZO_KR = ZO_KVA + MLA_KV_LORA
ZO_DIM = ZO_KR + 256

V7X_VMEM_LIMIT_BYTES = 48 * 1024 * 1024


def _params(*sem):
    return pltpu.CompilerParams(dimension_semantics=sem, vmem_limit_bytes=V7X_VMEM_LIMIT_BYTES)


def _sigmoid(x):
    return 1.0 / (1.0 + jnp.exp(-x))


def _mm_body(x_ref, w_ref, o_ref):
    o_ref[...] = jnp.dot(x_ref[...], w_ref[...], preferred_element_type=F32).astype(o_ref.dtype)


def _matmul(x, w, out_dtype, tm, tn):
    m, k = x.shape
    n = w.shape[1]
    return pl.pallas_call(
        _mm_body,
        grid=(m // tm, n // tn),
        in_specs=[pl.BlockSpec((tm, k), lambda i, j: (i, 0)),
                  pl.BlockSpec((k, tn), lambda i, j: (0, j))],
        out_specs=pl.BlockSpec((tm, tn), lambda i, j: (i, j)),
        out_shape=jax.ShapeDtypeStruct((m, n), out_dtype),
        compiler_params=_params("parallel", "arbitrary"),
        name="in_proj",
    )(x, w)


def _bf16_parts(x, n):
    parts = []
    for _ in range(n):
        part = x.astype(BF16)
        parts.append(part)
        x = x - part.astype(F32)
    return parts


def _bdot(a, b, dims):
    return lax.dot_general(a.astype(BF16), b.astype(BF16), (dims, ((0,), (0,))), preferred_element_type=F32)


def _bnt(a, b):
    return _bdot(a, b, ((2,), (2,)))


def _bnn(a, b):
    return _bdot(a, b, ((2,), (1,)))


def _btn(a, b):
    return _bdot(a, b, ((1,), (1,)))


RWKV_PAIRS = RWKV_HEADS // 2
PAIR_LANES = 2 * RWKV_HEAD_DIM


def _rwkv_body(z_ref, mu_ref, w0_ref, a0_ref, kk_ref, ka_ref, wup_ref, aup_ref, gup_ref, rk_ref, gng_ref, gnb_ref,
               y_ref, st_ref, carry_ref):
    c = pl.program_id(1)
    t, d, np_, pl_ = RWKV_CHUNK, RWKV_DIM, RWKV_PAIRS, PAIR_LANES

    @pl.when(c == 0)
    def _():
        st_ref[...] = jnp.zeros_like(st_ref)
        carry_ref[...] = jnp.zeros_like(carry_ref)

    z = z_ref[0]
    row = lax.broadcasted_iota(jnp.int32, z.shape, 0)
    zprev = jnp.where(row == 0, carry_ref[...], pltpu.roll(z, 1, axis=0))
    carry_ref[...] = z[t - 1:t, :]
    zs = z + (zprev - z) * mu_ref[...]
    r_w, k_w, v_w = zs[:, 0:d], zs[:, d:2 * d], zs[:, 2 * d:3 * d]
    lora = zs[:, 3 * d:3 * d + RWKV_LORA_PAD]
    gd = zs[:, 3 * d + RWKV_LORA_PAD:]
    th_hi, th_lo = _bf16_parts(jnp.tanh(lora), 2)
    u = (w0_ref[...] + jnp.dot(th_hi, wup_ref[0], preferred_element_type=F32)
         + jnp.dot(th_hi, wup_ref[1], preferred_element_type=F32)
         + jnp.dot(th_lo, wup_ref[0], preferred_element_type=F32))
    softplus = jnp.maximum(-u, 0.0) + jnp.log(1.0 + jnp.exp(-jnp.abs(u)))
    lw_w = -jnp.exp(-softplus - 0.5)
    a_w = _sigmoid(a0_ref[...] + jnp.dot(lora.astype(BF16), aup_ref[...], preferred_element_type=F32))
    g_w = jnp.dot(_sigmoid(gd).astype(BF16), gup_ref[...], preferred_element_type=F32)
    kmod_w = k_w * (1.0 + (a_w - 1.0) * ka_ref[...])
    kk_w = k_w * kk_ref[...]
    ti = lax.broadcasted_iota(jnp.int32, (t, t), 0)
    si = lax.broadcasted_iota(jnp.int32, (t, t), 1)
    incl = si <= ti
    strict = si < ti
    tri = incl.astype(BF16)
    lp_w = sum(jnp.dot(tri, part, preferred_element_type=F32) for part in _bf16_parts(lw_w, 3))

    pairs = lambda x: jnp.stack([x[:, p * pl_:(p + 1) * pl_] for p in range(np_)], axis=0)
    r, k, v, a, lw, lp, g = (pairs(x) for x in (r_w, kmod_w, v_w, a_w, lw_w, lp_w, g_w))
    li = lax.broadcasted_iota(jnp.int32, (pl_, pl_), 0)
    lj = lax.broadcasted_iota(jnp.int32, (pl_, pl_), 1)
    same_head = (li // RWKV_HEAD_DIM) == (lj // RWKV_HEAD_DIM)
    head_ones = same_head.astype(BF16)

    def head_sum(x):
        parts = _bf16_parts(x.reshape(np_ * t, pl_), 2)
        return sum(jnp.dot(part, head_ones, preferred_element_type=F32) for part in parts).reshape(np_, t, pl_)

    kk = pairs(kk_w)
    kk = kk / jnp.maximum(jnp.sqrt(head_sum(kk * kk)), 1e-12)
    lp_end = lp[:, t - 1:t, :]
    p_inv = jnp.exp(-lp)
    at = -kk * jnp.exp(lp - lw)
    bt = kk * a * p_inv
    kt = k * p_inv
    rt = r * jnp.exp(lp)
    to_end = jnp.exp(lp_end - lp)
    b_end = kk * a * to_end
    k_end = k * to_end

    lane = lax.broadcasted_iota(jnp.int32, (1, 1, pl_), 2)
    m0 = (lane < RWKV_HEAD_DIM).astype(F32)
    msk = jnp.concatenate([jnp.broadcast_to(m0, (np_, 1, pl_)), jnp.broadcast_to(1.0 - m0, (np_, 1, pl_))], axis=0)
    dup = lambda x: jnp.concatenate([x, x], axis=0)
    fold = lambda x: x[:np_] + x[np_:]
    lhs_a = dup(at) * msk
    lhs_r = dup(rt) * msk
    v2 = dup(v) * msk
    ar = jnp.concatenate([lhs_a, lhs_r], axis=1)
    x_b = _bnt(ar, dup(bt))
    x_k = _bnt(ar, dup(kt))
    a_ab = jnp.where(strict[None], x_b[:, :t], 0.0)
    a_rb = jnp.where(incl[None], x_b[:, t:], 0.0)
    a_ak = jnp.where(strict[None], x_k[:, :t], 0.0)
    a_rk = jnp.where(incl[None], x_k[:, t:], 0.0)

    sub = 16
    same_blk = ((ti // sub) == (si // sub))[None]
    eye = (ti == si).astype(F32)[None]
    ld = jnp.where(same_blk, a_ab, 0.0)
    lo = a_ab - ld
    dinv = eye + ld
    pw = ld
    for _ in range(3):
        pw = _bnn(pw, pw)
        dinv = dinv + _bnn(dinv, pw)
    n1 = _bnn(dinv, lo)
    n2 = _bnn(n1, n1)
    tinv = eye + n1 + n2 + _bnn(n1, n2)
    tinv = _bnn(tinv, dinv)

    akv = _bnn(a_ak, v2)
    wu = _bnn(tinv, jnp.concatenate([lhs_a, akv], axis=2))
    qy = _bnn(a_rb, wu)
    q = fold(lhs_r + qy[:, :, :pl_])
    y0 = fold(qy[:, :, pl_:] + _bnn(a_rk, v2))
    wt = fold(wu[:, :, :pl_])
    u0 = fold(wu[:, :, pl_:])

    s0 = st_ref[...]
    y = _bnn(q, s0) + y0
    diag_end = jnp.where((li == lj)[None], jnp.exp(lp_end), 0.0)
    m_t = diag_end + jnp.where(same_head[None], _btn(b_end, wt), 0.0)
    c_t = jnp.where(same_head[None], _btn(b_end, u0) + _btn(k_end, v), 0.0)
    st_ref[...] = _bnn(m_t, s0) + c_t

    inv_e = 1.0 / RWKV_HEAD_DIM
    yc = y - head_sum(y) * inv_e
    yv = head_sum(yc * yc) * inv_e
    yn = yc * lax.rsqrt(yv + RWKV_GN_EPS) * gng_ref[...] + gnb_ref[...]
    out = (yn + head_sum(r * k * rk_ref[...]) * v) * g
    for p in range(np_):
        y_ref[0, :, p * pl_:(p + 1) * pl_] = out[p].astype(y_ref.dtype)


def _rwkv_mix(z, mu, w0, a0, k_k, k_a, w_up_pad, a_up_pad, g_up, r_k, gn_g, gn_b):
    b, s, zin = z.shape
    d, t = RWKV_DIM, RWKV_CHUNK
    vec = lambda n: pl.BlockSpec((1, n), lambda bi, c: (0, 0))
    mat = lambda r: pl.BlockSpec((r, d), lambda bi, c: (0, 0))
    par = pl.BlockSpec((RWKV_PAIRS, 1, PAIR_LANES), lambda bi, c: (0, 0, 0))
    return pl.pallas_call(
        _rwkv_body,
        grid=(b, s // t),
        in_specs=[pl.BlockSpec((1, t, zin), lambda bi, c: (bi, c, 0)),
                  vec(zin), vec(d), vec(d), vec(d), vec(d),
                  pl.BlockSpec((2, RWKV_LORA_PAD, d), lambda bi, c: (0, 0, 0)),
                  mat(RWKV_LORA_PAD), mat(RWKV_GATE_LORA), par, par, par],
        out_specs=pl.BlockSpec((1, t, d), lambda bi, c: (bi, c, 0)),
        out_shape=jax.ShapeDtypeStruct((b, s, d), BF16),
        scratch_shapes=[pltpu.VMEM((RWKV_PAIRS, PAIR_LANES, PAIR_LANES), F32), pltpu.VMEM((1, zin), F32)],
        compiler_params=_params("parallel", "arbitrary"),
        name="rwkv_mix",
    )(z, mu, w0, a0, k_k, k_a, w_up_pad, a_up_pad, g_up, r_k, gn_g, gn_b)


def _dil_body(slopes_ref, q_ref, kc_ref, kp_ref, vc_ref, vp_ref, *rest, group, dil, n_sub, merge):
    head = pl.program_id(2)
    nblk = pl.program_id(3)
    bias = slopes_ref[group * DIL_HEADS_PER_GROUP + head] * float(dil)
    scale = DIL_HEAD_DIM ** -0.5
    nt = (((1,), (1,)), ((), ()))
    qi = lax.broadcasted_iota(jnp.int32, (DIL_BLOCK, DIL_BLOCK), 0)
    ki = lax.broadcasted_iota(jnp.int32, (DIL_BLOCK, DIL_BLOCK), 1)
    rel_c = qi - ki
    dist_c = bias * rel_c.astype(F32)
    dist_p = bias * (rel_c + DIL_BLOCK).astype(F32)
    if merge:
        o0_ref, l0_ref, o1_ref, l1_ref, y_ref = rest
    else:
        o_ref, lse_ref = rest
    for u in range(n_sub):
        rows = slice(u * DIL_BLOCK, (u + 1) * DIL_BLOCK)
        q = q_ref[rows, :]
        if u == 0:
            k_prev, v_prev = kp_ref[...], vp_ref[...]
            prev_lim = jnp.where(nblk == 0, -2 * DIL_BLOCK, 0)
        else:
            prows = slice((u - 1) * DIL_BLOCK, u * DIL_BLOCK)
            k_prev, v_prev = kc_ref[prows, :], vc_ref[prows, :]
            prev_lim = 0
        s_c = lax.dot_general(q, kc_ref[rows, :], nt, preferred_element_type=F32) * scale
        s_p = lax.dot_general(q, k_prev, nt, preferred_element_type=F32) * scale
        s_c = jnp.where(rel_c >= 0, s_c - dist_c, NEG_INF)
        s_p = jnp.where(rel_c <= prev_lim, s_p - dist_p, NEG_INF)
        m = jnp.maximum(jnp.max(s_c, axis=-1, keepdims=True), jnp.max(s_p, axis=-1, keepdims=True))
        e_c = jnp.exp(s_c - m)
        e_p = jnp.exp(s_p - m)
        den = jnp.sum(e_c, axis=-1, keepdims=True) + jnp.sum(e_p, axis=-1, keepdims=True)
        acc = (jnp.dot(e_c.astype(BF16), vc_ref[rows, :], preferred_element_type=F32)
               + jnp.dot(e_p.astype(BF16), v_prev, preferred_element_type=F32))
        o = acc / den
        lse = jnp.broadcast_to(m + jnp.log(den), (DIL_BLOCK, DIL_HEAD_DIM))
        if merge:
            l0, l1 = l0_ref[rows, :], l1_ref[rows, :]
            top = jnp.maximum(jnp.maximum(l0, l1), lse)
            w0, w1, w2 = jnp.exp(l0 - top), jnp.exp(l1 - top), jnp.exp(lse - top)
            y = (w0 * o0_ref[rows, :] + w1 * o1_ref[rows, :] + w2 * o) / (w0 + w1 + w2)
            y_ref[rows, :] = y.astype(y_ref.dtype)
        else:
            o_ref[rows, :] = o
            lse_ref[rows, :] = lse


def _dilated_group(slopes, zo, group, batch, s, prior=None):
    n = batch * s
    dil = DIL_PATTERNS[group][1]
    nb = s // dil // DIL_BLOCK
    n_sub = min(nb, 8)
    steps = nb // n_sub
    zcols = ZO_DIM // DIL_HEAD_DIM
    ocols = DIL_OUT_DIM // DIL_HEAD_DIM
    zv = zo.reshape(n // dil, dil * ZO_DIM)
    rows = n_sub * DIL_BLOCK

    def cur(col0):
        c = col0 // DIL_HEAD_DIM + group * DIL_HEADS_PER_GROUP
        return pl.BlockSpec((rows, DIL_HEAD_DIM), lambda b, r, h, i, sl: (b * steps + i, r * zcols + c + h))

    def prev(col0):
        c = col0 // DIL_HEAD_DIM + group * DIL_HEADS_PER_GROUP
        return pl.BlockSpec((DIL_BLOCK, DIL_HEAD_DIM),
                            lambda b, r, h, i, sl: (jnp.maximum(b * nb + i * n_sub - 1, 0), r * zcols + c + h))

    ospec = pl.BlockSpec((rows, DIL_HEAD_DIM), lambda b, r, h, i, sl: (b * steps + i, r * ocols + h))
    oshape = lambda dt: jax.ShapeDtypeStruct((n // dil, dil * DIL_OUT_DIM), dt)
    merge = prior is not None
    extra = [t.reshape(n // dil, dil * DIL_OUT_DIM) for t in prior] if merge else []
    out = pl.pallas_call(
        functools.partial(_dil_body, group=group, dil=dil, n_sub=n_sub, merge=merge),
        grid_spec=pltpu.PrefetchScalarGridSpec(
            num_scalar_prefetch=1, grid=(batch, dil, DIL_HEADS_PER_GROUP, steps),
            in_specs=[cur(ZO_DQ), cur(ZO_DK), prev(ZO_DK), cur(ZO_DV), prev(ZO_DV)] + [ospec] * len(extra),
            out_specs=ospec if merge else [ospec, ospec]),
        out_shape=oshape(BF16) if merge else [oshape(F32), oshape(F32)],
        compiler_params=_params("parallel", "parallel", "parallel", "arbitrary"),
        name="dilated_attn_g%d" % group,
    )(slopes, zv, zv, zv, zv, zv, *extra)
    if merge:
        return out.reshape(n, DIL_OUT_DIM)
    return [t.reshape(n, DIL_OUT_DIM) for t in out]


def _rms(x_bf16, g):
    x = x_bf16.astype(F32)
    return (x * lax.rsqrt(jnp.mean(x * x, axis=-1, keepdims=True) + RMS_EPS) * g).astype(BF16)


def _mla_q_body(x_ref, g_ref, w_ref, cos_ref, sin_ref, q_ref):
    acc = jnp.dot(_rms(x_ref[...], g_ref[...]), w_ref[...], preferred_element_type=F32)
    scale = MLA_QK_DIM ** -0.5
    cos, sin = cos_ref[...], sin_ref[...]
    nope_w = MLA_HEADS * MLA_NOPE_DIM
    for h in range(MLA_HEADS):
        q_ref[0, h, :, 0:MLA_NOPE_DIM] = (acc[:, h * 128:(h + 1) * 128] * scale).astype(BF16)
        base = nope_w + h * 256
        rope = acc[:, base:base + 128] * cos + acc[:, base + 128:base + 256] * sin
        q_ref[0, h, :, MLA_NOPE_DIM:MLA_QK_DIM] = (rope[:, 0:MLA_ROPE_DIM] * scale).astype(BF16)


def _mla_q(zo, g, w, cos, sin, batch, tm=512):
    n = zo.shape[0]
    s = n // batch
    nblk = s // tm
    return pl.pallas_call(
        _mla_q_body,
        grid=(batch, nblk),
        in_specs=[pl.BlockSpec((tm, MLA_Q_LORA), lambda b, i: (b * nblk + i, ZO_QA // MLA_Q_LORA)),
                  pl.BlockSpec((1, MLA_Q_LORA), lambda b, i: (0, 0)),
                  pl.BlockSpec(w.shape, lambda b, i: (0, 0)),
                  pl.BlockSpec((tm, 128), lambda b, i: (i, 0)),
                  pl.BlockSpec((tm, 128), lambda b, i: (i, 0))],
        out_specs=pl.BlockSpec((1, MLA_HEADS, tm, MLA_QK_DIM), lambda b, i: (b, 0, i, 0)),
        out_shape=jax.ShapeDtypeStruct((batch, MLA_HEADS, s, MLA_QK_DIM), BF16),
        compiler_params=_params("parallel", "parallel"),
        name="mla_q_proj",
    )(zo, g, w, cos, sin)


def _mla_kv_body(x_ref, kr_ref, g_ref, w_ref, cos_ref, sin_ref, k_ref, v_ref):
    acc = jnp.dot(_rms(x_ref[...], g_ref[...]), w_ref[...], preferred_element_type=F32)
    kr = kr_ref[...].astype(F32)
    rope = (kr[:, 0:128] * cos_ref[...] + kr[:, 128:256] * sin_ref[...])[:, 0:MLA_ROPE_DIM].astype(BF16)
    for h in range(MLA_HEADS):
        k_ref[0, h, :, 0:MLA_NOPE_DIM] = acc[:, h * 256:h * 256 + 128].astype(BF16)
        k_ref[0, h, :, MLA_NOPE_DIM:MLA_QK_DIM] = rope
        v_ref[0, h] = acc[:, h * 256 + 128:(h + 1) * 256].astype(BF16)


def _mla_kv(zo, g, w, cos, sin, batch, tm=512):
    n = zo.shape[0]
    s = n // batch
    nblk = s // tm
    return pl.pallas_call(
        _mla_kv_body,
        grid=(batch, nblk),
        in_specs=[pl.BlockSpec((tm, MLA_KV_LORA), lambda b, i: (b * nblk + i, ZO_KVA // MLA_KV_LORA)),
                  pl.BlockSpec((tm, 256), lambda b, i: (b * nblk + i, ZO_KR // 256)),
                  pl.BlockSpec((1, MLA_KV_LORA), lambda b, i: (0, 0)),
                  pl.BlockSpec(w.shape, lambda b, i: (0, 0)),
                  pl.BlockSpec((tm, 128), lambda b, i: (i, 0)),
                  pl.BlockSpec((tm, 128), lambda b, i: (i, 0))],
        out_specs=[pl.BlockSpec((1, MLA_HEADS, tm, MLA_QK_DIM), lambda b, i: (b, 0, i, 0)),
                   pl.BlockSpec((1, MLA_HEADS, tm, MLA_V_DIM), lambda b, i: (b, 0, i, 0))],
        out_shape=[jax.ShapeDtypeStruct((batch, MLA_HEADS, s, MLA_QK_DIM), BF16),
                   jax.ShapeDtypeStruct((batch, MLA_HEADS, s, MLA_V_DIM), BF16)],
        compiler_params=_params("parallel", "parallel"),
        name="mla_kv_proj",
    )(zo, zo, g, w, cos, sin)


def _flash_body(qi_ref, kj_ref, q_ref, k_ref, v_ref, o_ref, m_ref, l_ref, acc_ref, *, tq):
    p = pl.program_id(1)
    i = qi_ref[p]
    j = kj_ref[p]
    heads, dv = acc_ref.shape[0], acc_ref.shape[2]

    @pl.when(j == 0)
    def _():
        m_ref[...] = jnp.full_like(m_ref, NEG_INF)
        l_ref[...] = jnp.zeros_like(l_ref)
        acc_ref[...] = jnp.zeros_like(acc_ref)

    def update(masked):
        for h in range(heads):
            s = lax.dot_general(q_ref[0, h], k_ref[0, h], (((1,), (1,)), ((), ())), preferred_element_type=F32)
            if masked:
                qpos = lax.broadcasted_iota(jnp.int32, (tq, tq), 0)
                kpos = lax.broadcasted_iota(jnp.int32, (tq, tq), 1)
                s = jnp.where(kpos <= qpos, s, NEG_INF)
            m_old = m_ref[h]
            m_new = jnp.maximum(m_old, jnp.max(s, axis=-1, keepdims=True))
            alpha = jnp.exp(m_old - m_new)
            e = jnp.exp(s - jnp.concatenate([m_new] * (tq // dv), axis=1))
            l_ref[h] = alpha * l_ref[h] + jnp.sum(e, axis=-1, keepdims=True)
            acc_ref[h] = alpha * acc_ref[h] + jnp.dot(e.astype(BF16), v_ref[0, h], preferred_element_type=F32)
            m_ref[h] = m_new

    @pl.when(j < i)
    def _():
        update(False)

    @pl.when(j == i)
    def _():
        update(True)
        for h in range(heads):
            o_ref[0, :, h * dv:(h + 1) * dv] = (acc_ref[h] / l_ref[h]).astype(o_ref.dtype)


def _flash(q, k, v, tq=512):
    b, h, s, dq = q.shape
    dv = v.shape[-1]
    nq = s // tq
    pairs = [(i, j) for i in range(nq) for j in range(i + 1)]
    qi = jnp.asarray([pr[0] for pr in pairs], jnp.int32)
    kj = jnp.asarray([pr[1] for pr in pairs], jnp.int32)
    return pl.pallas_call(
        functools.partial(_flash_body, tq=tq),
        grid_spec=pltpu.PrefetchScalarGridSpec(
            num_scalar_prefetch=2, grid=(b, len(pairs)),
            in_specs=[pl.BlockSpec((1, h, tq, dq), lambda bi, p, qi, kj: (bi, 0, qi[p], 0)),
                      pl.BlockSpec((1, h, tq, dq), lambda bi, p, qi, kj: (bi, 0, kj[p], 0)),
                      pl.BlockSpec((1, h, tq, dv), lambda bi, p, qi, kj: (bi, 0, kj[p], 0))],
            out_specs=pl.BlockSpec((1, tq, h * dv), lambda bi, p, qi, kj: (bi, qi[p], 0)),
            scratch_shapes=[pltpu.VMEM((h, tq, dv), F32), pltpu.VMEM((h, tq, dv), F32), pltpu.VMEM((h, tq, dv), F32)]),
        out_shape=jax.ShapeDtypeStruct((b, s, h * dv), BF16),
        compiler_params=_params("parallel", "arbitrary"),
        name="mla_flash",
    )(qi, kj, q, k, v)


def _layernorm(h, g, b):
    mu = jnp.mean(h, axis=-1, keepdims=True)
    hc = h - mu
    var = jnp.mean(hc * hc, axis=-1, keepdims=True)
    return hc * lax.rsqrt(var + LN_EPS) * g + b


def _out_ln_body(mix_ref, w_ref, x_ref, g_ref, b_ref, o_ref, *, alpha):
    acc = jnp.dot(mix_ref[...], w_ref[...], preferred_element_type=F32)
    o_ref[...] = _layernorm(alpha * x_ref[...] + acc, g_ref[...], b_ref[...])


def _out_ln(mix, w, x, g, b, alpha, tm=256):
    n, d = x.shape
    k = mix.shape[1]
    row = lambda c: pl.BlockSpec((tm, c), lambda i: (i, 0))
    vec = pl.BlockSpec((1, d), lambda i: (0, 0))
    return pl.pallas_call(
        functools.partial(_out_ln_body, alpha=alpha),
        grid=(n // tm,),
        in_specs=[row(k), pl.BlockSpec((k, d), lambda i: (0, 0)), row(d), vec, vec],
        out_specs=row(d),
        out_shape=jax.ShapeDtypeStruct((n, d), F32),
        compiler_params=_params("parallel"),
        name="out_proj_ln",
    )(mix, w, x, g, b)


def _router_body(x_ref, w_ref, b_ref, idx_ref, gate_ref):
    logits = lax.dot_general(w_ref[...], x_ref[...], (((1,), (1,)), ((), ())),
                             precision=HIGHEST, preferred_element_type=F32)
    scores = _sigmoid(logits)
    sel = scores + b_ref[...]
    ng = N_EXPERT_GROUPS
    sel_j = [sel[j * ng:(j + 1) * ng] for j in range(EXPERTS_PER_GROUP)]
    sc_j = [scores[j * ng:(j + 1) * ng] for j in range(EXPERTS_PER_GROUP)]
    grp = None
    for p in range(EXPERTS_PER_GROUP):
        for q in range(p + 1, EXPERTS_PER_GROUP):
            pair = sel_j[p] + sel_j[q]
            grp = pair if grp is None else jnp.maximum(grp, pair)
    gi = lax.broadcasted_iota(jnp.int32, grp.shape, 0)
    gmax = jnp.max(grp, axis=0, keepdims=True)
    gbest = jnp.min(jnp.where(grp == gmax, gi, ng), axis=0, keepdims=True)
    pick = gi == gbest
    mem = [jnp.sum(jnp.where(pick, t, 0.0), axis=0, keepdims=True) for t in sel_j]
    msc = [jnp.sum(jnp.where(pick, t, 0.0), axis=0, keepdims=True) for t in sc_j]

    def first_argmax(vals, exclude):
        best = jnp.full_like(vals[0], -jnp.inf)
        bi = jnp.zeros(vals[0].shape, jnp.int32)
        bs = jnp.zeros_like(vals[0])
        for j in range(EXPERTS_PER_GROUP):
            ok = vals[j] > best
            if exclude is not None:
                ok = ok & (exclude != j)
            best = jnp.where(ok, vals[j], best)
            bi = jnp.where(ok, j, bi)
            bs = jnp.where(ok, msc[j], bs)
        return bi, bs

    i1, s1 = first_argmax(mem, None)
    i2, s2 = first_argmax(mem, i1)
    tot = s1 + s2
    base = gbest * EXPERTS_PER_GROUP
    idx_ref[0:1, :] = base + i1
    idx_ref[1:2, :] = base + i2
    gate_ref[0:1, :] = s1 / tot
    gate_ref[1:2, :] = s2 / tot


def _router(x, w_t, bias, tm=512):
    n, d = x.shape
    return pl.pallas_call(
        _router_body,
        grid=(n // tm,),
        in_specs=[pl.BlockSpec((tm, d), lambda i: (i, 0)),
                  pl.BlockSpec((N_EXPERTS, d), lambda i: (0, 0)),
                  pl.BlockSpec((N_EXPERTS, 1), lambda i: (0, 0))],
        out_specs=[pl.BlockSpec((TOP_K, tm), lambda i: (0, i)), pl.BlockSpec((TOP_K, tm), lambda i: (0, i))],
        out_shape=[jax.ShapeDtypeStruct((TOP_K, n), jnp.int32), jax.ShapeDtypeStruct((TOP_K, n), F32)],
        compiler_params=_params("parallel"),
        name="moe_router",
    )(x, w_t, bias)


def _expert_body(blk_e_ref, nvalid_ref, src_ref, x_hbm, wg_ref, wu_ref, wd_ref, out_hbm,
                 xg_ref, yo_ref, sem_in, sem_out, *, n_tokens, n_blocks):
    i = pl.program_id(0)
    slot = i & 1

    def for_rows(count, fn):
        def group(gi, c):
            for u in range(ROW_UNROLL):
                fn(gi * ROW_UNROLL + u, u)
            return c
        full = count // ROW_UNROLL
        lax.fori_loop(0, full, group, 0)
        for u in range(ROW_UNROLL - 1):
            @pl.when(full * ROW_UNROLL + u < count)
            def _():
                fn(full * ROW_UNROLL + u, u)

    def gather(blk, sl, start):
        def row(r, u):
            if start:
                a = src_ref[blk * MOE_BLOCK + r]
                tok = jnp.where(a >= n_tokens, a - n_tokens, a)
                pltpu.make_async_copy(x_hbm.at[pl.ds(tok, 1)], xg_ref.at[sl, pl.ds(r, 1)],
                                      sem_in.at[sl]).start(priority=u % 2)
            else:
                pltpu.make_async_copy(x_hbm.at[pl.ds(0, 1)], xg_ref.at[sl, pl.ds(0, 1)], sem_in.at[sl]).wait()
        for_rows(nvalid_ref[blk], row)

    def scatter(blk, sl, start):
        def row(r, u):
            if start:
                a = src_ref[blk * MOE_BLOCK + r]
                pltpu.make_async_copy(yo_ref.at[sl, pl.ds(r, 1)], out_hbm.at[pl.ds(a, 1)],
                                      sem_out.at[sl]).start(priority=u % 2)
            else:
                pltpu.make_async_copy(yo_ref.at[sl, pl.ds(0, 1)], out_hbm.at[pl.ds(0, 1)], sem_out.at[sl]).wait()
        for_rows(nvalid_ref[blk], row)

    @pl.when(i == 0)
    def _():
        xg_ref[...] = jnp.zeros_like(xg_ref)
        gather(0, 0, True)

    @pl.when(i + 1 < n_blocks)
    def _():
        gather(i + 1, 1 - slot, True)

    gather(i, slot, False)

    @pl.when(i >= 2)
    def _():
        scatter(i - 2, slot, False)

    @pl.when(nvalid_ref[i] > 0)
    def _():
        xb = xg_ref[slot].astype(BF16)
        gt = jnp.dot(xb, wg_ref[0], preferred_element_type=F32)
        up = jnp.dot(xb, wu_ref[0], preferred_element_type=F32)
        hid = (gt * _sigmoid(gt) * up).astype(BF16)
        yo_ref[slot] = jnp.dot(hid, wd_ref[0], preferred_element_type=F32)

    scatter(i, slot, True)

    @pl.when(i == n_blocks - 1)
    def _():
        scatter(i, slot, False)

        @pl.when(i >= 1)
        def _():
            scatter(i - 1, 1 - slot, False)


def _experts(blk_e, nvalid, src, x, wg, wu, wd):
    n, d = x.shape
    nb = blk_e.shape[0]
    wspec = lambda shp: pl.BlockSpec((1,) + shp, lambda i, be, nv, sr: (be[i], 0, 0))
    return pl.pallas_call(
        functools.partial(_expert_body, n_tokens=n, n_blocks=nb),
        grid_spec=pltpu.PrefetchScalarGridSpec(
            num_scalar_prefetch=3, grid=(nb,),
            in_specs=[pl.BlockSpec(memory_space=pl.ANY),
                      wspec((d, D_EXPERT)), wspec((d, D_EXPERT)), wspec((D_EXPERT, d))],
            out_specs=pl.BlockSpec(memory_space=pl.ANY),
            scratch_shapes=[pltpu.VMEM((2, MOE_BLOCK, d), F32), pltpu.VMEM((2, MOE_BLOCK, d), F32),
                            pltpu.SemaphoreType.DMA((2,)), pltpu.SemaphoreType.DMA((2,))]),
        out_shape=jax.ShapeDtypeStruct((TOP_K * n, d), F32),
        compiler_params=_params("arbitrary"),
        name="moe_experts",
    )(blk_e, nvalid, src, x, wg, wu, wd)


def _dispatch(idx):
    n = idx.shape[1]
    a_tot = TOP_K * n
    e_flat = idx.reshape(a_tot)
    onehot = (e_flat[:, None] == jnp.arange(N_EXPERTS, dtype=jnp.int32)[None, :]).astype(F32)
    chunk = 128
    oh = onehot.reshape(a_tot // chunk, chunk, N_EXPERTS)
    tri = (jnp.arange(chunk)[None, :] <= jnp.arange(chunk)[:, None]).astype(F32)
    within = jnp.einsum('ts,csn->ctn', tri, oh)
    tot = within[:, -1, :]
    csum = (within + (jnp.cumsum(tot, axis=0) - tot)[:, None, :]).reshape(a_tot, N_EXPERTS)
    rank = jnp.sum(onehot * csum, axis=1).astype(jnp.int32) - 1
    counts = csum[-1].astype(jnp.int32)
    padded = (counts + MOE_BLOCK - 1) // MOE_BLOCK * MOE_BLOCK
    pad_end = jnp.cumsum(padded)
    pad_start = pad_end - padded
    dest = pad_start[e_flat] + rank
    nb = (a_tot + N_EXPERTS * (MOE_BLOCK - 1) + MOE_BLOCK - 1) // MOE_BLOCK
    p_rows = nb * MOE_BLOCK
    src = jnp.zeros((p_rows,), jnp.int32).at[dest].set(jnp.arange(a_tot, dtype=jnp.int32))
    blk_start = jnp.arange(nb, dtype=jnp.int32) * MOE_BLOCK
    blk_e = jnp.sum((pad_end[None, :] <= blk_start[:, None]).astype(jnp.int32), axis=1)
    blk_e = jnp.minimum(blk_e, N_EXPERTS - 1)
    nvalid = jnp.clip(pad_start[blk_e] + counts[blk_e] - blk_start, 0, MOE_BLOCK).astype(jnp.int32)
    return blk_e, nvalid, src


def _final_body(x_ref, ya_ref, yb_ref, ga_ref, gb_ref, g_ref, b_ref, p_ref, wp_ref, wg_ref, o_ref, *, alpha):
    ffn = ya_ref[...] * ga_ref[...] + yb_ref[...] * gb_ref[...]
    x2 = _layernorm(alpha * x_ref[...] + ffn, g_ref[...], b_ref[...])
    gate = jnp.dot(x2.astype(BF16), wg_ref[...], preferred_element_type=F32)
    proj = jnp.dot(p_ref[...], wp_ref[...], preferred_element_type=F32)
    o_ref[...] = x2 + _sigmoid(gate) * proj


def _final(x, y2, gates, g, b, p, wp, wg, alpha, tm=256):
    n, d = x.shape
    nblk = n // tm
    row = pl.BlockSpec((tm, d), lambda i: (i, 0))
    vec = pl.BlockSpec((1, d), lambda i: (0, 0))
    return pl.pallas_call(
        functools.partial(_final_body, alpha=alpha),
        grid=(nblk,),
        in_specs=[row, row, pl.BlockSpec((tm, d), lambda i: (i + nblk, 0)),
                  pl.BlockSpec((tm, 1), lambda i: (i, 0)), pl.BlockSpec((tm, 1), lambda i: (i + nblk, 0)), vec, vec,
                  pl.BlockSpec((tm, PLE_DIM), lambda i: (i, 0)),
                  pl.BlockSpec((PLE_DIM, d), lambda i: (0, 0)),
                  pl.BlockSpec((d, d), lambda i: (0, 0))],
        out_specs=row,
        out_shape=jax.ShapeDtypeStruct((n, d), F32),
        compiler_params=_params("parallel"),
        name="moe_combine_ln_ple",
    )(x, y2, y2, gates, gates, g, b, p, wp, wg)


def _rot_cols(w):
    half = MLA_ROPE_DIM // 2
    return jnp.concatenate([-w[..., half:], w[..., :half]], axis=-1)


def _pad_lanes(w, width=128):
    return jnp.pad(w, [(0, 0)] * (w.ndim - 1) + [(0, width - w.shape[-1])])


def _rope_tables(s):
    half = MLA_ROPE_DIM // 2
    inv = ROPE_THETA ** (-jnp.arange(half, dtype=F32) / half)
    ang = jnp.arange(s, dtype=F32)[:, None] * inv[None, :]
    cos = jnp.concatenate([jnp.cos(ang), jnp.cos(ang)], axis=-1)
    sin = jnp.concatenate([jnp.sin(ang), jnp.sin(ang)], axis=-1)
    return _pad_lanes(cos), _pad_lanes(sin)


def kernel(x, p, w_in, rwkv_mu, rwkv_w0, rwkv_w_up, rwkv_a0, rwkv_a_up, rwkv_g_up, rwkv_k_k, rwkv_k_a, rwkv_r_k, rwkv_gn_g, rwkv_gn_b, mla_qa_g, mla_w_uq, mla_kva_g, mla_w_ukv, w_out, ln1_g, ln1_b, router_w, router_b, moe_w_gate, moe_w_up, moe_w_down, ln2_g, ln2_b, ple_w_proj, ple_w_gate):
    batch, s, d = x.shape
    depth = w_in.shape[0]
    n = batch * s
    alpha = (2 * depth) ** 0.25
    assert s % (DIL_PATTERNS[-1][1] * DIL_BLOCK) == 0 and s % 512 == 0
    hh, he = RWKV_HEADS, RWKV_HEAD_DIM

    cos, sin = _rope_tables(s)
    slopes = jnp.exp2(-ALIBI_MAX_BIAS * jnp.arange(1, 13, dtype=F32) / 12)
    perm = np.array([4 * g + j for j in range(EXPERTS_PER_GROUP) for g in range(N_EXPERT_GROUPS)])
    router_wt = router_w.T[perm]
    router_bt = router_b[perm].reshape(N_EXPERTS, 1)

    xf = x.reshape(n, d)
    for li in range(depth):
        wi = w_in[li]
        off = np.cumsum([0, RWKV_IN_DIM, DIL_QKV_DIM, DIL_QKV_DIM, DIL_QKV_DIM, MLA_Q_LORA, MLA_KV_LORA, MLA_ROPE_DIM])
        w_kr = wi[:, off[6]:off[7]]
        w_rest = jnp.concatenate([wi[:, off[1]:off[6]], _pad_lanes(w_kr), _pad_lanes(_rot_cols(w_kr))], axis=1)
        xb = xf.astype(BF16)
        z_r = _matmul(xb, wi[:, :RWKV_IN_DIM].astype(BF16), F32, 1024, 512)
        z_o = _matmul(xb, w_rest.astype(BF16), BF16, 1024, 512)

        w_up_pad = jnp.pad(rwkv_w_up[li], ((0, 64), (0, 0)))
        w_up_hi = w_up_pad.astype(BF16)
        w_up_pad = jnp.stack([w_up_hi, (w_up_pad - w_up_hi.astype(F32)).astype(BF16)])
        a_up_pad = jnp.pad(rwkv_a_up[li], ((64, 0), (0, 0))).astype(BF16)
        row = lambda t: t.reshape(1, -1)
        hpar = lambda t: t.reshape(RWKV_PAIRS, 1, PAIR_LANES)
        y_a = _rwkv_mix(z_r.reshape(batch, s, RWKV_IN_DIM), row(rwkv_mu[li]), row(rwkv_w0[li]), row(rwkv_a0[li]),
                        row(rwkv_k_k[li]), row(rwkv_k_a[li]), w_up_pad, a_up_pad, rwkv_g_up[li].astype(BF16),
                        hpar(rwkv_r_k[li]), hpar(rwkv_gn_g[li]), hpar(rwkv_gn_b[li])).reshape(n, RWKV_DIM)

        prior = _dilated_group(slopes, z_o, 0, batch, s) + _dilated_group(slopes, z_o, 1, batch, s)
        y_b = _dilated_group(slopes, z_o, 2, batch, s, prior=prior)

        wq = mla_w_uq[li].reshape(MLA_Q_LORA, MLA_HEADS, MLA_QK_DIM)
        wq_rope = wq[:, :, MLA_NOPE_DIM:]
        wq_all = jnp.concatenate(
            [wq[:, :, :MLA_NOPE_DIM].reshape(MLA_Q_LORA, -1),
             jnp.concatenate([_pad_lanes(wq_rope), _pad_lanes(_rot_cols(wq_rope))], axis=-1).reshape(MLA_Q_LORA, -1)],
            axis=1).astype(BF16)
        q_c = _mla_q(z_o, row(mla_qa_g[li]), wq_all, cos, sin, batch)
        k_c, v_c = _mla_kv(z_o, row(mla_kva_g[li]), mla_w_ukv[li].astype(BF16), cos, sin, batch)
        y_c = _flash(q_c, k_c, v_c).reshape(n, MLA_OUT_DIM)

        mix = jnp.concatenate([y_a, y_b, y_c], axis=-1)
        x1 = _out_ln(mix, w_out[li].astype(BF16), xf, row(ln1_g[li]), row(ln1_b[li]), alpha)

        idx, gate = _router(x1, router_wt, router_bt)
        blk_e, nvalid, src = _dispatch(idx)
        y2 = _experts(blk_e, nvalid, src, x1, moe_w_gate[li].astype(BF16), moe_w_up[li].astype(BF16),
                      moe_w_down[li].astype(BF16))

        xf = _final(x1, y2, gate.reshape(TOP_K * n, 1), row(ln2_g[li]), row(ln2_b[li]),
                    p[li].reshape(n, PLE_DIM).astype(BF16),
                    ple_w_proj[li].astype(BF16), ple_w_gate[li].astype(BF16), alpha)
    return xf.reshape(batch, s, d)
```

```python
import functools
import math

import numpy as np
import jax
import jax.numpy as jnp
from jax import lax
from jax.experimental import pallas as pl
from jax.experimental.pallas import tpu as pltpu

F32 = jnp.float32
BF16 = jnp.bfloat16
HIGHEST = lax.Precision.HIGHEST

PLE_DIM = 256
RWKV_HEADS = 12
RWKV_HEAD_DIM = 64
RWKV_DIM = RWKV_HEADS * RWKV_HEAD_DIM
RWKV_LORA_PAD = 128
RWKV_GATE_LORA = 128
RWKV_IN_DIM = 3 * RWKV_DIM + RWKV_LORA_PAD + RWKV_GATE_LORA
RWKV_GN_EPS = 64e-5
RWKV_CHUNK = 64
DIL_PATTERNS = ((128, 1), (512, 4), (2048, 16))
DIL_GROUPS = 3
DIL_HEADS_PER_GROUP = 4
DIL_HEAD_DIM = 128
DIL_QKV_DIM = DIL_GROUPS * DIL_HEADS_PER_GROUP * DIL_HEAD_DIM
DIL_OUT_DIM = DIL_HEADS_PER_GROUP * DIL_HEAD_DIM
DIL_BLOCK = 128
ALIBI_MAX_BIAS = 8.0
MLA_HEADS = 6
MLA_NOPE_DIM = 128
MLA_ROPE_DIM = 64
MLA_V_DIM = 128
MLA_Q_LORA = 512
MLA_KV_LORA = 256
MLA_QK_DIM = MLA_NOPE_DIM + MLA_ROPE_DIM
MLA_OUT_DIM = MLA_HEADS * MLA_V_DIM
ROPE_THETA = 10000.0
N_EXPERTS = 32
N_EXPERT_GROUPS = 8
EXPERTS_PER_GROUP = 4
TOP_K = 2
D_EXPERT = 512
MOE_BLOCK = 128
ROW_UNROLL = 8
LN_EPS = 1e-5
RMS_EPS = 1e-6
NEG_INF = -1e30

ZO_DQ = 0
ZO_DK = DIL_QKV_DIM
ZO_DV = 2 * DIL_QKV_DIM
ZO_QA = 3 * DIL_QKV_DIM
ZO_DIM = ZO_QA + MLA_Q_LORA
ZT_KVA = 0
ZT_KR = MLA_KV_LORA
ZT_DIM = ZT_KR + 256

V7X_VMEM_LIMIT_BYTES = 48 * 1024 * 1024


def _params(*sem):
    return pltpu.CompilerParams(dimension_semantics=sem, vmem_limit_bytes=V7X_VMEM_LIMIT_BYTES)


def _sigmoid(x):
    return 1.0 / (1.0 + jnp.exp(-x))


def _mm_body(x_ref, w_ref, o_ref):
    o_ref[...] = jnp.dot(x_ref[...], w_ref[...], preferred_element_type=F32).astype(o_ref.dtype)


def _matmul(x, w, out_dtype, tm, tn):
    m, k = x.shape
    n = w.shape[1]
    return pl.pallas_call(
        _mm_body,
        grid=(m // tm, n // tn),
        in_specs=[pl.BlockSpec((tm, k), lambda i, j: (i, 0)),
                  pl.BlockSpec((k, tn), lambda i, j: (0, j))],
        out_specs=pl.BlockSpec((tm, tn), lambda i, j: (i, j)),
        out_shape=jax.ShapeDtypeStruct((m, n), out_dtype),
        compiler_params=_params("parallel", "arbitrary"),
        name="in_proj_tail",
    )(x, w)


def _in_proj_body(x_ref, w_ref, o_ref, wb_ref):
    @pl.when(pl.program_id(1) == 0)
    def _():
        wb_ref[...] = w_ref[0].astype(BF16)

    o_ref[...] = jnp.dot(x_ref[...], wb_ref[...], preferred_element_type=F32).astype(o_ref.dtype)


def _in_proj(x, w_all, layer, col0, ncols, out_dtype, tm=1024, tn=512):
    m, k = x.shape
    assert col0 % tn == 0 and ncols % tn == 0 and col0 + ncols <= w_all.shape[2]
    cb0 = col0 // tn
    return pl.pallas_call(
        _in_proj_body,
        grid=(ncols // tn, m // tm),
        in_specs=[pl.BlockSpec((tm, k), lambda j, i: (i, 0)),
                  pl.BlockSpec((1, k, tn), lambda j, i: (layer, 0, cb0 + j))],
        out_specs=pl.BlockSpec((tm, tn), lambda j, i: (i, j)),
        out_shape=jax.ShapeDtypeStruct((m, ncols), out_dtype),
        scratch_shapes=[pltpu.VMEM((k, tn), BF16)],
        compiler_params=_params("parallel", "arbitrary"),
        name="in_proj",
    )(x, w_all)


def _bf16_parts(x, n):
    parts = []
    for _ in range(n):
        part = x.astype(BF16)
        parts.append(part)
        x = x - part.astype(F32)
    return parts


def _bdot(a, b, dims):
    return lax.dot_general(a.astype(BF16), b.astype(BF16), (dims, ((0,), (0,))), preferred_element_type=F32)


def _bnt(a, b):
    return _bdot(a, b, ((2,), (2,)))


def _bnn(a, b):
    return _bdot(a, b, ((2,), (1,)))


def _btn(a, b):
    return _bdot(a, b, ((1,), (1,)))


RWKV_PAIRS = RWKV_HEADS // 2
PAIR_LANES = 2 * RWKV_HEAD_DIM


def _rwkv_body(z_ref, mu_ref, w0_ref, a0_ref, kk_ref, ka_ref, wup_ref, aup_ref, gup_ref, rk_ref, gng_ref, gnb_ref,
               y_ref, st_ref, carry_ref):
    c = pl.program_id(1)
    t, d, np_, pl_ = RWKV_CHUNK, RWKV_DIM, RWKV_PAIRS, PAIR_LANES

    @pl.when(c == 0)
    def _():
        st_ref[...] = jnp.zeros_like(st_ref)
        carry_ref[...] = jnp.zeros_like(carry_ref)

    z = z_ref[0]
    row = lax.broadcasted_iota(jnp.int32, z.shape, 0)
    zprev = jnp.where(row == 0, carry_ref[...], pltpu.roll(z, 1, axis=0))
    carry_ref[...] = z[t - 1:t, :]
    zs = z + (zprev - z) * mu_ref[...]
    r_w, k_w, v_w = zs[:, 0:d], zs[:, d:2 * d], zs[:, 2 * d:3 * d]
    lora = zs[:, 3 * d:3 * d + RWKV_LORA_PAD]
    gd = zs[:, 3 * d + RWKV_LORA_PAD:]
    th_hi, th_lo = _bf16_parts(jnp.tanh(lora), 2)
    u = (w0_ref[...] + jnp.dot(th_hi, wup_ref[0], preferred_element_type=F32)
         + jnp.dot(th_hi, wup_ref[1], preferred_element_type=F32)
         + jnp.dot(th_lo, wup_ref[0], preferred_element_type=F32))
    softplus = jnp.maximum(-u, 0.0) + jnp.log(1.0 + jnp.exp(-jnp.abs(u)))
    lw_w = -jnp.exp(-softplus - 0.5)
    a_w = _sigmoid(a0_ref[...] + jnp.dot(lora.astype(BF16), aup_ref[...], preferred_element_type=F32))
    g_w = jnp.dot(_sigmoid(gd).astype(BF16), gup_ref[...], preferred_element_type=F32)
    kmod_w = k_w * (1.0 + (a_w - 1.0) * ka_ref[...])
    kk_w = k_w * kk_ref[...]
    ti = lax.broadcasted_iota(jnp.int32, (t, t), 0)
    si = lax.broadcasted_iota(jnp.int32, (t, t), 1)
    incl = si <= ti
    strict = si < ti
    tri = incl.astype(BF16)
    lp_w = sum(jnp.dot(tri, part, preferred_element_type=F32) for part in _bf16_parts(lw_w, 3))

    pairs = lambda x: jnp.stack([x[:, p * pl_:(p + 1) * pl_] for p in range(np_)], axis=0)
    r, k, v, a, lw, lp, g = (pairs(x) for x in (r_w, kmod_w, v_w, a_w, lw_w, lp_w, g_w))
    li = lax.broadcasted_iota(jnp.int32, (pl_, pl_), 0)
    lj = lax.broadcasted_iota(jnp.int32, (pl_, pl_), 1)
    same_head = (li // RWKV_HEAD_DIM) == (lj // RWKV_HEAD_DIM)
    head_ones = same_head.astype(BF16)

    def head_sum(x):
        parts = _bf16_parts(x.reshape(np_ * t, pl_), 2)
        return sum(jnp.dot(part, head_ones, preferred_element_type=F32) for part in parts).reshape(np_, t, pl_)

    kk = pairs(kk_w)
    kk = kk / jnp.maximum(jnp.sqrt(head_sum(kk * kk)), 1e-12)
    lp_end = lp[:, t - 1:t, :]
    p_inv = jnp.exp(-lp)
    at = -kk * jnp.exp(lp - lw)
    bt = kk * a * p_inv
    kt = k * p_inv
    rt = r * jnp.exp(lp)
    to_end = jnp.exp(lp_end - lp)
    b_end = kk * a * to_end
    k_end = k * to_end

    lane = lax.broadcasted_iota(jnp.int32, (1, 1, pl_), 2)
    m0 = (lane < RWKV_HEAD_DIM).astype(F32)
    msk = jnp.concatenate([jnp.broadcast_to(m0, (np_, 1, pl_)), jnp.broadcast_to(1.0 - m0, (np_, 1, pl_))], axis=0)
    dup = lambda x: jnp.concatenate([x, x], axis=0)
    fold = lambda x: x[:np_] + x[np_:]
    lhs_a = dup(at) * msk
    lhs_r = dup(rt) * msk
    v2 = dup(v) * msk
    ar = jnp.concatenate([lhs_a, lhs_r], axis=1)
    x_b = _bnt(ar, dup(bt))
    x_k = _bnt(ar, dup(kt))
    a_ab = jnp.where(strict[None], x_b[:, :t], 0.0)
    a_rb = jnp.where(incl[None], x_b[:, t:], 0.0)
    a_ak = jnp.where(strict[None], x_k[:, :t], 0.0)
    a_rk = jnp.where(incl[None], x_k[:, t:], 0.0)

    sub = 16
    same_blk = ((ti // sub) == (si // sub))[None]
    eye = (ti == si).astype(F32)[None]
    ld = jnp.where(same_blk, a_ab, 0.0)
    lo = a_ab - ld
    dinv = eye + ld
    pw = ld
    for _ in range(3):
        pw = _bnn(pw, pw)
        dinv = dinv + _bnn(dinv, pw)
    n1 = _bnn(dinv, lo)
    n2 = _bnn(n1, n1)
    tinv = eye + n1 + n2 + _bnn(n1, n2)
    tinv = _bnn(tinv, dinv)

    akv = _bnn(a_ak, v2)
    wu = _bnn(tinv, jnp.concatenate([lhs_a, akv], axis=2))
    qy = _bnn(a_rb, wu)
    q = fold(lhs_r + qy[:, :, :pl_])
    y0 = fold(qy[:, :, pl_:] + _bnn(a_rk, v2))
    wt = fold(wu[:, :, :pl_])
    u0 = fold(wu[:, :, pl_:])

    s0 = st_ref[...]
    y = _bnn(q, s0) + y0
    diag_end = jnp.where((li == lj)[None], jnp.exp(lp_end), 0.0)
    m_t = diag_end + jnp.where(same_head[None], _btn(b_end, wt), 0.0)
    c_t = jnp.where(same_head[None], _btn(b_end, u0) + _btn(k_end, v), 0.0)
    st_ref[...] = _bnn(m_t, s0) + c_t

    inv_e = 1.0 / RWKV_HEAD_DIM
    yc = y - head_sum(y) * inv_e
    yv = head_sum(yc * yc) * inv_e
    yn = yc * lax.rsqrt(yv + RWKV_GN_EPS) * gng_ref[...] + gnb_ref[...]
    out = (yn + head_sum(r * k * rk_ref[...]) * v) * g
    for p in range(np_):
        y_ref[0, :, p * pl_:(p + 1) * pl_] = out[p].astype(y_ref.dtype)


def _rwkv_mix(z, mu, w0, a0, k_k, k_a, w_up_pad, a_up_pad, g_up, r_k, gn_g, gn_b):
    b, s, zin = z.shape
    d, t = RWKV_DIM, RWKV_CHUNK
    vec = lambda n: pl.BlockSpec((1, n), lambda bi, c: (0, 0))
    mat = lambda r: pl.BlockSpec((r, d), lambda bi, c: (0, 0))
    par = pl.BlockSpec((RWKV_PAIRS, 1, PAIR_LANES), lambda bi, c: (0, 0, 0))
    return pl.pallas_call(
        _rwkv_body,
        grid=(b, s // t),
        in_specs=[pl.BlockSpec((1, t, zin), lambda bi, c: (bi, c, 0)),
                  vec(zin), vec(d), vec(d), vec(d), vec(d),
                  pl.BlockSpec((2, RWKV_LORA_PAD, d), lambda bi, c: (0, 0, 0)),
                  mat(RWKV_LORA_PAD), mat(RWKV_GATE_LORA), par, par, par],
        out_specs=pl.BlockSpec((1, t, d), lambda bi, c: (bi, c, 0)),
        out_shape=jax.ShapeDtypeStruct((b, s, d), BF16),
        scratch_shapes=[pltpu.VMEM((RWKV_PAIRS, PAIR_LANES, PAIR_LANES), F32), pltpu.VMEM((1, zin), F32)],
        compiler_params=_params("parallel", "arbitrary"),
        name="rwkv_mix",
    )(z, mu, w0, a0, k_k, k_a, w_up_pad, a_up_pad, g_up, r_k, gn_g, gn_b)


def _dil_body(slopes_ref, q_ref, kc_ref, kp_ref, vc_ref, vp_ref, *rest, group, dil, n_sub, merge):
    head = pl.program_id(2)
    nblk = pl.program_id(3)
    bias = slopes_ref[group * DIL_HEADS_PER_GROUP + head] * float(dil)
    scale = DIL_HEAD_DIM ** -0.5
    nt = (((1,), (1,)), ((), ()))
    qi = lax.broadcasted_iota(jnp.int32, (DIL_BLOCK, DIL_BLOCK), 0)
    ki = lax.broadcasted_iota(jnp.int32, (DIL_BLOCK, DIL_BLOCK), 1)
    rel_c = qi - ki
    dist_c = bias * rel_c.astype(F32)
    dist_p = bias * (rel_c + DIL_BLOCK).astype(F32)
    if merge:
        o0_ref, l0_ref, o1_ref, l1_ref, y_ref = rest
    else:
        o_ref, lse_ref = rest
    for u in range(n_sub):
        rows = slice(u * DIL_BLOCK, (u + 1) * DIL_BLOCK)
        q = q_ref[rows, :]
        if u == 0:
            k_prev, v_prev = kp_ref[...], vp_ref[...]
            prev_lim = jnp.where(nblk == 0, -2 * DIL_BLOCK, 0)
        else:
            prows = slice((u - 1) * DIL_BLOCK, u * DIL_BLOCK)
            k_prev, v_prev = kc_ref[prows, :], vc_ref[prows, :]
            prev_lim = 0
        s_c = lax.dot_general(q, kc_ref[rows, :], nt, preferred_element_type=F32) * scale
        s_p = lax.dot_general(q, k_prev, nt, preferred_element_type=F32) * scale
        s_c = jnp.where(rel_c >= 0, s_c - dist_c, NEG_INF)
        s_p = jnp.where(rel_c <= prev_lim, s_p - dist_p, NEG_INF)
        m = jnp.maximum(jnp.max(s_c, axis=-1, keepdims=True), jnp.max(s_p, axis=-1, keepdims=True))
        e_c = jnp.exp(s_c - m)
        e_p = jnp.exp(s_p - m)
        den = jnp.sum(e_c, axis=-1, keepdims=True) + jnp.sum(e_p, axis=-1, keepdims=True)
        acc = (jnp.dot(e_c.astype(BF16), vc_ref[rows, :], preferred_element_type=F32)
               + jnp.dot(e_p.astype(BF16), v_prev, preferred_element_type=F32))
        o = acc / den
        lse = jnp.broadcast_to(m + jnp.log(den), (DIL_BLOCK, DIL_HEAD_DIM))
        if merge:
            l0, l1 = l0_ref[rows, :], l1_ref[rows, :]
            top = jnp.maximum(jnp.maximum(l0, l1), lse)
            w0, w1, w2 = jnp.exp(l0 - top), jnp.exp(l1 - top), jnp.exp(lse - top)
            y = (w0 * o0_ref[rows, :] + w1 * o1_ref[rows, :] + w2 * o) / (w0 + w1 + w2)
            y_ref[rows, :] = y.astype(y_ref.dtype)
        else:
            o_ref[rows, :] = o.astype(o_ref.dtype)
            lse_ref[rows, :] = lse


def _dilated_group(slopes, zo, group, batch, s, prior=None):
    n = batch * s
    dil = DIL_PATTERNS[group][1]
    nb = s // dil // DIL_BLOCK
    n_sub = min(nb, 8)
    steps = nb // n_sub
    ocols = DIL_OUT_DIM // DIL_HEAD_DIM
    rows = n_sub * DIL_BLOCK
    gcol = lambda col0: col0 + group * DIL_OUT_DIM
    if dil == 1:
        zv, qkv_cols = zo, (gcol(ZO_DQ), gcol(ZO_DK), gcol(ZO_DV))
    else:
        zv = jnp.concatenate([zo[:, gcol(c):gcol(c) + DIL_OUT_DIM] for c in (ZO_DQ, ZO_DK, ZO_DV)], axis=1)
        zv, qkv_cols = zv.reshape(n // dil, dil * 3 * DIL_OUT_DIM), (0, DIL_OUT_DIM, 2 * DIL_OUT_DIM)
    zcols = zv.shape[1] // dil // DIL_HEAD_DIM

    def cur(col0):
        c = col0 // DIL_HEAD_DIM
        return pl.BlockSpec((rows, DIL_HEAD_DIM), lambda b, r, h, i, sl: (b * steps + i, r * zcols + c + h))

    def prev(col0):
        c = col0 // DIL_HEAD_DIM
        return pl.BlockSpec((DIL_BLOCK, DIL_HEAD_DIM),
                            lambda b, r, h, i, sl: (jnp.maximum(b * nb + i * n_sub - 1, 0), r * zcols + c + h))

    ospec = pl.BlockSpec((rows, DIL_HEAD_DIM), lambda b, r, h, i, sl: (b * steps + i, r * ocols + h))
    oshape = lambda dt: jax.ShapeDtypeStruct((n // dil, dil * DIL_OUT_DIM), dt)
    merge = prior is not None
    extra = [t.reshape(n // dil, dil * DIL_OUT_DIM) for t in prior] if merge else []
    qc, kc, vc = qkv_cols
    out = pl.pallas_call(
        functools.partial(_dil_body, group=group, dil=dil, n_sub=n_sub, merge=merge),
        grid_spec=pltpu.PrefetchScalarGridSpec(
            num_scalar_prefetch=1, grid=(batch, dil, DIL_HEADS_PER_GROUP, steps),
            in_specs=[cur(qc), cur(kc), prev(kc), cur(vc), prev(vc)] + [ospec] * len(extra),
            out_specs=ospec if merge else [ospec, ospec]),
        out_shape=oshape(BF16) if merge else [oshape(BF16), oshape(F32)],
        compiler_params=_params("parallel", "parallel", "parallel", "arbitrary"),
        name="dilated_attn_g%d" % group,
    )(slopes, zv, zv, zv, zv, zv, *extra)
    if merge:
        return out.reshape(n, DIL_OUT_DIM)
    return [t.reshape(n, DIL_OUT_DIM) for t in out]


def _rms(x_bf16, g):
    x = x_bf16.astype(F32)
    return (x * lax.rsqrt(jnp.mean(x * x, axis=-1, keepdims=True) + RMS_EPS) * g).astype(BF16)


def _mla_q_body(x_ref, g_ref, w_ref, cos_ref, sin_ref, q_ref):
    acc = jnp.dot(_rms(x_ref[...], g_ref[...]), w_ref[...], preferred_element_type=F32)
    scale = MLA_QK_DIM ** -0.5
    cos, sin = cos_ref[...], sin_ref[...]
    nope_w = MLA_HEADS * MLA_NOPE_DIM
    for h in range(MLA_HEADS):
        q_ref[0, h, :, 0:MLA_NOPE_DIM] = (acc[:, h * 128:(h + 1) * 128] * scale).astype(BF16)
        base = nope_w + h * 256
        rope = acc[:, base:base + 128] * cos + acc[:, base + 128:base + 256] * sin
        q_ref[0, h, :, MLA_NOPE_DIM:MLA_QK_DIM] = (rope[:, 0:MLA_ROPE_DIM] * scale).astype(BF16)


def _mla_q(zo, g, w, cos, sin, batch, tm=512):
    n = zo.shape[0]
    s = n // batch
    nblk = s // tm
    return pl.pallas_call(
        _mla_q_body,
        grid=(batch, nblk),
        in_specs=[pl.BlockSpec((tm, MLA_Q_LORA), lambda b, i: (b * nblk + i, ZO_QA // MLA_Q_LORA)),
                  pl.BlockSpec((1, MLA_Q_LORA), lambda b, i: (0, 0)),
                  pl.BlockSpec(w.shape, lambda b, i: (0, 0)),
                  pl.BlockSpec((tm, 128), lambda b, i: (i, 0)),
                  pl.BlockSpec((tm, 128), lambda b, i: (i, 0))],
        out_specs=pl.BlockSpec((1, MLA_HEADS, tm, MLA_QK_DIM), lambda b, i: (b, 0, i, 0)),
        out_shape=jax.ShapeDtypeStruct((batch, MLA_HEADS, s, MLA_QK_DIM), BF16),
        compiler_params=_params("parallel", "parallel"),
        name="mla_q_proj",
    )(zo, g, w, cos, sin)


def _mla_kv_body(x_ref, kr_ref, g_ref, w_ref, cos_ref, sin_ref, k_ref, v_ref):
    acc = jnp.dot(_rms(x_ref[...], g_ref[...]), w_ref[...], preferred_element_type=F32)
    kr = kr_ref[...].astype(F32)
    rope = (kr[:, 0:128] * cos_ref[...] + kr[:, 128:256] * sin_ref[...])[:, 0:MLA_ROPE_DIM].astype(BF16)
    for h in range(MLA_HEADS):
        k_ref[0, h, :, 0:MLA_NOPE_DIM] = acc[:, h * 256:h * 256 + 128].astype(BF16)
        k_ref[0, h, :, MLA_NOPE_DIM:MLA_QK_DIM] = rope
        v_ref[0, h] = acc[:, h * 256 + 128:(h + 1) * 256].astype(BF16)


def _mla_kv(zo, g, w, cos, sin, batch, tm=512):
    n = zo.shape[0]
    s = n // batch
    nblk = s // tm
    return pl.pallas_call(
        _mla_kv_body,
        grid=(batch, nblk),
        in_specs=[pl.BlockSpec((tm, MLA_KV_LORA), lambda b, i: (b * nblk + i, ZT_KVA // MLA_KV_LORA)),
                  pl.BlockSpec((tm, 256), lambda b, i: (b * nblk + i, ZT_KR // 256)),
                  pl.BlockSpec((1, MLA_KV_LORA), lambda b, i: (0, 0)),
                  pl.BlockSpec(w.shape, lambda b, i: (0, 0)),
                  pl.BlockSpec((tm, 128), lambda b, i: (i, 0)),
                  pl.BlockSpec((tm, 128), lambda b, i: (i, 0))],
        out_specs=[pl.BlockSpec((1, MLA_HEADS, tm, MLA_QK_DIM), lambda b, i: (b, 0, i, 0)),
                   pl.BlockSpec((1, MLA_HEADS, tm, MLA_V_DIM), lambda b, i: (b, 0, i, 0))],
        out_shape=[jax.ShapeDtypeStruct((batch, MLA_HEADS, s, MLA_QK_DIM), BF16),
                   jax.ShapeDtypeStruct((batch, MLA_HEADS, s, MLA_V_DIM), BF16)],
        compiler_params=_params("parallel", "parallel"),
        name="mla_kv_proj",
    )(zo, zo, g, w, cos, sin)


def _flash_body(qi_ref, kj_ref, q_ref, k_ref, v_ref, o_ref, m_ref, l_ref, acc_ref, *, tq):
    p = pl.program_id(1)
    i = qi_ref[p]
    j = kj_ref[p]
    heads, dv = acc_ref.shape[0], acc_ref.shape[2]

    @pl.when(j == 0)
    def _():
        m_ref[...] = jnp.full_like(m_ref, NEG_INF)
        l_ref[...] = jnp.zeros_like(l_ref)
        acc_ref[...] = jnp.zeros_like(acc_ref)

    def update(masked):
        for h in range(heads):
            s = lax.dot_general(q_ref[0, h], k_ref[0, h], (((1,), (1,)), ((), ())), preferred_element_type=F32)
            if masked:
                qpos = lax.broadcasted_iota(jnp.int32, (tq, tq), 0)
                kpos = lax.broadcasted_iota(jnp.int32, (tq, tq), 1)
                s = jnp.where(kpos <= qpos, s, NEG_INF)
            m_old = m_ref[h]
            m_new = jnp.maximum(m_old, jnp.max(s, axis=-1, keepdims=True))
            alpha = jnp.exp(m_old - m_new)
            e = jnp.exp(s - jnp.concatenate([m_new] * (tq // dv), axis=1))
            l_ref[h] = alpha * l_ref[h] + jnp.sum(e, axis=-1, keepdims=True)
            acc_ref[h] = alpha * acc_ref[h] + jnp.dot(e.astype(BF16), v_ref[0, h], preferred_element_type=F32)
            m_ref[h] = m_new

    @pl.when(j < i)
    def _():
        update(False)

    @pl.when(j == i)
    def _():
        update(True)
        for h in range(heads):
            o_ref[0, :, h * dv:(h + 1) * dv] = (acc_ref[h] / l_ref[h]).astype(o_ref.dtype)


def _flash(q, k, v, tq=512):
    b, h, s, dq = q.shape
    dv = v.shape[-1]
    nq = s // tq
    pairs = [(i, j) for i in range(nq) for j in range(i + 1)]
    qi = jnp.asarray([pr[0] for pr in pairs], jnp.int32)
    kj = jnp.asarray([pr[1] for pr in pairs], jnp.int32)
    return pl.pallas_call(
        functools.partial(_flash_body, tq=tq),
        grid_spec=pltpu.PrefetchScalarGridSpec(
            num_scalar_prefetch=2, grid=(b, len(pairs)),
            in_specs=[pl.BlockSpec((1, h, tq, dq), lambda bi, p, qi, kj: (bi, 0, qi[p], 0)),
                      pl.BlockSpec((1, h, tq, dq), lambda bi, p, qi, kj: (bi, 0, kj[p], 0)),
                      pl.BlockSpec((1, h, tq, dv), lambda bi, p, qi, kj: (bi, 0, kj[p], 0))],
            out_specs=pl.BlockSpec((1, tq, h * dv), lambda bi, p, qi, kj: (bi, qi[p], 0)),
            scratch_shapes=[pltpu.VMEM((h, tq, dv), F32), pltpu.VMEM((h, tq, dv), F32), pltpu.VMEM((h, tq, dv), F32)]),
        out_shape=jax.ShapeDtypeStruct((b, s, h * dv), BF16),
        compiler_params=_params("parallel", "arbitrary"),
        name="mla_flash",
    )(qi, kj, q, k, v)


def _layernorm(h, g, b):
    mu = jnp.mean(h, axis=-1, keepdims=True)
    hc = h - mu
    var = jnp.mean(hc * hc, axis=-1, keepdims=True)
    return hc * lax.rsqrt(var + LN_EPS) * g + b


def _out_ln_body(ya_ref, yb_ref, yc_ref, w_ref, x_ref, g_ref, b_ref, o_ref, *, alpha):
    ka, kb = ya_ref.shape[1], yb_ref.shape[1]
    acc = (jnp.dot(ya_ref[...], w_ref[0:ka, :], preferred_element_type=F32)
           + jnp.dot(yb_ref[...], w_ref[ka:ka + kb, :], preferred_element_type=F32)
           + jnp.dot(yc_ref[...], w_ref[ka + kb:, :], preferred_element_type=F32))
    o_ref[...] = _layernorm(alpha * x_ref[...] + acc, g_ref[...], b_ref[...])


def _out_ln(ya, yb, yc, w, x, g, b, alpha, tm=256):
    n, d = x.shape
    row = lambda c: pl.BlockSpec((tm, c), lambda i: (i, 0))
    vec = pl.BlockSpec((1, d), lambda i: (0, 0))
    return pl.pallas_call(
        functools.partial(_out_ln_body, alpha=alpha),
        grid=(n // tm,),
        in_specs=[row(ya.shape[1]), row(yb.shape[1]), row(yc.shape[1]),
                  pl.BlockSpec(w.shape, lambda i: (0, 0)), row(d), vec, vec],
        out_specs=row(d),
        out_shape=jax.ShapeDtypeStruct((n, d), F32),
        compiler_params=_params("parallel"),
        name="out_proj_ln",
    )(ya, yb, yc, w, x, g, b)


def _router_body(x_ref, w_ref, b_ref, idx_ref, gate_ref):
    logits = lax.dot_general(w_ref[...], x_ref[...], (((1,), (1,)), ((), ())),
                             precision=HIGHEST, preferred_element_type=F32)
    scores = _sigmoid(logits)
    sel = scores + b_ref[...]
    ng = N_EXPERT_GROUPS
    sel_j = [sel[j * ng:(j + 1) * ng] for j in range(EXPERTS_PER_GROUP)]
    sc_j = [scores[j * ng:(j + 1) * ng] for j in range(EXPERTS_PER_GROUP)]
    grp = None
    for p in range(EXPERTS_PER_GROUP):
        for q in range(p + 1, EXPERTS_PER_GROUP):
            pair = sel_j[p] + sel_j[q]
            grp = pair if grp is None else jnp.maximum(grp, pair)
    gi = lax.broadcasted_iota(jnp.int32, grp.shape, 0)
    gmax = jnp.max(grp, axis=0, keepdims=True)
    gbest = jnp.min(jnp.where(grp == gmax, gi, ng), axis=0, keepdims=True)
    pick = gi == gbest
    mem = [jnp.sum(jnp.where(pick, t, 0.0), axis=0, keepdims=True) for t in sel_j]
    msc = [jnp.sum(jnp.where(pick, t, 0.0), axis=0, keepdims=True) for t in sc_j]

    def first_argmax(vals, exclude):
        best = jnp.full_like(vals[0], -jnp.inf)
        bi = jnp.zeros(vals[0].shape, jnp.int32)
        bs = jnp.zeros_like(vals[0])
        for j in range(EXPERTS_PER_GROUP):
            ok = vals[j] > best
            if exclude is not None:
                ok = ok & (exclude != j)
            best = jnp.where(ok, vals[j], best)
            bi = jnp.where(ok, j, bi)
            bs = jnp.where(ok, msc[j], bs)
        return bi, bs

    i1, s1 = first_argmax(mem, None)
    i2, s2 = first_argmax(mem, i1)
    tot = s1 + s2
    base = gbest * EXPERTS_PER_GROUP
    idx_ref[0:1, :] = base + i1
    idx_ref[1:2, :] = base + i2
    gate_ref[0:1, :] = s1 / tot
    gate_ref[1:2, :] = s2 / tot


def _router(x, w_t, bias, tm=512):
    n, d = x.shape
    return pl.pallas_call(
        _router_body,
        grid=(n // tm,),
        in_specs=[pl.BlockSpec((tm, d), lambda i: (i, 0)),
                  pl.BlockSpec((N_EXPERTS, d), lambda i: (0, 0)),
                  pl.BlockSpec((N_EXPERTS, 1), lambda i: (0, 0))],
        out_specs=[pl.BlockSpec((TOP_K, tm), lambda i: (0, i)), pl.BlockSpec((TOP_K, tm), lambda i: (0, i))],
        out_shape=[jax.ShapeDtypeStruct((TOP_K, n), jnp.int32), jax.ShapeDtypeStruct((TOP_K, n), F32)],
        compiler_params=_params("parallel"),
        name="moe_router",
    )(x, w_t, bias)


def _expert_body(blk_e_ref, nvalid_ref, src_ref, x_hbm, wg_ref, wu_ref, wd_ref, out_hbm,
                 xg_ref, yo_ref, wgb_ref, wub_ref, wdb_ref, sem_in, sem_out, *, n_tokens, n_blocks):
    i = pl.program_id(0)
    slot = i & 1

    def for_rows(count, fn):
        def group(gi, c):
            for u in range(ROW_UNROLL):
                fn(gi * ROW_UNROLL + u, u)
            return c
        full = count // ROW_UNROLL
        lax.fori_loop(0, full, group, 0)
        for u in range(ROW_UNROLL - 1):
            @pl.when(full * ROW_UNROLL + u < count)
            def _():
                fn(full * ROW_UNROLL + u, u)

    def gather(blk, sl, start):
        def row(r, u):
            if start:
                a = src_ref[blk * MOE_BLOCK + r]
                tok = jnp.where(a >= n_tokens, a - n_tokens, a)
                pltpu.make_async_copy(x_hbm.at[pl.ds(tok, 1)], xg_ref.at[sl, pl.ds(r, 1)],
                                      sem_in.at[sl]).start(priority=u % 2)
            else:
                pltpu.make_async_copy(x_hbm.at[pl.ds(0, 1)], xg_ref.at[sl, pl.ds(0, 1)], sem_in.at[sl]).wait()
        for_rows(nvalid_ref[blk], row)

    def scatter(blk, sl, start):
        def row(r, u):
            if start:
                a = src_ref[blk * MOE_BLOCK + r]
                pltpu.make_async_copy(yo_ref.at[sl, pl.ds(r, 1)], out_hbm.at[pl.ds(a, 1)],
                                      sem_out.at[sl]).start(priority=u % 2)
            else:
                pltpu.make_async_copy(yo_ref.at[sl, pl.ds(0, 1)], out_hbm.at[pl.ds(0, 1)], sem_out.at[sl]).wait()
        for_rows(nvalid_ref[blk], row)

    @pl.when(i == 0)
    def _():
        xg_ref[...] = jnp.zeros_like(xg_ref)
        gather(0, 0, True)

    @pl.when(i + 1 < n_blocks)
    def _():
        gather(i + 1, 1 - slot, True)

    gather(i, slot, False)

    @pl.when(i >= 2)
    def _():
        scatter(i - 2, slot, False)

    @pl.when((i == 0) | (blk_e_ref[i] != blk_e_ref[jnp.maximum(i - 1, 0)]))
    def _():
        wgb_ref[...] = wg_ref[0, 0].astype(BF16)
        wub_ref[...] = wu_ref[0, 0].astype(BF16)
        wdb_ref[...] = wd_ref[0, 0].astype(BF16)

    @pl.when(nvalid_ref[i] > 0)
    def _():
        xb = xg_ref[slot].astype(BF16)
        gt = jnp.dot(xb, wgb_ref[...], preferred_element_type=F32)
        up = jnp.dot(xb, wub_ref[...], preferred_element_type=F32)
        hid = (gt * _sigmoid(gt) * up).astype(BF16)
        yo_ref[slot] = jnp.dot(hid, wdb_ref[...], preferred_element_type=F32)

    scatter(i, slot, True)

    @pl.when(i == n_blocks - 1)
    def _():
        scatter(i, slot, False)

        @pl.when(i >= 1)
        def _():
            scatter(i - 1, 1 - slot, False)


def _experts(blk_e, nvalid, src, x, wg, wu, wd, layer):
    n, d = x.shape
    nb = blk_e.shape[0]
    wspec = lambda shp: pl.BlockSpec((1, 1) + shp, lambda i, be, nv, sr: (layer, be[i], 0, 0))
    return pl.pallas_call(
        functools.partial(_expert_body, n_tokens=n, n_blocks=nb),
        grid_spec=pltpu.PrefetchScalarGridSpec(
            num_scalar_prefetch=3, grid=(nb,),
            in_specs=[pl.BlockSpec(memory_space=pl.ANY),
                      wspec((d, D_EXPERT)), wspec((d, D_EXPERT)), wspec((D_EXPERT, d))],
            out_specs=pl.BlockSpec(memory_space=pl.ANY),
            scratch_shapes=[pltpu.VMEM((2, MOE_BLOCK, d), F32), pltpu.VMEM((2, MOE_BLOCK, d), F32),
                            pltpu.VMEM((d, D_EXPERT), BF16), pltpu.VMEM((d, D_EXPERT), BF16),
                            pltpu.VMEM((D_EXPERT, d), BF16),
                            pltpu.SemaphoreType.DMA((2,)), pltpu.SemaphoreType.DMA((2,))]),
        out_shape=jax.ShapeDtypeStruct((TOP_K * n, d), F32),
        compiler_params=_params("arbitrary"),
        name="moe_experts",
    )(blk_e, nvalid, src, x, wg, wu, wd)


def _dispatch(idx):
    n = idx.shape[1]
    a_tot = TOP_K * n
    e_flat = idx.reshape(a_tot)
    onehot = (e_flat[:, None] == jnp.arange(N_EXPERTS, dtype=jnp.int32)[None, :]).astype(F32)
    chunk = 128
    oh = onehot.reshape(a_tot // chunk, chunk, N_EXPERTS)
    tri = (jnp.arange(chunk)[None, :] <= jnp.arange(chunk)[:, None]).astype(F32)
    within = jnp.einsum('ts,csn->ctn', tri, oh)
    tot = within[:, -1, :]
    csum = (within + (jnp.cumsum(tot, axis=0) - tot)[:, None, :]).reshape(a_tot, N_EXPERTS)
    rank = jnp.sum(onehot * csum, axis=1).astype(jnp.int32) - 1
    counts = csum[-1].astype(jnp.int32)
    padded = (counts + MOE_BLOCK - 1) // MOE_BLOCK * MOE_BLOCK
    pad_end = jnp.cumsum(padded)
    pad_start = pad_end - padded
    dest = pad_start[e_flat] + rank
    nb = (a_tot + N_EXPERTS * (MOE_BLOCK - 1) + MOE_BLOCK - 1) // MOE_BLOCK
    p_rows = nb * MOE_BLOCK
    src = jnp.zeros((p_rows,), jnp.int32).at[dest].set(jnp.arange(a_tot, dtype=jnp.int32))
    blk_start = jnp.arange(nb, dtype=jnp.int32) * MOE_BLOCK
    blk_e = jnp.sum((pad_end[None, :] <= blk_start[:, None]).astype(jnp.int32), axis=1)
    blk_e = jnp.minimum(blk_e, N_EXPERTS - 1)
    nvalid = jnp.clip(pad_start[blk_e] + counts[blk_e] - blk_start, 0, MOE_BLOCK).astype(jnp.int32)
    return blk_e, nvalid, src


def _final_body(x_ref, ya_ref, yb_ref, ga_ref, gb_ref, g_ref, b_ref, p_ref, wp_ref, wg_ref, o_ref, ob_ref, *, alpha):
    ffn = ya_ref[...] * ga_ref[...] + yb_ref[...] * gb_ref[...]
    x2 = _layernorm(alpha * x_ref[...] + ffn, g_ref[...], b_ref[...])
    gate = jnp.dot(x2.astype(BF16), wg_ref[...], preferred_element_type=F32)
    proj = jnp.dot(p_ref[...].astype(BF16), wp_ref[...], preferred_element_type=F32)
    out = x2 + _sigmoid(gate) * proj
    o_ref[...] = out
    ob_ref[...] = out.astype(BF16)


def _final(x, y2, gates, g, b, p, wp, wg, alpha, tm=256):
    n, d = x.shape
    nblk = n // tm
    row = pl.BlockSpec((tm, d), lambda i: (i, 0))
    vec = pl.BlockSpec((1, d), lambda i: (0, 0))
    return pl.pallas_call(
        functools.partial(_final_body, alpha=alpha),
        grid=(nblk,),
        in_specs=[row, row, pl.BlockSpec((tm, d), lambda i: (i + nblk, 0)),
                  pl.BlockSpec((tm, 1), lambda i: (i, 0)), pl.BlockSpec((tm, 1), lambda i: (i + nblk, 0)), vec, vec,
                  pl.BlockSpec((tm, PLE_DIM), lambda i: (i, 0)),
                  pl.BlockSpec((PLE_DIM, d), lambda i: (0, 0)),
                  pl.BlockSpec((d, d), lambda i: (0, 0))],
        out_specs=[row, row],
        out_shape=[jax.ShapeDtypeStruct((n, d), F32), jax.ShapeDtypeStruct((n, d), BF16)],
        compiler_params=_params("parallel"),
        name="moe_combine_ln_ple",
    )(x, y2, y2, gates, gates, g, b, p, wp, wg)


def _rot_cols(w):
    half = MLA_ROPE_DIM // 2
    return jnp.concatenate([-w[..., half:], w[..., :half]], axis=-1)


def _pad_lanes(w, width=128):
    return jnp.pad(w, [(0, 0)] * (w.ndim - 1) + [(0, width - w.shape[-1])])


def _rope_tables(s):
    half = MLA_ROPE_DIM // 2
    inv = ROPE_THETA ** (-jnp.arange(half, dtype=F32) / half)
    ang = jnp.arange(s, dtype=F32)[:, None] * inv[None, :]
    cos = jnp.concatenate([jnp.cos(ang), jnp.cos(ang)], axis=-1)
    sin = jnp.concatenate([jnp.sin(ang), jnp.sin(ang)], axis=-1)
    return _pad_lanes(cos), _pad_lanes(sin)


def kernel(x, p, w_in, rwkv_mu, rwkv_w0, rwkv_w_up, rwkv_a0, rwkv_a_up, rwkv_g_up, rwkv_k_k, rwkv_k_a, rwkv_r_k, rwkv_gn_g, rwkv_gn_b, mla_qa_g, mla_w_uq, mla_kva_g, mla_w_ukv, w_out, ln1_g, ln1_b, router_w, router_b, moe_w_gate, moe_w_up, moe_w_down, ln2_g, ln2_b, ple_w_proj, ple_w_gate):
    batch, s, d = x.shape
    depth = w_in.shape[0]
    n = batch * s
    alpha = (2 * depth) ** 0.25
    assert s % (DIL_PATTERNS[-1][1] * DIL_BLOCK) == 0 and s % 512 == 0

    cos, sin = _rope_tables(s)
    slopes = jnp.exp2(-ALIBI_MAX_BIAS * jnp.arange(1, 13, dtype=F32) / 12)
    perm = np.array([4 * g + j for j in range(EXPERTS_PER_GROUP) for g in range(N_EXPERT_GROUPS)])
    router_wt = router_w.T[perm]
    router_bt = router_b[perm].reshape(N_EXPERTS, 1)

    xf = x.reshape(n, d)
    xb = xf.astype(BF16)
    kva0 = RWKV_IN_DIM + ZO_DIM
    for li in range(depth):
        w_kr = w_in[li, :, kva0 + MLA_KV_LORA:]
        w_tail = jnp.concatenate([w_in[li, :, kva0:kva0 + MLA_KV_LORA], _pad_lanes(w_kr), _pad_lanes(_rot_cols(w_kr))],
                                 axis=1).astype(BF16)
        z_r = _in_proj(xb, w_in, li, 0, RWKV_IN_DIM, F32)
        z_o = _in_proj(xb, w_in, li, RWKV_IN_DIM, ZO_DIM, BF16)
        z_t = _matmul(xb, w_tail, BF16, 1024, ZT_DIM)

        w_up_pad = jnp.pad(rwkv_w_up[li], ((0, 64), (0, 0)))
        w_up_hi = w_up_pad.astype(BF16)
        w_up_pad = jnp.stack([w_up_hi, (w_up_pad - w_up_hi.astype(F32)).astype(BF16)])
        a_up_pad = jnp.pad(rwkv_a_up[li], ((64, 0), (0, 0))).astype(BF16)
        row = lambda t: t.reshape(1, -1)
        hpar = lambda t: t.reshape(RWKV_PAIRS, 1, PAIR_LANES)
        y_a = _rwkv_mix(z_r.reshape(batch, s, RWKV_IN_DIM), row(rwkv_mu[li]), row(rwkv_w0[li]), row(rwkv_a0[li]),
                        row(rwkv_k_k[li]), row(rwkv_k_a[li]), w_up_pad, a_up_pad, rwkv_g_up[li].astype(BF16),
                        hpar(rwkv_r_k[li]), hpar(rwkv_gn_g[li]), hpar(rwkv_gn_b[li])).reshape(n, RWKV_DIM)

        prior = _dilated_group(slopes, z_o, 0, batch, s) + _dilated_group(slopes, z_o, 1, batch, s)
        y_b = _dilated_group(slopes, z_o, 2, batch, s, prior=prior)

        wq = mla_w_uq[li].reshape(MLA_Q_LORA, MLA_HEADS, MLA_QK_DIM)
        wq_rope = wq[:, :, MLA_NOPE_DIM:]
        wq_all = jnp.concatenate(
            [wq[:, :, :MLA_NOPE_DIM].reshape(MLA_Q_LORA, -1),
             jnp.concatenate([_pad_lanes(wq_rope), _pad_lanes(_rot_cols(wq_rope))], axis=-1).reshape(MLA_Q_LORA, -1)],
            axis=1).astype(BF16)
        q_c = _mla_q(z_o, row(mla_qa_g[li]), wq_all, cos, sin, batch)
        k_c, v_c = _mla_kv(z_t, row(mla_kva_g[li]), mla_w_ukv[li].astype(BF16), cos, sin, batch)
        y_c = _flash(q_c, k_c, v_c).reshape(n, MLA_OUT_DIM)

        x1 = _out_ln(y_a, y_b, y_c, w_out[li].astype(BF16), xf, row(ln1_g[li]), row(ln1_b[li]), alpha)

        idx, gate = _router(x1, router_wt, router_bt)
        blk_e, nvalid, src = _dispatch(idx)
        y2 = _experts(blk_e, nvalid, src, x1, moe_w_gate, moe_w_up, moe_w_down, li)

        xf, xb = _final(x1, y2, gate.reshape(TOP_K * n, 1), row(ln2_g[li]), row(ln2_b[li]), p[li].reshape(n, PLE_DIM),
                        ple_w_proj[li].astype(BF16), ple_w_gate[li].astype(BF16), alpha)
    return xf.reshape(batch, s, d)
```

```python
import functools
import math

import numpy as np
import jax
import jax.numpy as jnp
from jax import lax
from jax.experimental import pallas as pl
from jax.experimental.pallas import tpu as pltpu

F32 = jnp.float32
BF16 = jnp.bfloat16
HIGHEST = lax.Precision.HIGHEST

PLE_DIM = 256
RWKV_HEADS = 12
RWKV_HEAD_DIM = 64
RWKV_DIM = RWKV_HEADS * RWKV_HEAD_DIM
RWKV_LORA_PAD = 128
RWKV_GATE_LORA = 128
RWKV_IN_DIM = 3 * RWKV_DIM + RWKV_LORA_PAD + RWKV_GATE_LORA
RWKV_GN_EPS = 64e-5
RWKV_CHUNK = 64
DIL_PATTERNS = ((128, 1), (512, 4), (2048, 16))
DIL_GROUPS = 3
DIL_HEADS_PER_GROUP = 4
DIL_HEAD_DIM = 128
DIL_QKV_DIM = DIL_GROUPS * DIL_HEADS_PER_GROUP * DIL_HEAD_DIM
DIL_OUT_DIM = DIL_HEADS_PER_GROUP * DIL_HEAD_DIM
DIL_BLOCK = 128
ALIBI_MAX_BIAS = 8.0
MLA_HEADS = 6
MLA_NOPE_DIM = 128
MLA_ROPE_DIM = 64
MLA_V_DIM = 128
MLA_Q_LORA = 512
MLA_KV_LORA = 256
MLA_QK_DIM = MLA_NOPE_DIM + MLA_ROPE_DIM
MLA_OUT_DIM = MLA_HEADS * MLA_V_DIM
ROPE_THETA = 10000.0
N_EXPERTS = 32
N_EXPERT_GROUPS = 8
EXPERTS_PER_GROUP = 4
TOP_K = 2
D_EXPERT = 512
MOE_BLOCK = 128
LN_EPS = 1e-5
RMS_EPS = 1e-6
NEG_INF = -1e30

ZO_DQ = 0
ZO_DK = DIL_QKV_DIM
ZO_DV = 2 * DIL_QKV_DIM
ZO_QA = 3 * DIL_QKV_DIM
ZO_DIM = ZO_QA + MLA_Q_LORA
ZT_KVA = 0
ZT_KR = MLA_KV_LORA
ZT_DIM = ZT_KR + 256

V7X_VMEM_LIMIT_BYTES = 48 * 1024 * 1024


def _params(*sem):
    return pltpu.CompilerParams(dimension_semantics=sem, vmem_limit_bytes=V7X_VMEM_LIMIT_BYTES)


def _sigmoid(x):
    return 1.0 / (1.0 + jnp.exp(-x))


def _mm_body(x_ref, w_ref, o_ref):
    o_ref[...] = jnp.dot(x_ref[...], w_ref[...], preferred_element_type=F32).astype(o_ref.dtype)


def _matmul(x, w, out_dtype, tm, tn):
    m, k = x.shape
    n = w.shape[1]
    return pl.pallas_call(
        _mm_body,
        grid=(m // tm, n // tn),
        in_specs=[pl.BlockSpec((tm, k), lambda i, j: (i, 0)),
                  pl.BlockSpec((k, tn), lambda i, j: (0, j))],
        out_specs=pl.BlockSpec((tm, tn), lambda i, j: (i, j)),
        out_shape=jax.ShapeDtypeStruct((m, n), out_dtype),
        compiler_params=_params("parallel", "arbitrary"),
        name="in_proj_tail",
    )(x, w)


def _in_proj_body(x_ref, w_ref, o_ref, wb_ref):
    @pl.when(pl.program_id(1) == 0)
    def _():
        wb_ref[...] = w_ref[0].astype(BF16)

    o_ref[...] = jnp.dot(x_ref[...], wb_ref[...], preferred_element_type=F32).astype(o_ref.dtype)


def _in_proj(x, w_all, layer, col0, ncols, out_dtype, tm=1024, tn=512):
    m, k = x.shape
    assert col0 % tn == 0 and ncols % tn == 0 and col0 + ncols <= w_all.shape[2]
    cb0 = col0 // tn
    return pl.pallas_call(
        _in_proj_body,
        grid=(ncols // tn, m // tm),
        in_specs=[pl.BlockSpec((tm, k), lambda j, i: (i, 0)),
                  pl.BlockSpec((1, k, tn), lambda j, i: (layer, 0, cb0 + j))],
        out_specs=pl.BlockSpec((tm, tn), lambda j, i: (i, j)),
        out_shape=jax.ShapeDtypeStruct((m, ncols), out_dtype),
        scratch_shapes=[pltpu.VMEM((k, tn), BF16)],
        compiler_params=_params("parallel", "arbitrary"),
        name="in_proj",
    )(x, w_all)


def _bf16_parts(x, n):
    parts = []
    for _ in range(n):
        part = x.astype(BF16)
        parts.append(part)
        x = x - part.astype(F32)
    return parts


def _bdot(a, b, dims):
    return lax.dot_general(a.astype(BF16), b.astype(BF16), (dims, ((0,), (0,))), preferred_element_type=F32)


def _bnt(a, b):
    return _bdot(a, b, ((2,), (2,)))


def _bnn(a, b):
    return _bdot(a, b, ((2,), (1,)))


def _btn(a, b):
    return _bdot(a, b, ((1,), (1,)))


RWKV_PAIRS = RWKV_HEADS // 2
PAIR_LANES = 2 * RWKV_HEAD_DIM


def _rwkv_body(z_ref, mu_ref, w0_ref, a0_ref, kk_ref, ka_ref, wup_ref, aup_ref, gup_ref, rk_ref, gng_ref, gnb_ref,
               y_ref, st_ref, carry_ref):
    c = pl.program_id(1)
    t, d, np_, pl_ = RWKV_CHUNK, RWKV_DIM, RWKV_PAIRS, PAIR_LANES

    @pl.when(c == 0)
    def _():
        st_ref[...] = jnp.zeros_like(st_ref)
        carry_ref[...] = jnp.zeros_like(carry_ref)

    z = z_ref[0]
    row = lax.broadcasted_iota(jnp.int32, z.shape, 0)
    zprev = jnp.where(row == 0, carry_ref[...], pltpu.roll(z, 1, axis=0))
    carry_ref[...] = z[t - 1:t, :]
    zs = z + (zprev - z) * mu_ref[...]
    r_w, k_w, v_w = zs[:, 0:d], zs[:, d:2 * d], zs[:, 2 * d:3 * d]
    lora = zs[:, 3 * d:3 * d + RWKV_LORA_PAD]
    gd = zs[:, 3 * d + RWKV_LORA_PAD:]
    th_hi, th_lo = _bf16_parts(jnp.tanh(lora), 2)
    u = (w0_ref[...] + jnp.dot(th_hi, wup_ref[0], preferred_element_type=F32)
         + jnp.dot(th_hi, wup_ref[1], preferred_element_type=F32)
         + jnp.dot(th_lo, wup_ref[0], preferred_element_type=F32))
    softplus = jnp.maximum(-u, 0.0) + jnp.log(1.0 + jnp.exp(-jnp.abs(u)))
    lw_w = -jnp.exp(-softplus - 0.5)
    a_w = _sigmoid(a0_ref[...] + jnp.dot(lora.astype(BF16), aup_ref[...], preferred_element_type=F32))
    g_w = jnp.dot(_sigmoid(gd).astype(BF16), gup_ref[...], preferred_element_type=F32)
    kmod_w = k_w * (1.0 + (a_w - 1.0) * ka_ref[...])
    kk_w = k_w * kk_ref[...]
    ti = lax.broadcasted_iota(jnp.int32, (t, t), 0)
    si = lax.broadcasted_iota(jnp.int32, (t, t), 1)
    incl = si <= ti
    strict = si < ti
    tri = incl.astype(BF16)
    lp_w = sum(jnp.dot(tri, part, preferred_element_type=F32) for part in _bf16_parts(lw_w, 3))

    pairs = lambda x: jnp.stack([x[:, p * pl_:(p + 1) * pl_] for p in range(np_)], axis=0)
    r, k, v, a, lw, lp, g = (pairs(x) for x in (r_w, kmod_w, v_w, a_w, lw_w, lp_w, g_w))
    li = lax.broadcasted_iota(jnp.int32, (pl_, pl_), 0)
    lj = lax.broadcasted_iota(jnp.int32, (pl_, pl_), 1)
    same_head = (li // RWKV_HEAD_DIM) == (lj // RWKV_HEAD_DIM)
    head_ones = same_head.astype(BF16)

    def head_sum(x):
        parts = _bf16_parts(x.reshape(np_ * t, pl_), 2)
        return sum(jnp.dot(part, head_ones, preferred_element_type=F32) for part in parts).reshape(np_, t, pl_)

    kk = pairs(kk_w)
    kk = kk / jnp.maximum(jnp.sqrt(head_sum(kk * kk)), 1e-12)
    lp_end = lp[:, t - 1:t, :]
    p_inv = jnp.exp(-lp)
    at = -kk * jnp.exp(lp - lw)
    bt = kk * a * p_inv
    kt = k * p_inv
    rt = r * jnp.exp(lp)
    to_end = jnp.exp(lp_end - lp)
    b_end = kk * a * to_end
    k_end = k * to_end

    lane = lax.broadcasted_iota(jnp.int32, (1, 1, pl_), 2)
    m0 = (lane < RWKV_HEAD_DIM).astype(F32)
    msk = jnp.concatenate([jnp.broadcast_to(m0, (np_, 1, pl_)), jnp.broadcast_to(1.0 - m0, (np_, 1, pl_))], axis=0)
    dup = lambda x: jnp.concatenate([x, x], axis=0)
    fold = lambda x: x[:np_] + x[np_:]
    lhs_a = dup(at) * msk
    lhs_r = dup(rt) * msk
    v2 = dup(v) * msk
    ar = jnp.concatenate([lhs_a, lhs_r], axis=1)
    x_b = _bnt(ar, dup(bt))
    x_k = _bnt(ar, dup(kt))
    a_ab = jnp.where(strict[None], x_b[:, :t], 0.0)
    a_rb = jnp.where(incl[None], x_b[:, t:], 0.0)
    a_ak = jnp.where(strict[None], x_k[:, :t], 0.0)
    a_rk = jnp.where(incl[None], x_k[:, t:], 0.0)

    sub = 16
    same_blk = ((ti // sub) == (si // sub))[None]
    eye = (ti == si).astype(F32)[None]
    ld = jnp.where(same_blk, a_ab, 0.0)
    lo = a_ab - ld
    dinv = eye + ld
    pw = ld
    for _ in range(3):
        pw = _bnn(pw, pw)
        dinv = dinv + _bnn(dinv, pw)
    n1 = _bnn(dinv, lo)
    n2 = _bnn(n1, n1)
    tinv = eye + n1 + n2 + _bnn(n1, n2)
    tinv = _bnn(tinv, dinv)

    akv = _bnn(a_ak, v2)
    wu = _bnn(tinv, jnp.concatenate([lhs_a, akv], axis=2))
    qy = _bnn(a_rb, wu)
    q = fold(lhs_r + qy[:, :, :pl_])
    y0 = fold(qy[:, :, pl_:] + _bnn(a_rk, v2))
    wt = fold(wu[:, :, :pl_])
    u0 = fold(wu[:, :, pl_:])

    s0 = st_ref[...]
    y = _bnn(q, s0) + y0
    diag_end = jnp.where((li == lj)[None], jnp.exp(lp_end), 0.0)
    m_t = diag_end + jnp.where(same_head[None], _btn(b_end, wt), 0.0)
    c_t = jnp.where(same_head[None], _btn(b_end, u0) + _btn(k_end, v), 0.0)
    st_ref[...] = _bnn(m_t, s0) + c_t

    inv_e = 1.0 / RWKV_HEAD_DIM
    yc = y - head_sum(y) * inv_e
    yv = head_sum(yc * yc) * inv_e
    yn = yc * lax.rsqrt(yv + RWKV_GN_EPS) * gng_ref[...] + gnb_ref[...]
    out = (yn + head_sum(r * k * rk_ref[...]) * v) * g
    for p in range(np_):
        y_ref[0, :, p * pl_:(p + 1) * pl_] = out[p].astype(y_ref.dtype)


def _rwkv_mix(z, mu, w0, a0, k_k, k_a, w_up_pad, a_up_pad, g_up, r_k, gn_g, gn_b):
    b, s, zin = z.shape
    d, t = RWKV_DIM, RWKV_CHUNK
    vec = lambda n: pl.BlockSpec((1, n), lambda bi, c: (0, 0))
    mat = lambda r: pl.BlockSpec((r, d), lambda bi, c: (0, 0))
    par = pl.BlockSpec((RWKV_PAIRS, 1, PAIR_LANES), lambda bi, c: (0, 0, 0))
    return pl.pallas_call(
        _rwkv_body,
        grid=(b, s // t),
        in_specs=[pl.BlockSpec((1, t, zin), lambda bi, c: (bi, c, 0)),
                  vec(zin), vec(d), vec(d), vec(d), vec(d),
                  pl.BlockSpec((2, RWKV_LORA_PAD, d), lambda bi, c: (0, 0, 0)),
                  mat(RWKV_LORA_PAD), mat(RWKV_GATE_LORA), par, par, par],
        out_specs=pl.BlockSpec((1, t, d), lambda bi, c: (bi, c, 0)),
        out_shape=jax.ShapeDtypeStruct((b, s, d), BF16),
        scratch_shapes=[pltpu.VMEM((RWKV_PAIRS, PAIR_LANES, PAIR_LANES), F32), pltpu.VMEM((1, zin), F32)],
        compiler_params=_params("parallel", "arbitrary"),
        name="rwkv_mix",
    )(z, mu, w0, a0, k_k, k_a, w_up_pad, a_up_pad, g_up, r_k, gn_g, gn_b)


def _dil_body(slopes_ref, q_ref, kc_ref, kp_ref, vc_ref, vp_ref, *rest, group, dil, n_sub, n_heads, merge):
    nblk = pl.program_id(3)
    scale = DIL_HEAD_DIM ** -0.5
    nt = (((1,), (1,)), ((), ()))
    qi = lax.broadcasted_iota(jnp.int32, (DIL_BLOCK, DIL_BLOCK), 0)
    ki = lax.broadcasted_iota(jnp.int32, (DIL_BLOCK, DIL_BLOCK), 1)
    rel_c = qi - ki
    if merge:
        o0_ref, l0_ref, o1_ref, l1_ref, y_ref = rest
    else:
        o_ref, lse_ref = rest
    for hu in range(n_heads * n_sub):
        hh, u = divmod(hu, n_sub)
        rows = slice(u * DIL_BLOCK, (u + 1) * DIL_BLOCK)
        cols = slice(hh * DIL_HEAD_DIM, (hh + 1) * DIL_HEAD_DIM)
        head = pl.program_id(2) * n_heads + hh
        bias = slopes_ref[group * DIL_HEADS_PER_GROUP + head] * float(dil)
        dist_c = bias * rel_c.astype(F32)
        dist_p = bias * (rel_c + DIL_BLOCK).astype(F32)
        q = q_ref[rows, cols]
        if u == 0:
            k_prev, v_prev = kp_ref[:, cols], vp_ref[:, cols]
            prev_lim = jnp.where(nblk == 0, -2 * DIL_BLOCK, 0)
        else:
            prows = slice((u - 1) * DIL_BLOCK, u * DIL_BLOCK)
            k_prev, v_prev = kc_ref[prows, cols], vc_ref[prows, cols]
            prev_lim = 0
        s_c = lax.dot_general(q, kc_ref[rows, cols], nt, preferred_element_type=F32) * scale
        s_p = lax.dot_general(q, k_prev, nt, preferred_element_type=F32) * scale
        s_c = jnp.where(rel_c >= 0, s_c - dist_c, NEG_INF)
        s_p = jnp.where(rel_c <= prev_lim, s_p - dist_p, NEG_INF)
        m = jnp.maximum(jnp.max(s_c, axis=-1, keepdims=True), jnp.max(s_p, axis=-1, keepdims=True))
        e_c = jnp.exp(s_c - m)
        e_p = jnp.exp(s_p - m)
        den = jnp.sum(e_c, axis=-1, keepdims=True) + jnp.sum(e_p, axis=-1, keepdims=True)
        acc = (jnp.dot(e_c.astype(BF16), vc_ref[rows, cols], preferred_element_type=F32)
               + jnp.dot(e_p.astype(BF16), v_prev, preferred_element_type=F32))
        o = acc / den
        lse = jnp.broadcast_to(m + jnp.log(den), (DIL_BLOCK, DIL_HEAD_DIM))
        if merge:
            l0, l1 = l0_ref[rows, cols], l1_ref[rows, cols]
            top = jnp.maximum(jnp.maximum(l0, l1), lse)
            w0, w1, w2 = jnp.exp(l0 - top), jnp.exp(l1 - top), jnp.exp(lse - top)
            y = (w0 * o0_ref[rows, cols] + w1 * o1_ref[rows, cols] + w2 * o) / (w0 + w1 + w2)
            y_ref[rows, cols] = y.astype(y_ref.dtype)
        else:
            o_ref[rows, cols] = o.astype(o_ref.dtype)
            lse_ref[rows, cols] = lse


def _dilated_group(slopes, zo, group, batch, s, prior=None):
    n = batch * s
    dil = DIL_PATTERNS[group][1]
    nb = s // dil // DIL_BLOCK
    n_sub = min(nb, 8)
    n_heads = min(8 // n_sub, DIL_HEADS_PER_GROUP)
    steps = nb // n_sub
    ocols = DIL_OUT_DIM // DIL_HEAD_DIM // n_heads
    rows = n_sub * DIL_BLOCK
    width = n_heads * DIL_HEAD_DIM
    gcol = lambda col0: col0 + group * DIL_OUT_DIM
    if dil == 1:
        zv, qkv_cols = zo, (gcol(ZO_DQ), gcol(ZO_DK), gcol(ZO_DV))
    else:
        zv = jnp.concatenate([zo[:, gcol(c):gcol(c) + DIL_OUT_DIM] for c in (ZO_DQ, ZO_DK, ZO_DV)], axis=1)
        zv, qkv_cols = zv.reshape(n // dil, dil * 3 * DIL_OUT_DIM), (0, DIL_OUT_DIM, 2 * DIL_OUT_DIM)
    zcols = zv.shape[1] // dil // width

    def cur(col0):
        c = col0 // width
        return pl.BlockSpec((rows, width), lambda b, r, h, i, sl: (b * steps + i, r * zcols + c + h))

    def prev(col0):
        c = col0 // width
        return pl.BlockSpec((DIL_BLOCK, width),
                            lambda b, r, h, i, sl: (jnp.maximum(b * nb + i * n_sub - 1, 0), r * zcols + c + h))

    ospec = pl.BlockSpec((rows, width), lambda b, r, h, i, sl: (b * steps + i, r * ocols + h))
    oshape = lambda dt: jax.ShapeDtypeStruct((n // dil, dil * DIL_OUT_DIM), dt)
    merge = prior is not None
    extra = [t.reshape(n // dil, dil * DIL_OUT_DIM) for t in prior] if merge else []
    qc, kc, vc = qkv_cols
    out = pl.pallas_call(
        functools.partial(_dil_body, group=group, dil=dil, n_sub=n_sub, n_heads=n_heads, merge=merge),
        grid_spec=pltpu.PrefetchScalarGridSpec(
            num_scalar_prefetch=1, grid=(batch, dil, DIL_HEADS_PER_GROUP // n_heads, steps),
            in_specs=[cur(qc), cur(kc), prev(kc), cur(vc), prev(vc)] + [ospec] * len(extra),
            out_specs=ospec if merge else [ospec, ospec]),
        out_shape=oshape(BF16) if merge else [oshape(BF16), oshape(F32)],
        compiler_params=_params("parallel", "parallel", "parallel", "arbitrary"),
        name="dilated_attn_g%d" % group,
    )(slopes, zv, zv, zv, zv, zv, *extra)
    if merge:
        return out.reshape(n, DIL_OUT_DIM)
    return [t.reshape(n, DIL_OUT_DIM) for t in out]


def _rms(x_bf16, g):
    x = x_bf16.astype(F32)
    return (x * lax.rsqrt(jnp.mean(x * x, axis=-1, keepdims=True) + RMS_EPS) * g).astype(BF16)


def _mla_q_body(x_ref, g_ref, w_ref, cos_ref, sin_ref, q_ref):
    acc = jnp.dot(_rms(x_ref[...], g_ref[...]), w_ref[...], preferred_element_type=F32)
    scale = MLA_QK_DIM ** -0.5
    cos, sin = cos_ref[...], sin_ref[...]
    nope_w = MLA_HEADS * MLA_NOPE_DIM
    for h in range(MLA_HEADS):
        q_ref[0, h, :, 0:MLA_NOPE_DIM] = (acc[:, h * 128:(h + 1) * 128] * scale).astype(BF16)
        base = nope_w + h * 256
        rope = acc[:, base:base + 128] * cos + acc[:, base + 128:base + 256] * sin
        q_ref[0, h, :, MLA_NOPE_DIM:MLA_QK_DIM] = (rope[:, 0:MLA_ROPE_DIM] * scale).astype(BF16)


def _mla_q(zo, g, w, cos, sin, batch, tm=512):
    n = zo.shape[0]
    s = n // batch
    nblk = s // tm
    return pl.pallas_call(
        _mla_q_body,
        grid=(batch, nblk),
        in_specs=[pl.BlockSpec((tm, MLA_Q_LORA), lambda b, i: (b * nblk + i, ZO_QA // MLA_Q_LORA)),
                  pl.BlockSpec((1, MLA_Q_LORA), lambda b, i: (0, 0)),
                  pl.BlockSpec(w.shape, lambda b, i: (0, 0)),
                  pl.BlockSpec((tm, 128), lambda b, i: (i, 0)),
                  pl.BlockSpec((tm, 128), lambda b, i: (i, 0))],
        out_specs=pl.BlockSpec((1, MLA_HEADS, tm, MLA_QK_DIM), lambda b, i: (b, 0, i, 0)),
        out_shape=jax.ShapeDtypeStruct((batch, MLA_HEADS, s, MLA_QK_DIM), BF16),
        compiler_params=_params("parallel", "parallel"),
        name="mla_q_proj",
    )(zo, g, w, cos, sin)


def _mla_kv_body(x_ref, kr_ref, g_ref, w_ref, cos_ref, sin_ref, k_ref, v_ref):
    acc = jnp.dot(_rms(x_ref[...], g_ref[...]), w_ref[...], preferred_element_type=F32)
    kr = kr_ref[...].astype(F32)
    rope = (kr[:, 0:128] * cos_ref[...] + kr[:, 128:256] * sin_ref[...])[:, 0:MLA_ROPE_DIM].astype(BF16)
    for h in range(MLA_HEADS):
        k_ref[0, h, :, 0:MLA_NOPE_DIM] = acc[:, h * 256:h * 256 + 128].astype(BF16)
        k_ref[0, h, :, MLA_NOPE_DIM:MLA_QK_DIM] = rope
        v_ref[0, h] = acc[:, h * 256 + 128:(h + 1) * 256].astype(BF16)


def _mla_kv(zo, g, w, cos, sin, batch, tm=512):
    n = zo.shape[0]
    s = n // batch
    nblk = s // tm
    return pl.pallas_call(
        _mla_kv_body,
        grid=(batch, nblk),
        in_specs=[pl.BlockSpec((tm, MLA_KV_LORA), lambda b, i: (b * nblk + i, ZT_KVA // MLA_KV_LORA)),
                  pl.BlockSpec((tm, 256), lambda b, i: (b * nblk + i, ZT_KR // 256)),
                  pl.BlockSpec((1, MLA_KV_LORA), lambda b, i: (0, 0)),
                  pl.BlockSpec(w.shape, lambda b, i: (0, 0)),
                  pl.BlockSpec((tm, 128), lambda b, i: (i, 0)),
                  pl.BlockSpec((tm, 128), lambda b, i: (i, 0))],
        out_specs=[pl.BlockSpec((1, MLA_HEADS, tm, MLA_QK_DIM), lambda b, i: (b, 0, i, 0)),
                   pl.BlockSpec((1, MLA_HEADS, tm, MLA_V_DIM), lambda b, i: (b, 0, i, 0))],
        out_shape=[jax.ShapeDtypeStruct((batch, MLA_HEADS, s, MLA_QK_DIM), BF16),
                   jax.ShapeDtypeStruct((batch, MLA_HEADS, s, MLA_V_DIM), BF16)],
        compiler_params=_params("parallel", "parallel"),
        name="mla_kv_proj",
    )(zo, zo, g, w, cos, sin)


def _flash_body(qi_ref, kj_ref, q_ref, k_ref, v_ref, o_ref, m_ref, l_ref, acc_ref, *, tq):
    p = pl.program_id(1)
    i = qi_ref[p]
    j = kj_ref[p]
    heads, dv = acc_ref.shape[0], acc_ref.shape[2]

    @pl.when(j == 0)
    def _():
        m_ref[...] = jnp.full_like(m_ref, NEG_INF)
        l_ref[...] = jnp.zeros_like(l_ref)
        acc_ref[...] = jnp.zeros_like(acc_ref)

    def update(masked):
        for h in range(heads):
            s = lax.dot_general(q_ref[0, h], k_ref[0, h], (((1,), (1,)), ((), ())), preferred_element_type=F32)
            if masked:
                qpos = lax.broadcasted_iota(jnp.int32, (tq, tq), 0)
                kpos = lax.broadcasted_iota(jnp.int32, (tq, tq), 1)
                s = jnp.where(kpos <= qpos, s, NEG_INF)
            m_old = m_ref[h]
            m_new = jnp.maximum(m_old, jnp.max(s, axis=-1, keepdims=True))
            alpha = jnp.exp(m_old - m_new)
            e = jnp.exp(s - jnp.concatenate([m_new] * (tq // dv), axis=1))
            l_ref[h] = alpha * l_ref[h] + jnp.sum(e, axis=-1, keepdims=True)
            acc_ref[h] = alpha * acc_ref[h] + jnp.dot(e.astype(BF16), v_ref[0, h], preferred_element_type=F32)
            m_ref[h] = m_new

    @pl.when(j < i)
    def _():
        update(False)

    @pl.when(j == i)
    def _():
        update(True)
        for h in range(heads):
            o_ref[0, :, h * dv:(h + 1) * dv] = (acc_ref[h] / l_ref[h]).astype(o_ref.dtype)


def _flash(q, k, v, tq=512):
    b, h, s, dq = q.shape
    dv = v.shape[-1]
    nq = s // tq
    pairs = [(i, j) for i in range(nq) for j in range(i + 1)]
    qi = jnp.asarray([pr[0] for pr in pairs], jnp.int32)
    kj = jnp.asarray([pr[1] for pr in pairs], jnp.int32)
    return pl.pallas_call(
        functools.partial(_flash_body, tq=tq),
        grid_spec=pltpu.PrefetchScalarGridSpec(
            num_scalar_prefetch=2, grid=(b, len(pairs)),
            in_specs=[pl.BlockSpec((1, h, tq, dq), lambda bi, p, qi, kj: (bi, 0, qi[p], 0)),
                      pl.BlockSpec((1, h, tq, dq), lambda bi, p, qi, kj: (bi, 0, kj[p], 0)),
                      pl.BlockSpec((1, h, tq, dv), lambda bi, p, qi, kj: (bi, 0, kj[p], 0))],
            out_specs=pl.BlockSpec((1, tq, h * dv), lambda bi, p, qi, kj: (bi, qi[p], 0)),
            scratch_shapes=[pltpu.VMEM((h, tq, dv), F32), pltpu.VMEM((h, tq, dv), F32), pltpu.VMEM((h, tq, dv), F32)]),
        out_shape=jax.ShapeDtypeStruct((b, s, h * dv), BF16),
        compiler_params=_params("parallel", "arbitrary"),
        name="mla_flash",
    )(qi, kj, q, k, v)


def _layernorm(h, g, b):
    mu = jnp.mean(h, axis=-1, keepdims=True)
    hc = h - mu
    var = jnp.mean(hc * hc, axis=-1, keepdims=True)
    return hc * lax.rsqrt(var + LN_EPS) * g + b


def _out_ln_body(ya_ref, yb_ref, yc_ref, w_ref, x_ref, g_ref, b_ref, o_ref, *, alpha):
    ka, kb = ya_ref.shape[1], yb_ref.shape[1]
    acc = (jnp.dot(ya_ref[...], w_ref[0:ka, :], preferred_element_type=F32)
           + jnp.dot(yb_ref[...], w_ref[ka:ka + kb, :], preferred_element_type=F32)
           + jnp.dot(yc_ref[...], w_ref[ka + kb:, :], preferred_element_type=F32))
    o_ref[...] = _layernorm(alpha * x_ref[...] + acc, g_ref[...], b_ref[...])


def _out_ln(ya, yb, yc, w, x, g, b, alpha, tm=256):
    n, d = x.shape
    row = lambda c: pl.BlockSpec((tm, c), lambda i: (i, 0))
    vec = pl.BlockSpec((1, d), lambda i: (0, 0))
    return pl.pallas_call(
        functools.partial(_out_ln_body, alpha=alpha),
        grid=(n // tm,),
        in_specs=[row(ya.shape[1]), row(yb.shape[1]), row(yc.shape[1]),
                  pl.BlockSpec(w.shape, lambda i: (0, 0)), row(d), vec, vec],
        out_specs=row(d),
        out_shape=jax.ShapeDtypeStruct((n, d), F32),
        compiler_params=_params("parallel"),
        name="out_proj_ln",
    )(ya, yb, yc, w, x, g, b)


def _router_body(x_ref, w_ref, b_ref, idx_ref, gate_ref):
    logits = lax.dot_general(w_ref[...], x_ref[...], (((1,), (1,)), ((), ())),
                             precision=HIGHEST, preferred_element_type=F32)
    scores = _sigmoid(logits)
    sel = scores + b_ref[...]
    ng = N_EXPERT_GROUPS
    sel_j = [sel[j * ng:(j + 1) * ng] for j in range(EXPERTS_PER_GROUP)]
    sc_j = [scores[j * ng:(j + 1) * ng] for j in range(EXPERTS_PER_GROUP)]
    grp = None
    for p in range(EXPERTS_PER_GROUP):
        for q in range(p + 1, EXPERTS_PER_GROUP):
            pair = sel_j[p] + sel_j[q]
            grp = pair if grp is None else jnp.maximum(grp, pair)
    gi = lax.broadcasted_iota(jnp.int32, grp.shape, 0)
    gmax = jnp.max(grp, axis=0, keepdims=True)
    gbest = jnp.min(jnp.where(grp == gmax, gi, ng), axis=0, keepdims=True)
    pick = gi == gbest
    mem = [jnp.sum(jnp.where(pick, t, 0.0), axis=0, keepdims=True) for t in sel_j]
    msc = [jnp.sum(jnp.where(pick, t, 0.0), axis=0, keepdims=True) for t in sc_j]

    def first_argmax(vals, exclude):
        best = jnp.full_like(vals[0], -jnp.inf)
        bi = jnp.zeros(vals[0].shape, jnp.int32)
        bs = jnp.zeros_like(vals[0])
        for j in range(EXPERTS_PER_GROUP):
            ok = vals[j] > best
            if exclude is not None:
                ok = ok & (exclude != j)
            best = jnp.where(ok, vals[j], best)
            bi = jnp.where(ok, j, bi)
            bs = jnp.where(ok, msc[j], bs)
        return bi, bs

    i1, s1 = first_argmax(mem, None)
    i2, s2 = first_argmax(mem, i1)
    tot = s1 + s2
    base = gbest * EXPERTS_PER_GROUP
    idx_ref[0:1, :] = base + i1
    idx_ref[1:2, :] = base + i2
    gate_ref[0:1, :] = s1 / tot
    gate_ref[1:2, :] = s2 / tot


def _router(x, w_t, bias, tm=512):
    n, d = x.shape
    return pl.pallas_call(
        _router_body,
        grid=(n // tm,),
        in_specs=[pl.BlockSpec((tm, d), lambda i: (i, 0)),
                  pl.BlockSpec((N_EXPERTS, d), lambda i: (0, 0)),
                  pl.BlockSpec((N_EXPERTS, 1), lambda i: (0, 0))],
        out_specs=[pl.BlockSpec((TOP_K, tm), lambda i: (0, i)), pl.BlockSpec((TOP_K, tm), lambda i: (0, i))],
        out_shape=[jax.ShapeDtypeStruct((TOP_K, n), jnp.int32), jax.ShapeDtypeStruct((TOP_K, n), F32)],
        compiler_params=_params("parallel"),
        name="moe_router",
    )(x, w_t, bias)


def _expert_body(blk_e_ref, meta_ref, rows_ref, x_hbm, wg_ref, wu_ref, wd_ref, out_hbm,
                 xg_ref, yo_ref, wgb_ref, wub_ref, wdb_ref, sem_in, sem_out, *, tok_bits, n_blocks):
    i = pl.program_id(0)
    slot = i & 1
    other = 1 - slot
    n_act = meta_ref[0]

    def start_gather(blk, sl):
        for r in range(MOE_BLOCK):
            tok = rows_ref[blk * MOE_BLOCK + r] & ((1 << tok_bits) - 1)
            pltpu.make_async_copy(x_hbm.at[pl.ds(tok, 1)], xg_ref.at[sl, pl.ds(r, 1)],
                                  sem_in.at[sl]).start(priority=r % 2)

    def wait_gather(sl):
        for r in range(MOE_BLOCK):
            pltpu.make_async_copy(x_hbm.at[pl.ds(0, 1)], xg_ref.at[sl, pl.ds(0, 1)], sem_in.at[sl]).wait()

    def start_scatter(blk, sl):
        for r in range(MOE_BLOCK):
            row = rows_ref[blk * MOE_BLOCK + r] >> tok_bits
            pltpu.make_async_copy(yo_ref.at[sl, pl.ds(r, 1)], out_hbm.at[pl.ds(row, 1)],
                                  sem_out.at[sl]).start(priority=r % 2)

    def wait_scatter(sl):
        for r in range(MOE_BLOCK):
            pltpu.make_async_copy(yo_ref.at[sl, pl.ds(0, 1)], out_hbm.at[pl.ds(0, 1)], sem_out.at[sl]).wait()

    def compute(sl):
        xb = xg_ref[sl].astype(BF16)
        gt = jnp.dot(xb, wgb_ref[...], preferred_element_type=F32)
        up = jnp.dot(xb, wub_ref[...], preferred_element_type=F32)
        hid = (gt * _sigmoid(gt) * up).astype(BF16)
        yo_ref[sl] = jnp.dot(hid, wdb_ref[...], preferred_element_type=F32)

    @pl.when((i == 0) | ((i < n_act) & (blk_e_ref[i] != blk_e_ref[jnp.maximum(i - 1, 0)])))
    def _():
        wgb_ref[...] = wg_ref[0, 0].astype(BF16)
        wub_ref[...] = wu_ref[0, 0].astype(BF16)
        wdb_ref[...] = wd_ref[0, 0].astype(BF16)

    @pl.when(i == 0)
    def _():
        start_gather(0, 0)
        yo_ref[0] = jnp.zeros(yo_ref.shape[1:], yo_ref.dtype)
        spare0 = out_hbm.shape[0] - 2 * MOE_BLOCK
        fills = [pltpu.make_async_copy(yo_ref.at[0], out_hbm.at[pl.ds(spare0 + h * MOE_BLOCK, MOE_BLOCK)],
                                       sem_out.at[0]) for h in range(2)]
        for cp in fills:
            cp.start()
        for cp in fills:
            cp.wait()
        wait_gather(0)
        start_gather(1, 1)
        compute(0)

    @pl.when((i > 0) & (i < n_act))
    def _():
        wait_gather(slot)

        @pl.when(i >= 2)
        def _():
            wait_scatter(slot)

        start_gather(jnp.minimum(i + 1, n_blocks - 1), other)
        start_scatter(i - 1, other)
        compute(slot)

    @pl.when(i == n_act - 1)
    def _():
        start_scatter(i, slot)
        wait_scatter(other)
        wait_scatter(slot)
        wait_gather(other)


def _experts(blk_e, meta, rows, x, wg, wu, wd, layer):
    n, d = x.shape
    nb = blk_e.shape[0]
    wspec = lambda shp: pl.BlockSpec((1, 1) + shp, lambda i, be, nv, sr: (layer, be[i], 0, 0))
    return pl.pallas_call(
        functools.partial(_expert_body, tok_bits=_tok_bits(n), n_blocks=nb),
        grid_spec=pltpu.PrefetchScalarGridSpec(
            num_scalar_prefetch=3, grid=(nb,),
            in_specs=[pl.BlockSpec(memory_space=pl.ANY),
                      wspec((d, D_EXPERT)), wspec((d, D_EXPERT)), wspec((D_EXPERT, d))],
            out_specs=pl.BlockSpec(memory_space=pl.ANY),
            scratch_shapes=[pltpu.VMEM((2, MOE_BLOCK, d), F32), pltpu.VMEM((2, MOE_BLOCK, d), F32),
                            pltpu.VMEM((d, D_EXPERT), BF16), pltpu.VMEM((d, D_EXPERT), BF16),
                            pltpu.VMEM((D_EXPERT, d), BF16),
                            pltpu.SemaphoreType.DMA((2,)), pltpu.SemaphoreType.DMA((2,))]),
        out_shape=jax.ShapeDtypeStruct((TOP_K * n + 2 * MOE_BLOCK, d), F32),
        compiler_params=_params("arbitrary"),
        name="moe_experts",
    )(blk_e, meta, rows, x, wg, wu, wd)


def _tok_bits(n):
    return max((n - 1).bit_length(), 1)


def _dispatch(idx):
    n = idx.shape[1]
    a_tot = TOP_K * n
    e_flat = idx.reshape(a_tot)
    onehot = (e_flat[:, None] == jnp.arange(N_EXPERTS, dtype=jnp.int32)[None, :]).astype(F32)
    chunk = 128
    oh = onehot.reshape(a_tot // chunk, chunk, N_EXPERTS)
    tri = (jnp.arange(chunk)[None, :] <= jnp.arange(chunk)[:, None]).astype(F32)
    within = jnp.einsum('ts,csn->ctn', tri, oh)
    tot = within[:, -1, :]
    csum = (within + (jnp.cumsum(tot, axis=0) - tot)[:, None, :]).reshape(a_tot, N_EXPERTS)
    rank = jnp.sum(onehot * csum, axis=1).astype(jnp.int32) - 1
    counts = csum[-1].astype(jnp.int32)
    padded = (counts + MOE_BLOCK - 1) // MOE_BLOCK * MOE_BLOCK
    pad_end = jnp.cumsum(padded)
    pad_start = pad_end - padded
    dest = pad_start[e_flat] + rank
    nb = (a_tot + N_EXPERTS * (MOE_BLOCK - 1) + MOE_BLOCK - 1) // MOE_BLOCK
    p_rows = nb * MOE_BLOCK
    bits = _tok_bits(n)
    a_ids = jnp.arange(a_tot, dtype=jnp.int32)
    pr = jnp.arange(p_rows, dtype=jnp.int32)
    spare = a_tot + ((pr // MOE_BLOCK) % 2) * MOE_BLOCK + pr % MOE_BLOCK
    rows = (spare << bits).at[dest].set((a_ids % n) | (a_ids << bits))
    blk_start = jnp.arange(nb, dtype=jnp.int32) * MOE_BLOCK
    blk_e = jnp.sum((pad_end[None, :] <= blk_start[:, None]).astype(jnp.int32), axis=1)
    blk_e = jnp.minimum(blk_e, N_EXPERTS - 1)
    n_act = jnp.minimum(pad_end[-1] // MOE_BLOCK + 1, nb).astype(jnp.int32)
    return blk_e, n_act.reshape(1), rows


def _final_body(x_ref, ya_ref, yb_ref, ga_ref, gb_ref, g_ref, b_ref, p_ref, wp_ref, wg_ref, o_ref, ob_ref, *, alpha):
    ffn = ya_ref[...] * ga_ref[...] + yb_ref[...] * gb_ref[...]
    x2 = _layernorm(alpha * x_ref[...] + ffn, g_ref[...], b_ref[...])
    gate = jnp.dot(x2.astype(BF16), wg_ref[...], preferred_element_type=F32)
    proj = jnp.dot(p_ref[...].astype(BF16), wp_ref[...], preferred_element_type=F32)
    out = x2 + _sigmoid(gate) * proj
    o_ref[...] = out
    ob_ref[...] = out.astype(BF16)


def _final(x, y2, gates, g, b, p, wp, wg, alpha, tm=256):
    n, d = x.shape
    nblk = n // tm
    row = pl.BlockSpec((tm, d), lambda i: (i, 0))
    vec = pl.BlockSpec((1, d), lambda i: (0, 0))
    return pl.pallas_call(
        functools.partial(_final_body, alpha=alpha),
        grid=(nblk,),
        in_specs=[row, row, pl.BlockSpec((tm, d), lambda i: (i + nblk, 0)),
                  pl.BlockSpec((tm, 1), lambda i: (i, 0)), pl.BlockSpec((tm, 1), lambda i: (i + nblk, 0)), vec, vec,
                  pl.BlockSpec((tm, PLE_DIM), lambda i: (i, 0)),
                  pl.BlockSpec((PLE_DIM, d), lambda i: (0, 0)),
                  pl.BlockSpec((d, d), lambda i: (0, 0))],
        out_specs=[row, row],
        out_shape=[jax.ShapeDtypeStruct((n, d), F32), jax.ShapeDtypeStruct((n, d), BF16)],
        compiler_params=_params("parallel"),
        name="moe_combine_ln_ple",
    )(x, y2, y2, gates, gates, g, b, p, wp, wg)


def _rot_cols(w):
    half = MLA_ROPE_DIM // 2
    return jnp.concatenate([-w[..., half:], w[..., :half]], axis=-1)


def _pad_lanes(w, width=128):
    return jnp.pad(w, [(0, 0)] * (w.ndim - 1) + [(0, width - w.shape[-1])])


def _rope_tables(s):
    half = MLA_ROPE_DIM // 2
    inv = ROPE_THETA ** (-jnp.arange(half, dtype=F32) / half)
    ang = jnp.arange(s, dtype=F32)[:, None] * inv[None, :]
    cos = jnp.concatenate([jnp.cos(ang), jnp.cos(ang)], axis=-1)
    sin = jnp.concatenate([jnp.sin(ang), jnp.sin(ang)], axis=-1)
    return _pad_lanes(cos), _pad_lanes(sin)


def kernel(x, p, w_in, rwkv_mu, rwkv_w0, rwkv_w_up, rwkv_a0, rwkv_a_up, rwkv_g_up, rwkv_k_k, rwkv_k_a, rwkv_r_k, rwkv_gn_g, rwkv_gn_b, mla_qa_g, mla_w_uq, mla_kva_g, mla_w_ukv, w_out, ln1_g, ln1_b, router_w, router_b, moe_w_gate, moe_w_up, moe_w_down, ln2_g, ln2_b, ple_w_proj, ple_w_gate):
    batch, s, d = x.shape
    depth = w_in.shape[0]
    n = batch * s
    alpha = (2 * depth) ** 0.25
    assert s % (DIL_PATTERNS[-1][1] * DIL_BLOCK) == 0 and s % 512 == 0

    cos, sin = _rope_tables(s)
    slopes = jnp.exp2(-ALIBI_MAX_BIAS * jnp.arange(1, 13, dtype=F32) / 12)
    perm = np.array([4 * g + j for j in range(EXPERTS_PER_GROUP) for g in range(N_EXPERT_GROUPS)])
    router_wt = router_w.T[perm]
    router_bt = router_b[perm].reshape(N_EXPERTS, 1)

    xf = x.reshape(n, d)
    xb = xf.astype(BF16)
    kva0 = RWKV_IN_DIM + ZO_DIM
    for li in range(depth):
        w_kr = w_in[li, :, kva0 + MLA_KV_LORA:]
        w_tail = jnp.concatenate([w_in[li, :, kva0:kva0 + MLA_KV_LORA], _pad_lanes(w_kr), _pad_lanes(_rot_cols(w_kr))],
                                 axis=1).astype(BF16)
        z_r = _in_proj(xb, w_in, li, 0, RWKV_IN_DIM, F32)
        z_o = _in_proj(xb, w_in, li, RWKV_IN_DIM, ZO_DIM, BF16)
        z_t = _matmul(xb, w_tail, BF16, 1024, ZT_DIM)

        w_up_pad = jnp.pad(rwkv_w_up[li], ((0, 64), (0, 0)))
        w_up_hi = w_up_pad.astype(BF16)
        w_up_pad = jnp.stack([w_up_hi, (w_up_pad - w_up_hi.astype(F32)).astype(BF16)])
        a_up_pad = jnp.pad(rwkv_a_up[li], ((64, 0), (0, 0))).astype(BF16)
        row = lambda t: t.reshape(1, -1)
        hpar = lambda t: t.reshape(RWKV_PAIRS, 1, PAIR_LANES)
        y_a = _rwkv_mix(z_r.reshape(batch, s, RWKV_IN_DIM), row(rwkv_mu[li]), row(rwkv_w0[li]), row(rwkv_a0[li]),
                        row(rwkv_k_k[li]), row(rwkv_k_a[li]), w_up_pad, a_up_pad, rwkv_g_up[li].astype(BF16),
                        hpar(rwkv_r_k[li]), hpar(rwkv_gn_g[li]), hpar(rwkv_gn_b[li])).reshape(n, RWKV_DIM)

        prior = _dilated_group(slopes, z_o, 0, batch, s) + _dilated_group(slopes, z_o, 1, batch, s)
        y_b = _dilated_group(slopes, z_o, 2, batch, s, prior=prior)

        wq = mla_w_uq[li].reshape(MLA_Q_LORA, MLA_HEADS, MLA_QK_DIM)
        wq_rope = wq[:, :, MLA_NOPE_DIM:]
        wq_all = jnp.concatenate(
            [wq[:, :, :MLA_NOPE_DIM].reshape(MLA_Q_LORA, -1),
             jnp.concatenate([_pad_lanes(wq_rope), _pad_lanes(_rot_cols(wq_rope))], axis=-1).reshape(MLA_Q_LORA, -1)],
            axis=1).astype(BF16)
        q_c = _mla_q(z_o, row(mla_qa_g[li]), wq_all, cos, sin, batch)
        k_c, v_c = _mla_kv(z_t, row(mla_kva_g[li]), mla_w_ukv[li].astype(BF16), cos, sin, batch)
        y_c = _flash(q_c, k_c, v_c).reshape(n, MLA_OUT_DIM)

        x1 = _out_ln(y_a, y_b, y_c, w_out[li].astype(BF16), xf, row(ln1_g[li]), row(ln1_b[li]), alpha)

        idx, gate = _router(x1, router_wt, router_bt)
        blk_e, meta, rows = _dispatch(idx)
        y2 = _experts(blk_e, meta, rows, x1, moe_w_gate, moe_w_up, moe_w_down, li)

        xf, xb = _final(x1, y2, gate.reshape(TOP_K * n, 1), row(ln2_g[li]), row(ln2_b[li]), p[li].reshape(n, PLE_DIM),
                        ple_w_proj[li].astype(BF16), ple_w_gate[li].astype(BF16), alpha)
    return xf.reshape(batch, s, d)
```

```python
import functools
import math

import numpy as np
import jax
import jax.numpy as jnp
from jax import lax
from jax.experimental import pallas as pl
from jax.experimental.pallas import tpu as pltpu

F32 = jnp.float32
BF16 = jnp.bfloat16
HIGHEST = lax.Precision.HIGHEST

PLE_DIM = 256
RWKV_HEADS = 12
RWKV_HEAD_DIM = 64
RWKV_DIM = RWKV_HEADS * RWKV_HEAD_DIM
RWKV_LORA_PAD = 128
RWKV_GATE_LORA = 128
RWKV_IN_DIM = 3 * RWKV_DIM + RWKV_LORA_PAD + RWKV_GATE_LORA
RWKV_GN_EPS = 64e-5
RWKV_CHUNK = 64
DIL_PATTERNS = ((128, 1), (512, 4), (2048, 16))
DIL_GROUPS = 3
DIL_HEADS_PER_GROUP = 4
DIL_HEAD_DIM = 128
DIL_QKV_DIM = DIL_GROUPS * DIL_HEADS_PER_GROUP * DIL_HEAD_DIM
DIL_OUT_DIM = DIL_HEADS_PER_GROUP * DIL_HEAD_DIM
DIL_BLOCK = 128
ALIBI_MAX_BIAS = 8.0
MLA_HEADS = 6
MLA_NOPE_DIM = 128
MLA_ROPE_DIM = 64
MLA_V_DIM = 128
MLA_Q_LORA = 512
MLA_KV_LORA = 256
MLA_QK_DIM = MLA_NOPE_DIM + MLA_ROPE_DIM
MLA_OUT_DIM = MLA_HEADS * MLA_V_DIM
ROPE_THETA = 10000.0
N_EXPERTS = 32
N_EXPERT_GROUPS = 8
EXPERTS_PER_GROUP = 4
TOP_K = 2
D_EXPERT = 512
MOE_BLOCK = 128
LN_EPS = 1e-5
RMS_EPS = 1e-6
NEG_INF = -1e30

ZO_DQ = 0
ZO_DK = DIL_QKV_DIM
ZO_DV = 2 * DIL_QKV_DIM
ZO_QA = 3 * DIL_QKV_DIM
ZO_DIM = ZO_QA + MLA_Q_LORA
ZT_KVA = 0
ZT_KR = MLA_KV_LORA
ZT_DIM = ZT_KR + 256

V7X_VMEM_LIMIT_BYTES = 48 * 1024 * 1024


def _params(*sem):
    return pltpu.CompilerParams(dimension_semantics=sem, vmem_limit_bytes=V7X_VMEM_LIMIT_BYTES)


def _sigmoid(x):
    return 1.0 / (1.0 + jnp.exp(-x))


def _mm_body(x_ref, w_ref, o_ref):
    o_ref[...] = jnp.dot(x_ref[...], w_ref[...], preferred_element_type=F32).astype(o_ref.dtype)


def _layer_spec(arr, layer):
    zeros = (0,) * (arr.ndim - 1)
    return pl.BlockSpec((None,) + arr.shape[1:], lambda *_: (layer,) + zeros)


def _matmul(x, w_all, layer, out_dtype, tm, tn):
    m, k = x.shape
    n = w_all.shape[2]
    return pl.pallas_call(
        _mm_body,
        grid=(m // tm, n // tn),
        in_specs=[pl.BlockSpec((tm, k), lambda i, j: (i, 0)),
                  pl.BlockSpec((None, k, tn), lambda i, j: (layer, 0, j))],
        out_specs=pl.BlockSpec((tm, tn), lambda i, j: (i, j)),
        out_shape=jax.ShapeDtypeStruct((m, n), out_dtype),
        compiler_params=_params("parallel", "arbitrary"),
        name="in_proj_tail",
    )(x, w_all)


def _in_proj_body(x_ref, w_ref, o_ref, wb_ref):
    @pl.when(pl.program_id(1) == 0)
    def _():
        wb_ref[...] = w_ref[0].astype(BF16)

    o_ref[...] = jnp.dot(x_ref[...], wb_ref[...], preferred_element_type=F32).astype(o_ref.dtype)


def _in_proj(x, w_all, layer, col0, ncols, out_dtype, tm=1024, tn=512):
    m, k = x.shape
    assert col0 % tn == 0 and ncols % tn == 0 and col0 + ncols <= w_all.shape[2]
    cb0 = col0 // tn
    return pl.pallas_call(
        _in_proj_body,
        grid=(ncols // tn, m // tm),
        in_specs=[pl.BlockSpec((tm, k), lambda j, i: (i, 0)),
                  pl.BlockSpec((1, k, tn), lambda j, i: (layer, 0, cb0 + j))],
        out_specs=pl.BlockSpec((tm, tn), lambda j, i: (i, j)),
        out_shape=jax.ShapeDtypeStruct((m, ncols), out_dtype),
        scratch_shapes=[pltpu.VMEM((k, tn), BF16)],
        compiler_params=_params("parallel", "arbitrary"),
        name="in_proj",
    )(x, w_all)


def _bf16_parts(x, n):
    parts = []
    for _ in range(n):
        part = x.astype(BF16)
        parts.append(part)
        x = x - part.astype(F32)
    return parts


def _bdot(a, b, dims):
    return lax.dot_general(a.astype(BF16), b.astype(BF16), (dims, ((0,), (0,))), preferred_element_type=F32)


def _bnt(a, b):
    return _bdot(a, b, ((2,), (2,)))


def _bnn(a, b):
    return _bdot(a, b, ((2,), (1,)))


def _btn(a, b):
    return _bdot(a, b, ((1,), (1,)))


RWKV_PAIRS = RWKV_HEADS // 2
PAIR_LANES = 2 * RWKV_HEAD_DIM


def _rwkv_body(z_ref, mu_ref, w0_ref, a0_ref, kk_ref, ka_ref, wup_ref, aup_ref, gup_ref, rk_ref, gng_ref, gnb_ref,
               y_ref, st_ref, carry_ref):
    @pl.when(pl.program_id(0) == 0)
    def _():
        st_ref[...] = jnp.zeros_like(st_ref)
        carry_ref[...] = jnp.zeros_like(carry_ref)

    for bi in range(z_ref.shape[0]):
        _rwkv_chunk(bi, z_ref, mu_ref, w0_ref, a0_ref, kk_ref, ka_ref, wup_ref, aup_ref, gup_ref, rk_ref, gng_ref,
                    gnb_ref, y_ref, st_ref, carry_ref)


def _rwkv_chunk(bi, z_ref, mu_ref, w0_ref, a0_ref, kk_ref, ka_ref, wup_ref, aup_ref, gup_ref, rk_ref, gng_ref, gnb_ref,
                y_ref, st_ref, carry_ref):
    t, d, np_, pl_ = RWKV_CHUNK, RWKV_DIM, RWKV_PAIRS, PAIR_LANES

    z = z_ref[bi]
    row = lax.broadcasted_iota(jnp.int32, z.shape, 0)
    zprev = jnp.where(row == 0, carry_ref[bi], pltpu.roll(z, 1, axis=0))
    carry_ref[bi] = z[t - 1:t, :]
    zs = z + (zprev - z) * mu_ref[...]
    r_w, k_w, v_w = zs[:, 0:d], zs[:, d:2 * d], zs[:, 2 * d:3 * d]
    lora = zs[:, 3 * d:3 * d + RWKV_LORA_PAD]
    gd = zs[:, 3 * d + RWKV_LORA_PAD:]
    th_hi, th_lo = _bf16_parts(jnp.tanh(lora), 2)
    u = (w0_ref[...] + jnp.dot(th_hi, wup_ref[0], preferred_element_type=F32)
         + jnp.dot(th_hi, wup_ref[1], preferred_element_type=F32)
         + jnp.dot(th_lo, wup_ref[0], preferred_element_type=F32))
    softplus = jnp.maximum(-u, 0.0) + jnp.log(1.0 + jnp.exp(-jnp.abs(u)))
    lw_w = -jnp.exp(-softplus - 0.5)
    a_w = _sigmoid(a0_ref[...] + jnp.dot(lora.astype(BF16), aup_ref[...], preferred_element_type=F32))
    g_w = jnp.dot(_sigmoid(gd).astype(BF16), gup_ref[...], preferred_element_type=F32)
    kmod_w = k_w * (1.0 + (a_w - 1.0) * ka_ref[...])
    kk_w = k_w * kk_ref[...]
    ti = lax.broadcasted_iota(jnp.int32, (t, t), 0)
    si = lax.broadcasted_iota(jnp.int32, (t, t), 1)
    incl = si <= ti
    strict = si < ti
    tri = incl.astype(BF16)
    lp_w = sum(jnp.dot(tri, part, preferred_element_type=F32) for part in _bf16_parts(lw_w, 3))

    pairs = lambda x: jnp.stack([x[:, p * pl_:(p + 1) * pl_] for p in range(np_)], axis=0)
    r, k, v, a, lw, lp, g = (pairs(x) for x in (r_w, kmod_w, v_w, a_w, lw_w, lp_w, g_w))
    li = lax.broadcasted_iota(jnp.int32, (pl_, pl_), 0)
    lj = lax.broadcasted_iota(jnp.int32, (pl_, pl_), 1)
    same_head = (li // RWKV_HEAD_DIM) == (lj // RWKV_HEAD_DIM)
    head_ones = same_head.astype(BF16)

    def head_sum(x):
        parts = _bf16_parts(x.reshape(np_ * t, pl_), 2)
        return sum(jnp.dot(part, head_ones, preferred_element_type=F32) for part in parts).reshape(np_, t, pl_)

    kk = pairs(kk_w)
    kk = kk / jnp.maximum(jnp.sqrt(head_sum(kk * kk)), 1e-12)
    lp_end = lp[:, t - 1:t, :]
    p_inv = jnp.exp(-lp)
    at = -kk * jnp.exp(lp - lw)
    bt = kk * a * p_inv
    kt = k * p_inv
    rt = r * jnp.exp(lp)
    to_end = jnp.exp(lp_end - lp)
    b_end = kk * a * to_end
    k_end = k * to_end

    lane = lax.broadcasted_iota(jnp.int32, (1, 1, pl_), 2)
    m0 = (lane < RWKV_HEAD_DIM).astype(F32)
    msk = jnp.concatenate([jnp.broadcast_to(m0, (np_, 1, pl_)), jnp.broadcast_to(1.0 - m0, (np_, 1, pl_))], axis=0)
    dup = lambda x: jnp.concatenate([x, x], axis=0)
    fold = lambda x: x[:np_] + x[np_:]
    lhs_a = dup(at) * msk
    lhs_r = dup(rt) * msk
    v2 = dup(v) * msk
    ar = jnp.concatenate([lhs_a, lhs_r], axis=1)
    x_b = _bnt(ar, dup(bt))
    x_k = _bnt(ar, dup(kt))
    a_ab = jnp.where(strict[None], x_b[:, :t], 0.0)
    a_rb = jnp.where(incl[None], x_b[:, t:], 0.0)
    a_ak = jnp.where(strict[None], x_k[:, :t], 0.0)
    a_rk = jnp.where(incl[None], x_k[:, t:], 0.0)

    sub = 16
    same_blk = ((ti // sub) == (si // sub))[None]
    eye = (ti == si).astype(F32)[None]
    ld = jnp.where(same_blk, a_ab, 0.0)
    lo = a_ab - ld
    dinv = eye + ld
    pw = ld
    for _ in range(3):
        pw = _bnn(pw, pw)
        dinv = dinv + _bnn(dinv, pw)
    n1 = _bnn(dinv, lo)
    n2 = _bnn(n1, n1)
    tinv = eye + n1 + n2 + _bnn(n1, n2)
    tinv = _bnn(tinv, dinv)

    akv = _bnn(a_ak, v2)
    wu = _bnn(tinv, jnp.concatenate([lhs_a, akv], axis=2))
    qy = _bnn(a_rb, wu)
    q = fold(lhs_r + qy[:, :, :pl_])
    y0 = fold(qy[:, :, pl_:] + _bnn(a_rk, v2))
    wt = fold(wu[:, :, :pl_])
    u0 = fold(wu[:, :, pl_:])

    s0 = st_ref[bi]
    y = _bnn(q, s0) + y0
    diag_end = jnp.where((li == lj)[None], jnp.exp(lp_end), 0.0)
    m_t = diag_end + jnp.where(same_head[None], _btn(b_end, wt), 0.0)
    c_t = jnp.where(same_head[None], _btn(b_end, u0) + _btn(k_end, v), 0.0)
    st_ref[bi] = _bnn(m_t, s0) + c_t

    inv_e = 1.0 / RWKV_HEAD_DIM
    yc = y - head_sum(y) * inv_e
    yv = head_sum(yc * yc) * inv_e
    yn = yc * lax.rsqrt(yv + RWKV_GN_EPS) * gng_ref[...] + gnb_ref[...]
    out = (yn + head_sum(r * k * rk_ref[...]) * v) * g
    for p in range(np_):
        y_ref[bi, :, p * pl_:(p + 1) * pl_] = out[p].astype(y_ref.dtype)


def _rwkv_mix(z, layer, *params):
    b, s, zin = z.shape
    d, t = RWKV_DIM, RWKV_CHUNK
    return pl.pallas_call(
        _rwkv_body,
        grid=(s // t,),
        in_specs=[pl.BlockSpec((b, t, zin), lambda c: (0, c, 0))] + [_layer_spec(a, layer) for a in params],
        out_specs=pl.BlockSpec((b, t, d), lambda c: (0, c, 0)),
        out_shape=jax.ShapeDtypeStruct((b, s, d), BF16),
        scratch_shapes=[pltpu.VMEM((b, RWKV_PAIRS, PAIR_LANES, PAIR_LANES), F32), pltpu.VMEM((b, 1, zin), F32)],
        compiler_params=_params("arbitrary"),
        name="rwkv_mix",
    )(z, *params)


def _dil_body(slopes_ref, q_ref, kc_ref, kp_ref, vc_ref, vp_ref, *rest, group, dil, n_sub, n_heads, merge):
    nblk = pl.program_id(3)
    scale = DIL_HEAD_DIM ** -0.5
    nt = (((1,), (1,)), ((), ()))
    qi = lax.broadcasted_iota(jnp.int32, (DIL_BLOCK, DIL_BLOCK), 0)
    ki = lax.broadcasted_iota(jnp.int32, (DIL_BLOCK, DIL_BLOCK), 1)
    rel_c = qi - ki
    if merge:
        o0_ref, l0_ref, o1_ref, l1_ref, y_ref = rest
    else:
        o_ref, lse_ref = rest
    for hu in range(n_heads * n_sub):
        hh, u = divmod(hu, n_sub)
        rows = slice(u * DIL_BLOCK, (u + 1) * DIL_BLOCK)
        cols = slice(hh * DIL_HEAD_DIM, (hh + 1) * DIL_HEAD_DIM)
        head = pl.program_id(2) * n_heads + hh
        bias = slopes_ref[group * DIL_HEADS_PER_GROUP + head] * float(dil)
        dist_c = bias * rel_c.astype(F32)
        dist_p = bias * (rel_c + DIL_BLOCK).astype(F32)
        q = q_ref[rows, cols]
        if u == 0:
            k_prev, v_prev = kp_ref[:, cols], vp_ref[:, cols]
            prev_lim = jnp.where(nblk == 0, -2 * DIL_BLOCK, 0)
        else:
            prows = slice((u - 1) * DIL_BLOCK, u * DIL_BLOCK)
            k_prev, v_prev = kc_ref[prows, cols], vc_ref[prows, cols]
            prev_lim = 0
        s_c = lax.dot_general(q, kc_ref[rows, cols], nt, preferred_element_type=F32) * scale
        s_p = lax.dot_general(q, k_prev, nt, preferred_element_type=F32) * scale
        s_c = jnp.where(rel_c >= 0, s_c - dist_c, NEG_INF)
        s_p = jnp.where(rel_c <= prev_lim, s_p - dist_p, NEG_INF)
        m = jnp.maximum(jnp.max(s_c, axis=-1, keepdims=True), jnp.max(s_p, axis=-1, keepdims=True))
        e_c = jnp.exp(s_c - m)
        e_p = jnp.exp(s_p - m)
        den = jnp.sum(e_c, axis=-1, keepdims=True) + jnp.sum(e_p, axis=-1, keepdims=True)
        acc = (jnp.dot(e_c.astype(BF16), vc_ref[rows, cols], preferred_element_type=F32)
               + jnp.dot(e_p.astype(BF16), v_prev, preferred_element_type=F32))
        o = acc / den
        lse = jnp.broadcast_to(m + jnp.log(den), (DIL_BLOCK, DIL_HEAD_DIM))
        if merge:
            l0, l1 = l0_ref[rows, cols], l1_ref[rows, cols]
            top = jnp.maximum(jnp.maximum(l0, l1), lse)
            w0, w1, w2 = jnp.exp(l0 - top), jnp.exp(l1 - top), jnp.exp(lse - top)
            y = (w0 * o0_ref[rows, cols] + w1 * o1_ref[rows, cols] + w2 * o) / (w0 + w1 + w2)
            y_ref[rows, cols] = y.astype(y_ref.dtype)
        else:
            o_ref[rows, cols] = o.astype(o_ref.dtype)
            lse_ref[rows, cols] = lse


def _dilated_group(slopes, zo, group, batch, s, prior=None):
    n = batch * s
    dil = DIL_PATTERNS[group][1]
    nb = s // dil // DIL_BLOCK
    n_sub = min(nb, 8)
    n_heads = min(8 // n_sub, DIL_HEADS_PER_GROUP)
    steps = nb // n_sub
    ocols = DIL_OUT_DIM // DIL_HEAD_DIM // n_heads
    rows = n_sub * DIL_BLOCK
    width = n_heads * DIL_HEAD_DIM
    gcol = lambda col0: col0 + group * DIL_OUT_DIM
    if dil == 1:
        zv, qkv_cols = zo, (gcol(ZO_DQ), gcol(ZO_DK), gcol(ZO_DV))
    else:
        zv = jnp.concatenate([zo[:, gcol(c):gcol(c) + DIL_OUT_DIM] for c in (ZO_DQ, ZO_DK, ZO_DV)], axis=1)
        zv, qkv_cols = zv.reshape(n // dil, dil * 3 * DIL_OUT_DIM), (0, DIL_OUT_DIM, 2 * DIL_OUT_DIM)
    zcols = zv.shape[1] // dil // width

    def cur(col0):
        c = col0 // width
        return pl.BlockSpec((rows, width), lambda b, r, h, i, sl: (b * steps + i, r * zcols + c + h))

    def prev(col0):
        c = col0 // width
        return pl.BlockSpec((DIL_BLOCK, width),
                            lambda b, r, h, i, sl: (jnp.maximum(b * nb + i * n_sub - 1, 0), r * zcols + c + h))

    ospec = pl.BlockSpec((rows, width), lambda b, r, h, i, sl: (b * steps + i, r * ocols + h))
    oshape = lambda dt: jax.ShapeDtypeStruct((n // dil, dil * DIL_OUT_DIM), dt)
    merge = prior is not None
    extra = [t.reshape(n // dil, dil * DIL_OUT_DIM) for t in prior] if merge else []
    qc, kc, vc = qkv_cols
    out = pl.pallas_call(
        functools.partial(_dil_body, group=group, dil=dil, n_sub=n_sub, n_heads=n_heads, merge=merge),
        grid_spec=pltpu.PrefetchScalarGridSpec(
            num_scalar_prefetch=1, grid=(batch, dil, DIL_HEADS_PER_GROUP // n_heads, steps),
            in_specs=[cur(qc), cur(kc), prev(kc), cur(vc), prev(vc)] + [ospec] * len(extra),
            out_specs=ospec if merge else [ospec, ospec]),
        out_shape=oshape(BF16) if merge else [oshape(BF16), oshape(F32)],
        compiler_params=_params("parallel", "parallel", "parallel", "arbitrary"),
        name="dilated_attn_g%d" % group,
    )(slopes, zv, zv, zv, zv, zv, *extra)
    if merge:
        return out.reshape(n, DIL_OUT_DIM)
    return [t.reshape(n, DIL_OUT_DIM) for t in out]


def _rms(x_bf16, g):
    x = x_bf16.astype(F32)
    return (x * lax.rsqrt(jnp.mean(x * x, axis=-1, keepdims=True) + RMS_EPS) * g).astype(BF16)


def _mla_q_body(x_ref, g_ref, w_ref, cos_ref, sin_ref, q_ref):
    acc = jnp.dot(_rms(x_ref[...], g_ref[...]), w_ref[...], preferred_element_type=F32)
    scale = MLA_QK_DIM ** -0.5
    cos, sin = cos_ref[...], sin_ref[...]
    nope_w = MLA_HEADS * MLA_NOPE_DIM
    for h in range(MLA_HEADS):
        q_ref[0, h, :, 0:MLA_NOPE_DIM] = (acc[:, h * 128:(h + 1) * 128] * scale).astype(BF16)
        base = nope_w + h * 256
        rope = acc[:, base:base + 128] * cos + acc[:, base + 128:base + 256] * sin
        q_ref[0, h, :, MLA_NOPE_DIM:MLA_QK_DIM] = (rope[:, 0:MLA_ROPE_DIM] * scale).astype(BF16)


def _mla_q(zo, g, w, layer, cos, sin, batch, tm=512):
    n = zo.shape[0]
    s = n // batch
    nblk = s // tm
    return pl.pallas_call(
        _mla_q_body,
        grid=(batch, nblk),
        in_specs=[pl.BlockSpec((tm, MLA_Q_LORA), lambda b, i: (b * nblk + i, ZO_QA // MLA_Q_LORA)),
                  _layer_spec(g, layer), _layer_spec(w, layer),
                  pl.BlockSpec((tm, 128), lambda b, i: (i, 0)),
                  pl.BlockSpec((tm, 128), lambda b, i: (i, 0))],
        out_specs=pl.BlockSpec((1, MLA_HEADS, tm, MLA_QK_DIM), lambda b, i: (b, 0, i, 0)),
        out_shape=jax.ShapeDtypeStruct((batch, MLA_HEADS, s, MLA_QK_DIM), BF16),
        compiler_params=_params("parallel", "parallel"),
        name="mla_q_proj",
    )(zo, g, w, cos, sin)


def _mla_kv_body(x_ref, kr_ref, g_ref, w_ref, cos_ref, sin_ref, k_ref, v_ref):
    acc = jnp.dot(_rms(x_ref[...], g_ref[...]), w_ref[...], preferred_element_type=F32)
    kr = kr_ref[...].astype(F32)
    rope = (kr[:, 0:128] * cos_ref[...] + kr[:, 128:256] * sin_ref[...])[:, 0:MLA_ROPE_DIM].astype(BF16)
    for h in range(MLA_HEADS):
        k_ref[0, h, :, 0:MLA_NOPE_DIM] = acc[:, h * 256:h * 256 + 128].astype(BF16)
        k_ref[0, h, :, MLA_NOPE_DIM:MLA_QK_DIM] = rope
        v_ref[0, h] = acc[:, h * 256 + 128:(h + 1) * 256].astype(BF16)


def _mla_kv(zo, g, w, layer, cos, sin, batch, tm=512):
    n = zo.shape[0]
    s = n // batch
    nblk = s // tm
    return pl.pallas_call(
        _mla_kv_body,
        grid=(batch, nblk),
        in_specs=[pl.BlockSpec((tm, MLA_KV_LORA), lambda b, i: (b * nblk + i, ZT_KVA // MLA_KV_LORA)),
                  pl.BlockSpec((tm, 256), lambda b, i: (b * nblk + i, ZT_KR // 256)),
                  _layer_spec(g, layer), _layer_spec(w, layer),
                  pl.BlockSpec((tm, 128), lambda b, i: (i, 0)),
                  pl.BlockSpec((tm, 128), lambda b, i: (i, 0))],
        out_specs=[pl.BlockSpec((1, MLA_HEADS, tm, MLA_QK_DIM), lambda b, i: (b, 0, i, 0)),
                   pl.BlockSpec((1, MLA_HEADS, tm, MLA_V_DIM), lambda b, i: (b, 0, i, 0))],
        out_shape=[jax.ShapeDtypeStruct((batch, MLA_HEADS, s, MLA_QK_DIM), BF16),
                   jax.ShapeDtypeStruct((batch, MLA_HEADS, s, MLA_V_DIM), BF16)],
        compiler_params=_params("parallel", "parallel"),
        name="mla_kv_proj",
    )(zo, zo, g, w, cos, sin)


def _flash_body(qi_ref, kj_ref, q_ref, k_ref, v_ref, o_ref, m_ref, l_ref, acc_ref, *, tq):
    p = pl.program_id(1)
    i = qi_ref[p]
    j = kj_ref[p]
    heads, dv = acc_ref.shape[0], acc_ref.shape[2]

    @pl.when(j == 0)
    def _():
        m_ref[...] = jnp.full_like(m_ref, NEG_INF)
        l_ref[...] = jnp.zeros_like(l_ref)
        acc_ref[...] = jnp.zeros_like(acc_ref)

    def update(masked):
        for h in range(heads):
            s = lax.dot_general(q_ref[0, h], k_ref[0, h], (((1,), (1,)), ((), ())), preferred_element_type=F32)
            if masked:
                qpos = lax.broadcasted_iota(jnp.int32, (tq, tq), 0)
                kpos = lax.broadcasted_iota(jnp.int32, (tq, tq), 1)
                s = jnp.where(kpos <= qpos, s, NEG_INF)
            m_old = m_ref[h]
            m_new = jnp.maximum(m_old, jnp.max(s, axis=-1, keepdims=True))
            alpha = jnp.exp(m_old - m_new)
            e = jnp.exp(s - jnp.concatenate([m_new] * (tq // dv), axis=1))
            l_ref[h] = alpha * l_ref[h] + jnp.sum(e, axis=-1, keepdims=True)
            acc_ref[h] = alpha * acc_ref[h] + jnp.dot(e.astype(BF16), v_ref[0, h], preferred_element_type=F32)
            m_ref[h] = m_new

    @pl.when(j < i)
    def _():
        update(False)

    @pl.when(j == i)
    def _():
        update(True)
        for h in range(heads):
            o_ref[0, :, h * dv:(h + 1) * dv] = (acc_ref[h] / l_ref[h]).astype(o_ref.dtype)


def _flash(q, k, v, tq=512):
    b, h, s, dq = q.shape
    dv = v.shape[-1]
    nq = s // tq
    pairs = [(i, j) for i in range(nq) for j in range(i + 1)]
    qi = jnp.asarray([pr[0] for pr in pairs], jnp.int32)
    kj = jnp.asarray([pr[1] for pr in pairs], jnp.int32)
    return pl.pallas_call(
        functools.partial(_flash_body, tq=tq),
        grid_spec=pltpu.PrefetchScalarGridSpec(
            num_scalar_prefetch=2, grid=(b, len(pairs)),
            in_specs=[pl.BlockSpec((1, h, tq, dq), lambda bi, p, qi, kj: (bi, 0, qi[p], 0)),
                      pl.BlockSpec((1, h, tq, dq), lambda bi, p, qi, kj: (bi, 0, kj[p], 0)),
                      pl.BlockSpec((1, h, tq, dv), lambda bi, p, qi, kj: (bi, 0, kj[p], 0))],
            out_specs=pl.BlockSpec((1, tq, h * dv), lambda bi, p, qi, kj: (bi, qi[p], 0)),
            scratch_shapes=[pltpu.VMEM((h, tq, dv), F32), pltpu.VMEM((h, tq, dv), F32), pltpu.VMEM((h, tq, dv), F32)]),
        out_shape=jax.ShapeDtypeStruct((b, s, h * dv), BF16),
        compiler_params=_params("parallel", "arbitrary"),
        name="mla_flash",
    )(qi, kj, q, k, v)


def _layernorm(h, g, b):
    mu = jnp.mean(h, axis=-1, keepdims=True)
    hc = h - mu
    var = jnp.mean(hc * hc, axis=-1, keepdims=True)
    return hc * lax.rsqrt(var + LN_EPS) * g + b


def _out_ln_body(ya_ref, yb_ref, yc_ref, w_ref, x_ref, g_ref, b_ref, o_ref, *, alpha):
    ka, kb = ya_ref.shape[1], yb_ref.shape[1]
    acc = (jnp.dot(ya_ref[...], w_ref[0:ka, :], preferred_element_type=F32)
           + jnp.dot(yb_ref[...], w_ref[ka:ka + kb, :], preferred_element_type=F32)
           + jnp.dot(yc_ref[...], w_ref[ka + kb:, :], preferred_element_type=F32))
    o_ref[...] = _layernorm(alpha * x_ref[...] + acc, g_ref[...], b_ref[...])


def _out_ln(ya, yb, yc, w, x, g, b, layer, alpha, tm=256):
    n, d = x.shape
    row = lambda c: pl.BlockSpec((tm, c), lambda i: (i, 0))
    return pl.pallas_call(
        functools.partial(_out_ln_body, alpha=alpha),
        grid=(n // tm,),
        in_specs=[row(ya.shape[1]), row(yb.shape[1]), row(yc.shape[1]),
                  _layer_spec(w, layer), row(d), _layer_spec(g, layer), _layer_spec(b, layer)],
        out_specs=row(d),
        out_shape=jax.ShapeDtypeStruct((n, d), F32),
        compiler_params=_params("parallel"),
        name="out_proj_ln",
    )(ya, yb, yc, w, x, g, b)


def _router_body(x_ref, w_ref, b_ref, idx_ref, gate_ref):
    logits = lax.dot_general(w_ref[...], x_ref[...], (((1,), (1,)), ((), ())),
                             precision=HIGHEST, preferred_element_type=F32)
    scores = _sigmoid(logits)
    sel = scores + b_ref[...]
    ng = N_EXPERT_GROUPS
    sel_j = [sel[j * ng:(j + 1) * ng] for j in range(EXPERTS_PER_GROUP)]
    sc_j = [scores[j * ng:(j + 1) * ng] for j in range(EXPERTS_PER_GROUP)]
    grp = None
    for p in range(EXPERTS_PER_GROUP):
        for q in range(p + 1, EXPERTS_PER_GROUP):
            pair = sel_j[p] + sel_j[q]
            grp = pair if grp is None else jnp.maximum(grp, pair)
    gi = lax.broadcasted_iota(jnp.int32, grp.shape, 0)
    gmax = jnp.max(grp, axis=0, keepdims=True)
    gbest = jnp.min(jnp.where(grp == gmax, gi, ng), axis=0, keepdims=True)
    pick = gi == gbest
    mem = [jnp.sum(jnp.where(pick, t, 0.0), axis=0, keepdims=True) for t in sel_j]
    msc = [jnp.sum(jnp.where(pick, t, 0.0), axis=0, keepdims=True) for t in sc_j]

    def first_argmax(vals, exclude):
        best = jnp.full_like(vals[0], -jnp.inf)
        bi = jnp.zeros(vals[0].shape, jnp.int32)
        bs = jnp.zeros_like(vals[0])
        for j in range(EXPERTS_PER_GROUP):
            ok = vals[j] > best
            if exclude is not None:
                ok = ok & (exclude != j)
            best = jnp.where(ok, vals[j], best)
            bi = jnp.where(ok, j, bi)
            bs = jnp.where(ok, msc[j], bs)
        return bi, bs

    i1, s1 = first_argmax(mem, None)
    i2, s2 = first_argmax(mem, i1)
    tot = s1 + s2
    base = gbest * EXPERTS_PER_GROUP
    idx_ref[0:1, :] = base + i1
    idx_ref[1:2, :] = base + i2
    gate_ref[0:1, :] = s1 / tot
    gate_ref[1:2, :] = s2 / tot


def _router(x, w_t, bias, tm=512):
    n, d = x.shape
    return pl.pallas_call(
        _router_body,
        grid=(n // tm,),
        in_specs=[pl.BlockSpec((tm, d), lambda i: (i, 0)),
                  pl.BlockSpec((N_EXPERTS, d), lambda i: (0, 0)),
                  pl.BlockSpec((N_EXPERTS, 1), lambda i: (0, 0))],
        out_specs=[pl.BlockSpec((TOP_K, tm), lambda i: (0, i)), pl.BlockSpec((TOP_K, tm), lambda i: (0, i))],
        out_shape=[jax.ShapeDtypeStruct((TOP_K, n), jnp.int32), jax.ShapeDtypeStruct((TOP_K, n), F32)],
        compiler_params=_params("parallel"),
        name="moe_router",
    )(x, w_t, bias)


def _expert_body(blk_e_ref, meta_ref, rows_ref, x_hbm, wg_ref, wu_ref, wd_ref, out_hbm,
                 xg0_ref, xg1_ref, yo0_ref, yo1_ref, wgb_ref, wub_ref, wdb_ref, sem_in, sem_out, *, tok_bits, n_blocks):
    i = pl.program_id(0)
    n_act = meta_ref[0]
    xg = (xg0_ref, xg1_ref)
    yo = (yo0_ref, yo1_ref)

    def start_gather(blk, par):
        for r in range(MOE_BLOCK):
            tok = rows_ref[blk * MOE_BLOCK + r] & ((1 << tok_bits) - 1)
            pltpu.make_async_copy(x_hbm.at[pl.ds(tok, 1)], xg[par].at[pl.ds(r, 1)],
                                  sem_in.at[par]).start(priority=r % 2)

    def wait_gather(par):
        for r in range(MOE_BLOCK):
            pltpu.make_async_copy(x_hbm.at[pl.ds(0, 1)], xg[par].at[pl.ds(0, 1)], sem_in.at[par]).wait()

    def start_scatter(blk, par):
        for r in range(MOE_BLOCK):
            row = rows_ref[blk * MOE_BLOCK + r] >> tok_bits
            pltpu.make_async_copy(yo[par].at[pl.ds(r, 1)], out_hbm.at[pl.ds(row, 1)],
                                  sem_out.at[par]).start(priority=r % 2)

    def wait_scatter(par):
        for r in range(MOE_BLOCK):
            pltpu.make_async_copy(yo[par].at[pl.ds(0, 1)], out_hbm.at[pl.ds(0, 1)], sem_out.at[par]).wait()

    def compute(par):
        xb = xg[par][...].astype(BF16)
        gt = jnp.dot(xb, wgb_ref[...], preferred_element_type=F32)
        up = jnp.dot(xb, wub_ref[...], preferred_element_type=F32)
        hid = (gt * _sigmoid(gt) * up).astype(BF16)
        yo[par][...] = jnp.dot(hid, wdb_ref[...], preferred_element_type=F32)

    @pl.when((i == 0) | ((i < n_act) & (blk_e_ref[i] != blk_e_ref[jnp.maximum(i - 1, 0)])))
    def _():
        wgb_ref[...] = wg_ref[0, 0].astype(BF16)
        wub_ref[...] = wu_ref[0, 0].astype(BF16)
        wdb_ref[...] = wd_ref[0, 0].astype(BF16)

    @pl.when(i == 0)
    def _():
        start_gather(0, 0)
        yo0_ref[...] = jnp.zeros_like(yo0_ref)
        spare0 = out_hbm.shape[0] - 2 * MOE_BLOCK
        fills = [pltpu.make_async_copy(yo0_ref, out_hbm.at[pl.ds(spare0 + h * MOE_BLOCK, MOE_BLOCK)], sem_out.at[0])
                 for h in range(2)]
        for cp in fills:
            cp.start()
        for cp in fills:
            cp.wait()
        wait_gather(0)
        start_gather(1, 1)
        compute(0)

    for par in range(2):
        @pl.when((i > 0) & (i < n_act) & ((i & 1) == par))
        def _(par=par):
            wait_gather(par)

            @pl.when(i >= 2)
            def _():
                wait_scatter(par)

            start_gather(jnp.minimum(i + 1, n_blocks - 1), 1 - par)
            start_scatter(i - 1, 1 - par)
            compute(par)

    for par in range(2):
        @pl.when((i == n_act - 1) & ((i & 1) == par))
        def _(par=par):
            start_scatter(i, par)
            wait_scatter(1 - par)
            wait_scatter(par)
            wait_gather(1 - par)


def _experts(blk_e, meta, rows, x, wg, wu, wd, layer):
    n, d = x.shape
    nb = blk_e.shape[0]
    wspec = lambda shp: pl.BlockSpec((1, 1) + shp, lambda i, be, nv, sr: (layer, be[i], 0, 0))
    return pl.pallas_call(
        functools.partial(_expert_body, tok_bits=_tok_bits(n), n_blocks=nb),
        grid_spec=pltpu.PrefetchScalarGridSpec(
            num_scalar_prefetch=3, grid=(nb,),
            in_specs=[pl.BlockSpec(memory_space=pl.ANY),
                      wspec((d, D_EXPERT)), wspec((d, D_EXPERT)), wspec((D_EXPERT, d))],
            out_specs=pl.BlockSpec(memory_space=pl.ANY),
            scratch_shapes=[pltpu.VMEM((MOE_BLOCK, d), F32), pltpu.VMEM((MOE_BLOCK, d), F32),
                            pltpu.VMEM((MOE_BLOCK, d), F32), pltpu.VMEM((MOE_BLOCK, d), F32),
                            pltpu.VMEM((d, D_EXPERT), BF16), pltpu.VMEM((d, D_EXPERT), BF16),
                            pltpu.VMEM((D_EXPERT, d), BF16),
                            pltpu.SemaphoreType.DMA((2,)), pltpu.SemaphoreType.DMA((2,))]),
        out_shape=jax.ShapeDtypeStruct((TOP_K * n + 2 * MOE_BLOCK, d), F32),
        compiler_params=_params("arbitrary"),
        name="moe_experts",
    )(blk_e, meta, rows, x, wg, wu, wd)


def _tok_bits(n):
    return max((n - 1).bit_length(), 1)


def _dispatch(idx):
    n = idx.shape[1]
    a_tot = TOP_K * n
    e_flat = idx.reshape(a_tot)
    onehot = (e_flat[:, None] == jnp.arange(N_EXPERTS, dtype=jnp.int32)[None, :]).astype(F32)
    chunk = 128
    oh = onehot.reshape(a_tot // chunk, chunk, N_EXPERTS)
    tri = (jnp.arange(chunk)[None, :] <= jnp.arange(chunk)[:, None]).astype(F32)
    within = jnp.einsum('ts,csn->ctn', tri, oh)
    tot = within[:, -1, :]
    csum = (within + (jnp.cumsum(tot, axis=0) - tot)[:, None, :]).reshape(a_tot, N_EXPERTS)
    rank = jnp.sum(onehot * csum, axis=1).astype(jnp.int32) - 1
    counts = csum[-1].astype(jnp.int32)
    padded = (counts + MOE_BLOCK - 1) // MOE_BLOCK * MOE_BLOCK
    pad_end = jnp.cumsum(padded)
    pad_start = pad_end - padded
    dest = pad_start[e_flat] + rank
    nb = (a_tot + N_EXPERTS * (MOE_BLOCK - 1) + MOE_BLOCK - 1) // MOE_BLOCK
    p_rows = nb * MOE_BLOCK
    bits = _tok_bits(n)
    a_ids = jnp.arange(a_tot, dtype=jnp.int32)
    pr = jnp.arange(p_rows, dtype=jnp.int32)
    spare = a_tot + ((pr // MOE_BLOCK) % 2) * MOE_BLOCK + pr % MOE_BLOCK
    rows = (spare << bits).at[dest].set((a_ids % n) | (a_ids << bits))
    blk_start = jnp.arange(nb, dtype=jnp.int32) * MOE_BLOCK
    blk_e = jnp.sum((pad_end[None, :] <= blk_start[:, None]).astype(jnp.int32), axis=1)
    blk_e = jnp.minimum(blk_e, N_EXPERTS - 1)
    n_act = jnp.minimum(pad_end[-1] // MOE_BLOCK + 1, nb).astype(jnp.int32)
    return blk_e, n_act.reshape(1), rows


def _final_body(x_ref, ya_ref, yb_ref, ga_ref, gb_ref, g_ref, b_ref, p_ref, wp_ref, wg_ref, o_ref, ob_ref, *, alpha):
    ffn = ya_ref[...] * ga_ref[...] + yb_ref[...] * gb_ref[...]
    x2 = _layernorm(alpha * x_ref[...] + ffn, g_ref[...], b_ref[...])
    gate = jnp.dot(x2.astype(BF16), wg_ref[...], preferred_element_type=F32)
    proj = jnp.dot(p_ref[...].astype(BF16), wp_ref[...], preferred_element_type=F32)
    out = x2 + _sigmoid(gate) * proj
    o_ref[...] = out
    ob_ref[...] = out.astype(BF16)


def _final(x, y2, gates, g, b, p, wp, wg, layer, alpha, tm=256):
    n, d = x.shape
    nblk = n // tm
    row = pl.BlockSpec((tm, d), lambda i: (i, 0))
    return pl.pallas_call(
        functools.partial(_final_body, alpha=alpha),
        grid=(nblk,),
        in_specs=[row, row, pl.BlockSpec((tm, d), lambda i: (i + nblk, 0)),
                  pl.BlockSpec((tm, 1), lambda i: (i, 0)), pl.BlockSpec((tm, 1), lambda i: (i + nblk, 0)),
                  _layer_spec(g, layer), _layer_spec(b, layer),
                  pl.BlockSpec((None, tm, PLE_DIM), lambda i: (layer, i, 0)),
                  _layer_spec(wp, layer), _layer_spec(wg, layer)],
        out_specs=[row, row],
        out_shape=[jax.ShapeDtypeStruct((n, d), F32), jax.ShapeDtypeStruct((n, d), BF16)],
        compiler_params=_params("parallel"),
        name="moe_combine_ln_ple",
    )(x, y2, y2, gates, gates, g, b, p, wp, wg)


def _rot_cols(w):
    half = MLA_ROPE_DIM // 2
    return jnp.concatenate([-w[..., half:], w[..., :half]], axis=-1)


def _pad_lanes(w, width=128):
    return jnp.pad(w, [(0, 0)] * (w.ndim - 1) + [(0, width - w.shape[-1])])


def _rope_tables(s):
    half = MLA_ROPE_DIM // 2
    inv = ROPE_THETA ** (-jnp.arange(half, dtype=F32) / half)
    ang = jnp.arange(s, dtype=F32)[:, None] * inv[None, :]
    cos = jnp.concatenate([jnp.cos(ang), jnp.cos(ang)], axis=-1)
    sin = jnp.concatenate([jnp.sin(ang), jnp.sin(ang)], axis=-1)
    return _pad_lanes(cos), _pad_lanes(sin)


def kernel(x, p, w_in, rwkv_mu, rwkv_w0, rwkv_w_up, rwkv_a0, rwkv_a_up, rwkv_g_up, rwkv_k_k, rwkv_k_a, rwkv_r_k, rwkv_gn_g, rwkv_gn_b, mla_qa_g, mla_w_uq, mla_kva_g, mla_w_ukv, w_out, ln1_g, ln1_b, router_w, router_b, moe_w_gate, moe_w_up, moe_w_down, ln2_g, ln2_b, ple_w_proj, ple_w_gate):
    batch, s, d = x.shape
    depth = w_in.shape[0]
    n = batch * s
    alpha = (2 * depth) ** 0.25
    assert s % (DIL_PATTERNS[-1][1] * DIL_BLOCK) == 0 and s % 512 == 0

    cos, sin = _rope_tables(s)
    slopes = jnp.exp2(-ALIBI_MAX_BIAS * jnp.arange(1, 13, dtype=F32) / 12)
    perm = np.array([4 * g + j for j in range(EXPERTS_PER_GROUP) for g in range(N_EXPERT_GROUPS)])
    router_wt = router_w.T[perm]
    router_bt = router_b[perm].reshape(N_EXPERTS, 1)

    row = lambda t: t.reshape(depth, 1, -1)
    hpar = lambda t: t.reshape(depth, RWKV_PAIRS, 1, PAIR_LANES)
    kva0 = RWKV_IN_DIM + ZO_DIM
    w_kr = w_in[:, :, kva0 + MLA_KV_LORA:]
    w_tail = jnp.concatenate([w_in[:, :, kva0:kva0 + MLA_KV_LORA], _pad_lanes(w_kr), _pad_lanes(_rot_cols(w_kr))],
                             axis=2).astype(BF16)
    w_up_pad = jnp.pad(rwkv_w_up, ((0, 0), (0, 64), (0, 0)))
    w_up_hi = w_up_pad.astype(BF16)
    w_up_parts = jnp.stack([w_up_hi, (w_up_pad - w_up_hi.astype(F32)).astype(BF16)], axis=1)
    a_up_pad = jnp.pad(rwkv_a_up, ((0, 0), (64, 0), (0, 0))).astype(BF16)
    rwkv_params = (row(rwkv_mu), row(rwkv_w0), row(rwkv_a0), row(rwkv_k_k), row(rwkv_k_a), w_up_parts, a_up_pad,
                   rwkv_g_up.astype(BF16), hpar(rwkv_r_k), hpar(rwkv_gn_g), hpar(rwkv_gn_b))
    wq = mla_w_uq.reshape(depth, MLA_Q_LORA, MLA_HEADS, MLA_QK_DIM)
    wq_rope = wq[..., MLA_NOPE_DIM:]
    wq_all = jnp.concatenate(
        [wq[..., :MLA_NOPE_DIM].reshape(depth, MLA_Q_LORA, -1),
         jnp.concatenate([_pad_lanes(wq_rope), _pad_lanes(_rot_cols(wq_rope))], axis=-1).reshape(depth, MLA_Q_LORA, -1)],
        axis=2).astype(BF16)
    w_ukv_b, w_out_b = mla_w_ukv.astype(BF16), w_out.astype(BF16)
    ple_proj_b, ple_gate_b = ple_w_proj.astype(BF16), ple_w_gate.astype(BF16)
    qa_g, kva_g = row(mla_qa_g), row(mla_kva_g)
    ln1 = (row(ln1_g), row(ln1_b))
    ln2 = (row(ln2_g), row(ln2_b))
    p_rows = p.reshape(depth, n, PLE_DIM)

    xf = x.reshape(n, d)
    xb = xf.astype(BF16)
    for li in range(depth):
        z_r = _in_proj(xb, w_in, li, 0, RWKV_IN_DIM, F32)
        z_o = _in_proj(xb, w_in, li, RWKV_IN_DIM, ZO_DIM, BF16)
        z_t = _matmul(xb, w_tail, li, BF16, 1024, ZT_DIM)

        y_a = _rwkv_mix(z_r.reshape(batch, s, RWKV_IN_DIM), li, *rwkv_params).reshape(n, RWKV_DIM)

        prior = _dilated_group(slopes, z_o, 0, batch, s) + _dilated_group(slopes, z_o, 1, batch, s)
        y_b = _dilated_group(slopes, z_o, 2, batch, s, prior=prior)

        q_c = _mla_q(z_o, qa_g, wq_all, li, cos, sin, batch)
        k_c, v_c = _mla_kv(z_t, kva_g, w_ukv_b, li, cos, sin, batch)
        y_c = _flash(q_c, k_c, v_c).reshape(n, MLA_OUT_DIM)

        x1 = _out_ln(y_a, y_b, y_c, w_out_b, xf, *ln1, li, alpha)

        idx, gate = _router(x1, router_wt, router_bt)
        blk_e, meta, rows = _dispatch(idx)
        y2 = _experts(blk_e, meta, rows, x1, moe_w_gate, moe_w_up, moe_w_down, li)

        xf, xb = _final(x1, y2, gate.reshape(TOP_K * n, 1), *ln2, p_rows, ple_proj_b, ple_gate_b, li, alpha)
    return xf.reshape(batch, s, d)
```

```python
import functools
import math

import numpy as np
import jax
import jax.numpy as jnp
from jax import lax
from jax.experimental import pallas as pl
from jax.experimental.pallas import tpu as pltpu

F32 = jnp.float32
BF16 = jnp.bfloat16
HIGHEST = lax.Precision.HIGHEST

PLE_DIM = 256
RWKV_HEADS = 12
RWKV_HEAD_DIM = 64
RWKV_DIM = RWKV_HEADS * RWKV_HEAD_DIM
RWKV_LORA_PAD = 128
RWKV_GATE_LORA = 128
RWKV_IN_DIM = 3 * RWKV_DIM + RWKV_LORA_PAD + RWKV_GATE_LORA
RWKV_GN_EPS = 64e-5
RWKV_CHUNK = 64
DIL_PATTERNS = ((128, 1), (512, 4), (2048, 16))
DIL_GROUPS = 3
DIL_HEADS_PER_GROUP = 4
DIL_HEAD_DIM = 128
DIL_QKV_DIM = DIL_GROUPS * DIL_HEADS_PER_GROUP * DIL_HEAD_DIM
DIL_OUT_DIM = DIL_HEADS_PER_GROUP * DIL_HEAD_DIM
DIL_BLOCK = 128
ALIBI_MAX_BIAS = 8.0
MLA_HEADS = 6
MLA_NOPE_DIM = 128
MLA_ROPE_DIM = 64
MLA_V_DIM = 128
MLA_Q_LORA = 512
MLA_KV_LORA = 256
MLA_QK_DIM = MLA_NOPE_DIM + MLA_ROPE_DIM
MLA_OUT_DIM = MLA_HEADS * MLA_V_DIM
ROPE_THETA = 10000.0
N_EXPERTS = 32
N_EXPERT_GROUPS = 8
EXPERTS_PER_GROUP = 4
TOP_K = 2
D_EXPERT = 512
MOE_BLOCK = 128
LANES = 128
LN_EPS = 1e-5
RMS_EPS = 1e-6
NEG_INF = -1e30

ZO_DQ = 0
ZO_DK = DIL_QKV_DIM
ZO_DV = 2 * DIL_QKV_DIM
ZO_QA = 3 * DIL_QKV_DIM
ZO_DIM = ZO_QA + MLA_Q_LORA
ZT_KVA = 0
ZT_KR = MLA_KV_LORA
ZT_DIM = ZT_KR + 256

V7X_VMEM_LIMIT_BYTES = 48 * 1024 * 1024


def _params(*sem):
    return pltpu.CompilerParams(dimension_semantics=sem, vmem_limit_bytes=V7X_VMEM_LIMIT_BYTES)


def _sigmoid(x):
    return 1.0 / (1.0 + jnp.exp(-x))


def _mm_body(x_ref, w_ref, o_ref):
    o_ref[...] = jnp.dot(x_ref[...], w_ref[...], preferred_element_type=F32).astype(o_ref.dtype)


def _layer_spec(arr, layer):
    zeros = (0,) * (arr.ndim - 1)
    return pl.BlockSpec((None,) + arr.shape[1:], lambda *_: (layer,) + zeros)


def _matmul(x, w_all, layer, out_dtype, tm, tn):
    m, k = x.shape
    n = w_all.shape[2]
    return pl.pallas_call(
        _mm_body,
        grid=(m // tm, n // tn),
        in_specs=[pl.BlockSpec((tm, k), lambda i, j: (i, 0)),
                  pl.BlockSpec((None, k, tn), lambda i, j: (layer, 0, j))],
        out_specs=pl.BlockSpec((tm, tn), lambda i, j: (i, j)),
        out_shape=jax.ShapeDtypeStruct((m, n), out_dtype),
        compiler_params=_params("parallel", "arbitrary"),
        name="in_proj_tail",
    )(x, w_all)


def _in_proj_body(x_ref, w_ref, o_ref, wb_ref):
    @pl.when(pl.program_id(1) == 0)
    def _():
        wb_ref[...] = w_ref[0].astype(BF16)

    o_ref[...] = jnp.dot(x_ref[...], wb_ref[...], preferred_element_type=F32).astype(o_ref.dtype)


def _in_proj(x, w_all, layer, col0, ncols, out_dtype, tm=1024, tn=512):
    m, k = x.shape
    assert col0 % tn == 0 and ncols % tn == 0 and col0 + ncols <= w_all.shape[2]
    cb0 = col0 // tn
    return pl.pallas_call(
        _in_proj_body,
        grid=(ncols // tn, m // tm),
        in_specs=[pl.BlockSpec((tm, k), lambda j, i: (i, 0)),
                  pl.BlockSpec((1, k, tn), lambda j, i: (layer, 0, cb0 + j))],
        out_specs=pl.BlockSpec((tm, tn), lambda j, i: (i, j)),
        out_shape=jax.ShapeDtypeStruct((m, ncols), out_dtype),
        scratch_shapes=[pltpu.VMEM((k, tn), BF16)],
        compiler_params=_params("parallel", "arbitrary"),
        name="in_proj",
    )(x, w_all)


def _bf16_parts(x, n):
    parts = []
    for _ in range(n):
        part = x.astype(BF16)
        parts.append(part)
        x = x - part.astype(F32)
    return parts


def _bdot(a, b, dims):
    return lax.dot_general(a.astype(BF16), b.astype(BF16), (dims, ((0,), (0,))), preferred_element_type=F32)


def _bnt(a, b):
    return _bdot(a, b, ((2,), (2,)))


def _bnn(a, b):
    return _bdot(a, b, ((2,), (1,)))


def _btn(a, b):
    return _bdot(a, b, ((1,), (1,)))


RWKV_PAIRS = RWKV_HEADS // 2
PAIR_LANES = 2 * RWKV_HEAD_DIM


def _rwkv_body(z_ref, mu_ref, w0_ref, a0_ref, kk_ref, ka_ref, wup_ref, aup_ref, gup_ref, rk_ref, gng_ref, gnb_ref,
               y_ref, st_ref, carry_ref):
    @pl.when(pl.program_id(0) == 0)
    def _():
        st_ref[...] = jnp.zeros_like(st_ref)
        carry_ref[...] = jnp.zeros_like(carry_ref)

    for bi in range(z_ref.shape[0]):
        _rwkv_chunk(bi, z_ref, mu_ref, w0_ref, a0_ref, kk_ref, ka_ref, wup_ref, aup_ref, gup_ref, rk_ref, gng_ref,
                    gnb_ref, y_ref, st_ref, carry_ref)


def _rwkv_chunk(bi, z_ref, mu_ref, w0_ref, a0_ref, kk_ref, ka_ref, wup_ref, aup_ref, gup_ref, rk_ref, gng_ref, gnb_ref,
                y_ref, st_ref, carry_ref):
    t, d, np_, pl_ = RWKV_CHUNK, RWKV_DIM, RWKV_PAIRS, PAIR_LANES

    z = z_ref[bi]
    row = lax.broadcasted_iota(jnp.int32, z.shape, 0)
    zprev = jnp.where(row == 0, carry_ref[bi], pltpu.roll(z, 1, axis=0))
    carry_ref[bi] = z[t - 1:t, :]
    zs = z + (zprev - z) * mu_ref[...]
    r_w, k_w, v_w = zs[:, 0:d], zs[:, d:2 * d], zs[:, 2 * d:3 * d]
    lora = zs[:, 3 * d:3 * d + RWKV_LORA_PAD]
    gd = zs[:, 3 * d + RWKV_LORA_PAD:]
    th_hi, th_lo = _bf16_parts(jnp.tanh(lora), 2)
    u = (w0_ref[...] + jnp.dot(th_hi, wup_ref[0], preferred_element_type=F32)
         + jnp.dot(th_hi, wup_ref[1], preferred_element_type=F32)
         + jnp.dot(th_lo, wup_ref[0], preferred_element_type=F32))
    softplus = jnp.maximum(-u, 0.0) + jnp.log(1.0 + jnp.exp(-jnp.abs(u)))
    lw_w = -jnp.exp(-softplus - 0.5)
    a_w = _sigmoid(a0_ref[...] + jnp.dot(lora.astype(BF16), aup_ref[...], preferred_element_type=F32))
    g_w = jnp.dot(_sigmoid(gd).astype(BF16), gup_ref[...], preferred_element_type=F32)
    kmod_w = k_w * (1.0 + (a_w - 1.0) * ka_ref[...])
    kk_w = k_w * kk_ref[...]
    ti = lax.broadcasted_iota(jnp.int32, (t, t), 0)
    si = lax.broadcasted_iota(jnp.int32, (t, t), 1)
    incl = si <= ti
    strict = si < ti
    tri = incl.astype(BF16)
    lp_w = sum(jnp.dot(tri, part, preferred_element_type=F32) for part in _bf16_parts(lw_w, 3))

    pairs = lambda x: jnp.stack([x[:, p * pl_:(p + 1) * pl_] for p in range(np_)], axis=0)
    r, k, v, a, lw, lp, g = (pairs(x) for x in (r_w, kmod_w, v_w, a_w, lw_w, lp_w, g_w))
    li = lax.broadcasted_iota(jnp.int32, (pl_, pl_), 0)
    lj = lax.broadcasted_iota(jnp.int32, (pl_, pl_), 1)
    same_head = (li // RWKV_HEAD_DIM) == (lj // RWKV_HEAD_DIM)
    head_ones = same_head.astype(BF16)

    def head_sum(x):
        parts = _bf16_parts(x.reshape(np_ * t, pl_), 2)
        return sum(jnp.dot(part, head_ones, preferred_element_type=F32) for part in parts).reshape(np_, t, pl_)

    kk = pairs(kk_w)
    kk = kk / jnp.maximum(jnp.sqrt(head_sum(kk * kk)), 1e-12)
    lp_end = lp[:, t - 1:t, :]
    p_inv = jnp.exp(-lp)
    at = -kk * jnp.exp(lp - lw)
    bt = kk * a * p_inv
    kt = k * p_inv
    rt = r * jnp.exp(lp)
    to_end = jnp.exp(lp_end - lp)
    b_end = kk * a * to_end
    k_end = k * to_end

    lane = lax.broadcasted_iota(jnp.int32, (1, 1, pl_), 2)
    m0 = (lane < RWKV_HEAD_DIM).astype(F32)
    msk = jnp.concatenate([jnp.broadcast_to(m0, (np_, 1, pl_)), jnp.broadcast_to(1.0 - m0, (np_, 1, pl_))], axis=0)
    dup = lambda x: jnp.concatenate([x, x], axis=0)
    fold = lambda x: x[:np_] + x[np_:]
    lhs_a = dup(at) * msk
    lhs_r = dup(rt) * msk
    v2 = dup(v) * msk
    ar = jnp.concatenate([lhs_a, lhs_r], axis=1)
    x_b = _bnt(ar, dup(bt))
    x_k = _bnt(ar, dup(kt))
    a_ab = jnp.where(strict[None], x_b[:, :t], 0.0)
    a_rb = jnp.where(incl[None], x_b[:, t:], 0.0)
    a_ak = jnp.where(strict[None], x_k[:, :t], 0.0)
    a_rk = jnp.where(incl[None], x_k[:, t:], 0.0)

    sub = 16
    same_blk = ((ti // sub) == (si // sub))[None]
    eye = (ti == si).astype(F32)[None]
    ld = jnp.where(same_blk, a_ab, 0.0)
    lo = a_ab - ld
    dinv = eye + ld
    pw = ld
    for _ in range(3):
        pw = _bnn(pw, pw)
        dinv = dinv + _bnn(dinv, pw)
    n1 = _bnn(dinv, lo)
    n2 = _bnn(n1, n1)
    tinv = eye + n1 + n2 + _bnn(n1, n2)
    tinv = _bnn(tinv, dinv)

    akv = _bnn(a_ak, v2)
    wu = _bnn(tinv, jnp.concatenate([lhs_a, akv], axis=2))
    qy = _bnn(a_rb, wu)
    q = fold(lhs_r + qy[:, :, :pl_])
    y0 = fold(qy[:, :, pl_:] + _bnn(a_rk, v2))
    wt = fold(wu[:, :, :pl_])
    u0 = fold(wu[:, :, pl_:])

    s0 = st_ref[bi]
    y = _bnn(q, s0) + y0
    diag_end = jnp.where((li == lj)[None], jnp.exp(lp_end), 0.0)
    m_t = diag_end + jnp.where(same_head[None], _btn(b_end, wt), 0.0)
    c_t = jnp.where(same_head[None], _btn(b_end, u0) + _btn(k_end, v), 0.0)
    st_ref[bi] = _bnn(m_t, s0) + c_t

    inv_e = 1.0 / RWKV_HEAD_DIM
    yc = y - head_sum(y) * inv_e
    yv = head_sum(yc * yc) * inv_e
    yn = yc * lax.rsqrt(yv + RWKV_GN_EPS) * gng_ref[...] + gnb_ref[...]
    out = (yn + head_sum(r * k * rk_ref[...]) * v) * g
    for p in range(np_):
        y_ref[bi, :, p * pl_:(p + 1) * pl_] = out[p].astype(y_ref.dtype)


def _rwkv_mix(z, layer, *params):
    b, s, zin = z.shape
    d, t = RWKV_DIM, RWKV_CHUNK
    return pl.pallas_call(
        _rwkv_body,
        grid=(s // t,),
        in_specs=[pl.BlockSpec((b, t, zin), lambda c: (0, c, 0))] + [_layer_spec(a, layer) for a in params],
        out_specs=pl.BlockSpec((b, t, d), lambda c: (0, c, 0)),
        out_shape=jax.ShapeDtypeStruct((b, s, d), BF16),
        scratch_shapes=[pltpu.VMEM((b, RWKV_PAIRS, PAIR_LANES, PAIR_LANES), F32), pltpu.VMEM((b, 1, zin), F32)],
        compiler_params=_params("arbitrary"),
        name="rwkv_mix",
    )(z, *params)


def _dil_body(slopes_ref, q_ref, kc_ref, kp_ref, vc_ref, vp_ref, *rest, group, dil, n_sub, n_heads, merge):
    nblk = pl.program_id(3)
    scale = DIL_HEAD_DIM ** -0.5
    nt = (((1,), (1,)), ((), ()))
    qi = lax.broadcasted_iota(jnp.int32, (DIL_BLOCK, DIL_BLOCK), 0)
    ki = lax.broadcasted_iota(jnp.int32, (DIL_BLOCK, DIL_BLOCK), 1)
    rel_c = qi - ki
    if merge:
        o0_ref, l0_ref, o1_ref, l1_ref, y_ref = rest
    else:
        o_ref, lse_ref = rest
    for hu in range(n_heads * n_sub):
        hh, u = divmod(hu, n_sub)
        rows = slice(u * DIL_BLOCK, (u + 1) * DIL_BLOCK)
        cols = slice(hh * DIL_HEAD_DIM, (hh + 1) * DIL_HEAD_DIM)
        head = pl.program_id(2) * n_heads + hh
        bias = slopes_ref[group * DIL_HEADS_PER_GROUP + head] * float(dil)
        dist_c = bias * rel_c.astype(F32)
        dist_p = bias * (rel_c + DIL_BLOCK).astype(F32)
        q = q_ref[rows, cols]
        if u == 0:
            k_prev, v_prev = kp_ref[:, cols], vp_ref[:, cols]
            prev_lim = jnp.where(nblk == 0, -2 * DIL_BLOCK, 0)
        else:
            prows = slice((u - 1) * DIL_BLOCK, u * DIL_BLOCK)
            k_prev, v_prev = kc_ref[prows, cols], vc_ref[prows, cols]
            prev_lim = 0
        s_c = lax.dot_general(q, kc_ref[rows, cols], nt, preferred_element_type=F32) * scale
        s_p = lax.dot_general(q, k_prev, nt, preferred_element_type=F32) * scale
        s_c = jnp.where(rel_c >= 0, s_c - dist_c, NEG_INF)
        s_p = jnp.where(rel_c <= prev_lim, s_p - dist_p, NEG_INF)
        m = jnp.maximum(jnp.max(s_c, axis=-1, keepdims=True), jnp.max(s_p, axis=-1, keepdims=True))
        e_c = jnp.exp(s_c - m)
        e_p = jnp.exp(s_p - m)
        den = jnp.sum(e_c, axis=-1, keepdims=True) + jnp.sum(e_p, axis=-1, keepdims=True)
        acc = (jnp.dot(e_c.astype(BF16), vc_ref[rows, cols], preferred_element_type=F32)
               + jnp.dot(e_p.astype(BF16), v_prev, preferred_element_type=F32))
        o = acc / den
        lse = jnp.broadcast_to(m + jnp.log(den), (DIL_BLOCK, DIL_HEAD_DIM))
        if merge:
            l0, l1 = l0_ref[rows, cols], l1_ref[rows, cols]
            top = jnp.maximum(jnp.maximum(l0, l1), lse)
            w0, w1, w2 = jnp.exp(l0 - top), jnp.exp(l1 - top), jnp.exp(lse - top)
            y = (w0 * o0_ref[rows, cols] + w1 * o1_ref[rows, cols] + w2 * o) / (w0 + w1 + w2)
            y_ref[rows, cols] = y.astype(y_ref.dtype)
        else:
            o_ref[rows, cols] = o.astype(o_ref.dtype)
            lse_ref[rows, cols] = lse


def _dilated_group(slopes, zo, group, batch, s, prior=None):
    n = batch * s
    dil = DIL_PATTERNS[group][1]
    nb = s // dil // DIL_BLOCK
    n_sub = min(nb, 8)
    n_heads = min(8 // n_sub, DIL_HEADS_PER_GROUP)
    steps = nb // n_sub
    ocols = DIL_OUT_DIM // DIL_HEAD_DIM // n_heads
    rows = n_sub * DIL_BLOCK
    width = n_heads * DIL_HEAD_DIM
    gcol = lambda col0: col0 + group * DIL_OUT_DIM
    if dil == 1:
        zv, qkv_cols = zo, (gcol(ZO_DQ), gcol(ZO_DK), gcol(ZO_DV))
    else:
        zv = jnp.concatenate([zo[:, gcol(c):gcol(c) + DIL_OUT_DIM] for c in (ZO_DQ, ZO_DK, ZO_DV)], axis=1)
        zv, qkv_cols = zv.reshape(n // dil, dil * 3 * DIL_OUT_DIM), (0, DIL_OUT_DIM, 2 * DIL_OUT_DIM)
    zcols = zv.shape[1] // dil // width

    def cur(col0):
        c = col0 // width
        return pl.BlockSpec((rows, width), lambda b, r, h, i, sl: (b * steps + i, r * zcols + c + h))

    def prev(col0):
        c = col0 // width
        return pl.BlockSpec((DIL_BLOCK, width),
                            lambda b, r, h, i, sl: (jnp.maximum(b * nb + i * n_sub - 1, 0), r * zcols + c + h))

    ospec = pl.BlockSpec((rows, width), lambda b, r, h, i, sl: (b * steps + i, r * ocols + h))
    oshape = lambda dt: jax.ShapeDtypeStruct((n // dil, dil * DIL_OUT_DIM), dt)
    merge = prior is not None
    extra = [t.reshape(n // dil, dil * DIL_OUT_DIM) for t in prior] if merge else []
    qc, kc, vc = qkv_cols
    out = pl.pallas_call(
        functools.partial(_dil_body, group=group, dil=dil, n_sub=n_sub, n_heads=n_heads, merge=merge),
        grid_spec=pltpu.PrefetchScalarGridSpec(
            num_scalar_prefetch=1, grid=(batch, dil, DIL_HEADS_PER_GROUP // n_heads, steps),
            in_specs=[cur(qc), cur(kc), prev(kc), cur(vc), prev(vc)] + [ospec] * len(extra),
            out_specs=ospec if merge else [ospec, ospec]),
        out_shape=oshape(BF16) if merge else [oshape(BF16), oshape(F32)],
        compiler_params=_params("parallel", "parallel", "parallel", "arbitrary"),
        name="dilated_attn_g%d" % group,
    )(slopes, zv, zv, zv, zv, zv, *extra)
    if merge:
        return out.reshape(n, DIL_OUT_DIM)
    return [t.reshape(n, DIL_OUT_DIM) for t in out]


def _rms(x_bf16, g):
    x = x_bf16.astype(F32)
    return (x * lax.rsqrt(jnp.mean(x * x, axis=-1, keepdims=True) + RMS_EPS) * g).astype(BF16)


def _mla_q_body(x_ref, g_ref, w_ref, cos_ref, sin_ref, q_ref):
    acc = jnp.dot(_rms(x_ref[...], g_ref[...]), w_ref[...], preferred_element_type=F32)
    scale = MLA_QK_DIM ** -0.5
    cos, sin = cos_ref[...], sin_ref[...]
    nope_w = MLA_HEADS * MLA_NOPE_DIM
    for h in range(MLA_HEADS):
        q_ref[0, h, :, 0:MLA_NOPE_DIM] = (acc[:, h * 128:(h + 1) * 128] * scale).astype(BF16)
        base = nope_w + h * 256
        rope = acc[:, base:base + 128] * cos + acc[:, base + 128:base + 256] * sin
        q_ref[0, h, :, MLA_NOPE_DIM:MLA_QK_DIM] = (rope[:, 0:MLA_ROPE_DIM] * scale).astype(BF16)


def _mla_q(zo, g, w, layer, cos, sin, batch, tm=512):
    n = zo.shape[0]
    s = n // batch
    nblk = s // tm
    return pl.pallas_call(
        _mla_q_body,
        grid=(batch, nblk),
        in_specs=[pl.BlockSpec((tm, MLA_Q_LORA), lambda b, i: (b * nblk + i, ZO_QA // MLA_Q_LORA)),
                  _layer_spec(g, layer), _layer_spec(w, layer),
                  pl.BlockSpec((tm, 128), lambda b, i: (i, 0)),
                  pl.BlockSpec((tm, 128), lambda b, i: (i, 0))],
        out_specs=pl.BlockSpec((1, MLA_HEADS, tm, MLA_QK_DIM), lambda b, i: (b, 0, i, 0)),
        out_shape=jax.ShapeDtypeStruct((batch, MLA_HEADS, s, MLA_QK_DIM), BF16),
        compiler_params=_params("parallel", "parallel"),
        name="mla_q_proj",
    )(zo, g, w, cos, sin)


def _mla_kv_body(x_ref, kr_ref, g_ref, w_ref, cos_ref, sin_ref, k_ref, v_ref):
    acc = jnp.dot(_rms(x_ref[...], g_ref[...]), w_ref[...], preferred_element_type=F32)
    kr = kr_ref[...].astype(F32)
    rope = (kr[:, 0:128] * cos_ref[...] + kr[:, 128:256] * sin_ref[...])[:, 0:MLA_ROPE_DIM].astype(BF16)
    for h in range(MLA_HEADS):
        k_ref[0, h, :, 0:MLA_NOPE_DIM] = acc[:, h * 256:h * 256 + 128].astype(BF16)
        k_ref[0, h, :, MLA_NOPE_DIM:MLA_QK_DIM] = rope
        v_ref[0, h] = acc[:, h * 256 + 128:(h + 1) * 256].astype(BF16)


def _mla_kv(zo, g, w, layer, cos, sin, batch, tm=512):
    n = zo.shape[0]
    s = n // batch
    nblk = s // tm
    return pl.pallas_call(
        _mla_kv_body,
        grid=(batch, nblk),
        in_specs=[pl.BlockSpec((tm, MLA_KV_LORA), lambda b, i: (b * nblk + i, ZT_KVA // MLA_KV_LORA)),
                  pl.BlockSpec((tm, 256), lambda b, i: (b * nblk + i, ZT_KR // 256)),
                  _layer_spec(g, layer), _layer_spec(w, layer),
                  pl.BlockSpec((tm, 128), lambda b, i: (i, 0)),
                  pl.BlockSpec((tm, 128), lambda b, i: (i, 0))],
        out_specs=[pl.BlockSpec((1, MLA_HEADS, tm, MLA_QK_DIM), lambda b, i: (b, 0, i, 0)),
                   pl.BlockSpec((1, MLA_HEADS, tm, MLA_V_DIM), lambda b, i: (b, 0, i, 0))],
        out_shape=[jax.ShapeDtypeStruct((batch, MLA_HEADS, s, MLA_QK_DIM), BF16),
                   jax.ShapeDtypeStruct((batch, MLA_HEADS, s, MLA_V_DIM), BF16)],
        compiler_params=_params("parallel", "parallel"),
        name="mla_kv_proj",
    )(zo, zo, g, w, cos, sin)


def _flash_body(qi_ref, kj_ref, q_ref, k_ref, v_ref, o_ref, m_ref, l_ref, acc_ref, *, tq):
    p = pl.program_id(1)
    i = qi_ref[p]
    j = kj_ref[p]
    heads, dv = acc_ref.shape[0], acc_ref.shape[2]

    @pl.when(j == 0)
    def _():
        m_ref[...] = jnp.full_like(m_ref, NEG_INF)
        l_ref[...] = jnp.zeros_like(l_ref)
        acc_ref[...] = jnp.zeros_like(acc_ref)

    def update(masked):
        for h in range(heads):
            s = lax.dot_general(q_ref[0, h], k_ref[0, h], (((1,), (1,)), ((), ())), preferred_element_type=F32)
            if masked:
                qpos = lax.broadcasted_iota(jnp.int32, (tq, tq), 0)
                kpos = lax.broadcasted_iota(jnp.int32, (tq, tq), 1)
                s = jnp.where(kpos <= qpos, s, NEG_INF)
            m_old = m_ref[h]
            m_new = jnp.maximum(m_old, jnp.max(s, axis=-1, keepdims=True))
            alpha = jnp.exp(m_old - m_new)
            e = jnp.exp(s - jnp.concatenate([m_new] * (tq // dv), axis=1))
            l_ref[h] = alpha * l_ref[h] + jnp.sum(e, axis=-1, keepdims=True)
            acc_ref[h] = alpha * acc_ref[h] + jnp.dot(e.astype(BF16), v_ref[0, h], preferred_element_type=F32)
            m_ref[h] = m_new

    @pl.when(j < i)
    def _():
        update(False)

    @pl.when(j == i)
    def _():
        update(True)
        for h in range(heads):
            o_ref[0, :, h * dv:(h + 1) * dv] = (acc_ref[h] / l_ref[h]).astype(o_ref.dtype)


def _flash(q, k, v, tq=512):
    b, h, s, dq = q.shape
    dv = v.shape[-1]
    nq = s // tq
    pairs = [(i, j) for i in range(nq) for j in range(i + 1)]
    qi = jnp.asarray([pr[0] for pr in pairs], jnp.int32)
    kj = jnp.asarray([pr[1] for pr in pairs], jnp.int32)
    return pl.pallas_call(
        functools.partial(_flash_body, tq=tq),
        grid_spec=pltpu.PrefetchScalarGridSpec(
            num_scalar_prefetch=2, grid=(b, len(pairs)),
            in_specs=[pl.BlockSpec((1, h, tq, dq), lambda bi, p, qi, kj: (bi, 0, qi[p], 0)),
                      pl.BlockSpec((1, h, tq, dq), lambda bi, p, qi, kj: (bi, 0, kj[p], 0)),
                      pl.BlockSpec((1, h, tq, dv), lambda bi, p, qi, kj: (bi, 0, kj[p], 0))],
            out_specs=pl.BlockSpec((1, tq, h * dv), lambda bi, p, qi, kj: (bi, qi[p], 0)),
            scratch_shapes=[pltpu.VMEM((h, tq, dv), F32), pltpu.VMEM((h, tq, dv), F32), pltpu.VMEM((h, tq, dv), F32)]),
        out_shape=jax.ShapeDtypeStruct((b, s, h * dv), BF16),
        compiler_params=_params("parallel", "arbitrary"),
        name="mla_flash",
    )(qi, kj, q, k, v)


def _layernorm(h, g, b):
    mu = jnp.mean(h, axis=-1, keepdims=True)
    hc = h - mu
    var = jnp.mean(hc * hc, axis=-1, keepdims=True)
    return hc * lax.rsqrt(var + LN_EPS) * g + b


def _store_token_major(ref, val):
    tiles = val.shape[1] // LANES
    for c in range(tiles):
        ref[pl.ds(c, val.shape[0], stride=tiles), :] = val[:, c * LANES:(c + 1) * LANES]


def _load_token_major(ref, rows):
    tiles = ref.shape[0] // rows
    return jnp.concatenate([ref[pl.ds(c, rows, stride=tiles), :] for c in range(tiles)], axis=1)


def _out_ln_body(ya_ref, yb_ref, yc_ref, w_ref, x_ref, g_ref, b_ref, o_ref, otm_ref, *, alpha):
    ka, kb = ya_ref.shape[1], yb_ref.shape[1]
    acc = (jnp.dot(ya_ref[...], w_ref[0:ka, :], preferred_element_type=F32)
           + jnp.dot(yb_ref[...], w_ref[ka:ka + kb, :], preferred_element_type=F32)
           + jnp.dot(yc_ref[...], w_ref[ka + kb:, :], preferred_element_type=F32))
    x1 = _layernorm(alpha * x_ref[...] + acc, g_ref[...], b_ref[...])
    o_ref[...] = x1
    _store_token_major(otm_ref, x1)


def _out_ln(ya, yb, yc, w, x, g, b, layer, alpha, tm=256):
    n, d = x.shape
    row = lambda c: pl.BlockSpec((tm, c), lambda i: (i, 0))
    return pl.pallas_call(
        functools.partial(_out_ln_body, alpha=alpha),
        grid=(n // tm,),
        in_specs=[row(ya.shape[1]), row(yb.shape[1]), row(yc.shape[1]),
                  _layer_spec(w, layer), row(d), _layer_spec(g, layer), _layer_spec(b, layer)],
        out_specs=[row(d), pl.BlockSpec((tm * (d // LANES), LANES), lambda i: (i, 0))],
        out_shape=[jax.ShapeDtypeStruct((n, d), F32), jax.ShapeDtypeStruct((n * (d // LANES), LANES), F32)],
        compiler_params=_params("parallel"),
        name="out_proj_ln",
    )(ya, yb, yc, w, x, g, b)


def _router_body(x_ref, w_ref, b_ref, idx_ref, gate_ref):
    logits = lax.dot_general(w_ref[...], x_ref[...], (((1,), (1,)), ((), ())),
                             precision=HIGHEST, preferred_element_type=F32)
    scores = _sigmoid(logits)
    sel = scores + b_ref[...]
    ng = N_EXPERT_GROUPS
    sel_j = [sel[j * ng:(j + 1) * ng] for j in range(EXPERTS_PER_GROUP)]
    sc_j = [scores[j * ng:(j + 1) * ng] for j in range(EXPERTS_PER_GROUP)]
    grp = None
    for p in range(EXPERTS_PER_GROUP):
        for q in range(p + 1, EXPERTS_PER_GROUP):
            pair = sel_j[p] + sel_j[q]
            grp = pair if grp is None else jnp.maximum(grp, pair)
    gi = lax.broadcasted_iota(jnp.int32, grp.shape, 0)
    gmax = jnp.max(grp, axis=0, keepdims=True)
    gbest = jnp.min(jnp.where(grp == gmax, gi, ng), axis=0, keepdims=True)
    pick = gi == gbest
    mem = [jnp.sum(jnp.where(pick, t, 0.0), axis=0, keepdims=True) for t in sel_j]
    msc = [jnp.sum(jnp.where(pick, t, 0.0), axis=0, keepdims=True) for t in sc_j]

    def first_argmax(vals, exclude):
        best = jnp.full_like(vals[0], -jnp.inf)
        bi = jnp.zeros(vals[0].shape, jnp.int32)
        bs = jnp.zeros_like(vals[0])
        for j in range(EXPERTS_PER_GROUP):
            ok = vals[j] > best
            if exclude is not None:
                ok = ok & (exclude != j)
            best = jnp.where(ok, vals[j], best)
            bi = jnp.where(ok, j, bi)
            bs = jnp.where(ok, msc[j], bs)
        return bi, bs

    i1, s1 = first_argmax(mem, None)
    i2, s2 = first_argmax(mem, i1)
    tot = s1 + s2
    base = gbest * EXPERTS_PER_GROUP
    idx_ref[0:1, :] = base + i1
    idx_ref[1:2, :] = base + i2
    gate_ref[0:1, :] = s1 / tot
    gate_ref[1:2, :] = s2 / tot


def _router(x, w_t, bias, tm=512):
    n, d = x.shape
    return pl.pallas_call(
        _router_body,
        grid=(n // tm,),
        in_specs=[pl.BlockSpec((tm, d), lambda i: (i, 0)),
                  pl.BlockSpec((N_EXPERTS, d), lambda i: (0, 0)),
                  pl.BlockSpec((N_EXPERTS, 1), lambda i: (0, 0))],
        out_specs=[pl.BlockSpec((TOP_K, tm), lambda i: (0, i)), pl.BlockSpec((TOP_K, tm), lambda i: (0, i))],
        out_shape=[jax.ShapeDtypeStruct((TOP_K, n), jnp.int32), jax.ShapeDtypeStruct((TOP_K, n), F32)],
        compiler_params=_params("parallel"),
        name="moe_router",
    )(x, w_t, bias)


def _expert_body(blk_e_ref, meta_ref, rows_ref, x_hbm, wg_ref, wu_ref, wd_ref, out_hbm,
                 xg0_ref, xg1_ref, yo0_ref, yo1_ref, wgb_ref, wub_ref, wdb_ref, sem_in, sem_out, *, tok_bits, n_blocks):
    i = pl.program_id(0)
    n_act = meta_ref[0]
    xg = (xg0_ref, xg1_ref)
    yo = (yo0_ref, yo1_ref)
    tpr = wgb_ref.shape[0] // LANES

    def token_rows(ref, index):
        return ref.at[pl.ds(pl.multiple_of(index * tpr, tpr), tpr)]

    def start_gather(blk, par):
        for r in range(MOE_BLOCK):
            tok = rows_ref[blk * MOE_BLOCK + r] & ((1 << tok_bits) - 1)
            pltpu.make_async_copy(token_rows(x_hbm, tok), token_rows(xg[par], r), sem_in.at[par]).start(priority=r % 2)

    def wait_gather(par):
        for r in range(MOE_BLOCK):
            pltpu.make_async_copy(token_rows(x_hbm, 0), token_rows(xg[par], 0), sem_in.at[par]).wait()

    def start_scatter(blk, par):
        for r in range(MOE_BLOCK):
            row = rows_ref[blk * MOE_BLOCK + r] >> tok_bits
            pltpu.make_async_copy(token_rows(yo[par], r), token_rows(out_hbm, row), sem_out.at[par]).start(priority=r % 2)

    def wait_scatter(par):
        for r in range(MOE_BLOCK):
            pltpu.make_async_copy(token_rows(yo[par], 0), token_rows(out_hbm, 0), sem_out.at[par]).wait()

    def compute(par):
        xb = _load_token_major(xg[par], MOE_BLOCK).astype(BF16)
        gt = jnp.dot(xb, wgb_ref[...], preferred_element_type=F32)
        up = jnp.dot(xb, wub_ref[...], preferred_element_type=F32)
        hid = (gt * _sigmoid(gt) * up).astype(BF16)
        _store_token_major(yo[par], jnp.dot(hid, wdb_ref[...], preferred_element_type=F32))

    @pl.when((i == 0) | ((i < n_act) & (blk_e_ref[i] != blk_e_ref[jnp.maximum(i - 1, 0)])))
    def _():
        wgb_ref[...] = wg_ref[0, 0].astype(BF16)
        wub_ref[...] = wu_ref[0, 0].astype(BF16)
        wdb_ref[...] = wd_ref[0, 0].astype(BF16)

    @pl.when(i == 0)
    def _():
        start_gather(0, 0)
        yo0_ref[...] = jnp.zeros_like(yo0_ref)
        fill_rows = MOE_BLOCK * tpr
        spare0 = out_hbm.shape[0] - 2 * fill_rows
        fills = [pltpu.make_async_copy(yo0_ref, out_hbm.at[pl.ds(spare0 + h * fill_rows, fill_rows)], sem_out.at[0])
                 for h in range(2)]
        for cp in fills:
            cp.start()
        for cp in fills:
            cp.wait()
        wait_gather(0)
        start_gather(1, 1)
        compute(0)

    for par in range(2):
        @pl.when((i > 0) & (i < n_act) & ((i & 1) == par))
        def _(par=par):
            wait_gather(par)

            @pl.when(i >= 2)
            def _():
                wait_scatter(par)

            start_gather(jnp.minimum(i + 1, n_blocks - 1), 1 - par)
            start_scatter(i - 1, 1 - par)
            compute(par)

    for par in range(2):
        @pl.when((i == n_act - 1) & ((i & 1) == par))
        def _(par=par):
            start_scatter(i, par)
            wait_scatter(1 - par)
            wait_scatter(par)
            wait_gather(1 - par)


def _experts(blk_e, meta, rows, x_tm, wg, wu, wd, layer):
    d = wg.shape[2]
    tpr = d // LANES
    n = x_tm.shape[0] // tpr
    nb = blk_e.shape[0]
    buf = pltpu.VMEM((MOE_BLOCK * tpr, LANES), F32)
    wspec = lambda shp: pl.BlockSpec((1, 1) + shp, lambda i, be, nv, sr: (layer, be[i], 0, 0))
    return pl.pallas_call(
        functools.partial(_expert_body, tok_bits=_tok_bits(n), n_blocks=nb),
        grid_spec=pltpu.PrefetchScalarGridSpec(
            num_scalar_prefetch=3, grid=(nb,),
            in_specs=[pl.BlockSpec(memory_space=pl.ANY),
                      wspec((d, D_EXPERT)), wspec((d, D_EXPERT)), wspec((D_EXPERT, d))],
            out_specs=pl.BlockSpec(memory_space=pl.ANY),
            scratch_shapes=[buf, buf, buf, buf,
                            pltpu.VMEM((d, D_EXPERT), BF16), pltpu.VMEM((d, D_EXPERT), BF16),
                            pltpu.VMEM((D_EXPERT, d), BF16),
                            pltpu.SemaphoreType.DMA((2,)), pltpu.SemaphoreType.DMA((2,))]),
        out_shape=jax.ShapeDtypeStruct(((TOP_K * n + 2 * MOE_BLOCK) * tpr, LANES), F32),
        compiler_params=_params("arbitrary"),
        name="moe_experts",
    )(blk_e, meta, rows, x_tm, wg, wu, wd)


def _tok_bits(n):
    return max((n - 1).bit_length(), 1)


def _dispatch(idx):
    n = idx.shape[1]
    a_tot = TOP_K * n
    e_flat = idx.reshape(a_tot)
    onehot = (e_flat[:, None] == jnp.arange(N_EXPERTS, dtype=jnp.int32)[None, :]).astype(F32)
    chunk = 128
    oh = onehot.reshape(a_tot // chunk, chunk, N_EXPERTS)
    tri = (jnp.arange(chunk)[None, :] <= jnp.arange(chunk)[:, None]).astype(F32)
    within = jnp.einsum('ts,csn->ctn', tri, oh)
    tot = within[:, -1, :]
    csum = (within + (jnp.cumsum(tot, axis=0) - tot)[:, None, :]).reshape(a_tot, N_EXPERTS)
    rank = jnp.sum(onehot * csum, axis=1).astype(jnp.int32) - 1
    counts = csum[-1].astype(jnp.int32)
    padded = (counts + MOE_BLOCK - 1) // MOE_BLOCK * MOE_BLOCK
    pad_end = jnp.cumsum(padded)
    pad_start = pad_end - padded
    dest = pad_start[e_flat] + rank
    nb = (a_tot + N_EXPERTS * (MOE_BLOCK - 1) + MOE_BLOCK - 1) // MOE_BLOCK
    p_rows = nb * MOE_BLOCK
    bits = _tok_bits(n)
    a_ids = jnp.arange(a_tot, dtype=jnp.int32)
    pr = jnp.arange(p_rows, dtype=jnp.int32)
    spare = a_tot + ((pr // MOE_BLOCK) % 2) * MOE_BLOCK + pr % MOE_BLOCK
    rows = (spare << bits).at[dest].set((a_ids % n) | (a_ids << bits))
    blk_start = jnp.arange(nb, dtype=jnp.int32) * MOE_BLOCK
    blk_e = jnp.sum((pad_end[None, :] <= blk_start[:, None]).astype(jnp.int32), axis=1)
    blk_e = jnp.minimum(blk_e, N_EXPERTS - 1)
    n_act = jnp.minimum(pad_end[-1] // MOE_BLOCK + 1, nb).astype(jnp.int32)
    return blk_e, n_act.reshape(1), rows


def _final_body(x_ref, ya_ref, yb_ref, ga_ref, gb_ref, g_ref, b_ref, p_ref, wp_ref, wg_ref, o_ref, ob_ref, *, alpha):
    rows = x_ref.shape[0]
    ffn = _load_token_major(ya_ref, rows) * ga_ref[...] + _load_token_major(yb_ref, rows) * gb_ref[...]
    x2 = _layernorm(alpha * x_ref[...] + ffn, g_ref[...], b_ref[...])
    gate = jnp.dot(x2.astype(BF16), wg_ref[...], preferred_element_type=F32)
    proj = jnp.dot(p_ref[...].astype(BF16), wp_ref[...], preferred_element_type=F32)
    out = x2 + _sigmoid(gate) * proj
    o_ref[...] = out
    ob_ref[...] = out.astype(BF16)


def _final(x, y2, gates, g, b, p, wp, wg, layer, alpha, tm=256):
    n, d = x.shape
    nblk = n // tm
    row = pl.BlockSpec((tm, d), lambda i: (i, 0))
    return pl.pallas_call(
        functools.partial(_final_body, alpha=alpha),
        grid=(nblk,),
        in_specs=[row, pl.BlockSpec((tm * (d // LANES), LANES), lambda i: (i, 0)),
                  pl.BlockSpec((tm * (d // LANES), LANES), lambda i: (i + nblk, 0)),
                  pl.BlockSpec((tm, 1), lambda i: (i, 0)), pl.BlockSpec((tm, 1), lambda i: (i + nblk, 0)),
                  _layer_spec(g, layer), _layer_spec(b, layer),
                  pl.BlockSpec((None, tm, PLE_DIM), lambda i: (layer, i, 0)),
                  _layer_spec(wp, layer), _layer_spec(wg, layer)],
        out_specs=[row, row],
        out_shape=[jax.ShapeDtypeStruct((n, d), F32), jax.ShapeDtypeStruct((n, d), BF16)],
        compiler_params=_params("parallel"),
        name="moe_combine_ln_ple",
    )(x, y2, y2, gates, gates, g, b, p, wp, wg)


def _rot_cols(w):
    half = MLA_ROPE_DIM // 2
    return jnp.concatenate([-w[..., half:], w[..., :half]], axis=-1)


def _pad_lanes(w, width=128):
    return jnp.pad(w, [(0, 0)] * (w.ndim - 1) + [(0, width - w.shape[-1])])


def _rope_tables(s):
    half = MLA_ROPE_DIM // 2
    inv = ROPE_THETA ** (-jnp.arange(half, dtype=F32) / half)
    ang = jnp.arange(s, dtype=F32)[:, None] * inv[None, :]
    cos = jnp.concatenate([jnp.cos(ang), jnp.cos(ang)], axis=-1)
    sin = jnp.concatenate([jnp.sin(ang), jnp.sin(ang)], axis=-1)
    return _pad_lanes(cos), _pad_lanes(sin)


def kernel(x, p, w_in, rwkv_mu, rwkv_w0, rwkv_w_up, rwkv_a0, rwkv_a_up, rwkv_g_up, rwkv_k_k, rwkv_k_a, rwkv_r_k, rwkv_gn_g, rwkv_gn_b, mla_qa_g, mla_w_uq, mla_kva_g, mla_w_ukv, w_out, ln1_g, ln1_b, router_w, router_b, moe_w_gate, moe_w_up, moe_w_down, ln2_g, ln2_b, ple_w_proj, ple_w_gate):
    batch, s, d = x.shape
    depth = w_in.shape[0]
    n = batch * s
    alpha = (2 * depth) ** 0.25
    assert s % (DIL_PATTERNS[-1][1] * DIL_BLOCK) == 0 and s % 512 == 0

    cos, sin = _rope_tables(s)
    slopes = jnp.exp2(-ALIBI_MAX_BIAS * jnp.arange(1, 13, dtype=F32) / 12)
    perm = np.array([4 * g + j for j in range(EXPERTS_PER_GROUP) for g in range(N_EXPERT_GROUPS)])
    router_wt = router_w.T[perm]
    router_bt = router_b[perm].reshape(N_EXPERTS, 1)

    row = lambda t: t.reshape(depth, 1, -1)
    hpar = lambda t: t.reshape(depth, RWKV_PAIRS, 1, PAIR_LANES)
    kva0 = RWKV_IN_DIM + ZO_DIM
    w_kr = w_in[:, :, kva0 + MLA_KV_LORA:]
    w_tail = jnp.concatenate([w_in[:, :, kva0:kva0 + MLA_KV_LORA], _pad_lanes(w_kr), _pad_lanes(_rot_cols(w_kr))],
                             axis=2).astype(BF16)
    w_up_pad = jnp.pad(rwkv_w_up, ((0, 0), (0, 64), (0, 0)))
    w_up_hi = w_up_pad.astype(BF16)
    w_up_parts = jnp.stack([w_up_hi, (w_up_pad - w_up_hi.astype(F32)).astype(BF16)], axis=1)
    a_up_pad = jnp.pad(rwkv_a_up, ((0, 0), (64, 0), (0, 0))).astype(BF16)
    rwkv_params = (row(rwkv_mu), row(rwkv_w0), row(rwkv_a0), row(rwkv_k_k), row(rwkv_k_a), w_up_parts, a_up_pad,
                   rwkv_g_up.astype(BF16), hpar(rwkv_r_k), hpar(rwkv_gn_g), hpar(rwkv_gn_b))
    wq = mla_w_uq.reshape(depth, MLA_Q_LORA, MLA_HEADS, MLA_QK_DIM)
    wq_rope = wq[..., MLA_NOPE_DIM:]
    wq_all = jnp.concatenate(
        [wq[..., :MLA_NOPE_DIM].reshape(depth, MLA_Q_LORA, -1),
         jnp.concatenate([_pad_lanes(wq_rope), _pad_lanes(_rot_cols(wq_rope))], axis=-1).reshape(depth, MLA_Q_LORA, -1)],
        axis=2).astype(BF16)
    w_ukv_b, w_out_b = mla_w_ukv.astype(BF16), w_out.astype(BF16)
    ple_proj_b, ple_gate_b = ple_w_proj.astype(BF16), ple_w_gate.astype(BF16)
    qa_g, kva_g = row(mla_qa_g), row(mla_kva_g)
    ln1 = (row(ln1_g), row(ln1_b))
    ln2 = (row(ln2_g), row(ln2_b))
    p_rows = p.reshape(depth, n, PLE_DIM)

    xf = x.reshape(n, d)
    xb = xf.astype(BF16)
    for li in range(depth):
        z_r = _in_proj(xb, w_in, li, 0, RWKV_IN_DIM, F32)
        z_o = _in_proj(xb, w_in, li, RWKV_IN_DIM, ZO_DIM, BF16)
        z_t = _matmul(xb, w_tail, li, BF16, 1024, ZT_DIM)

        y_a = _rwkv_mix(z_r.reshape(batch, s, RWKV_IN_DIM), li, *rwkv_params).reshape(n, RWKV_DIM)

        prior = _dilated_group(slopes, z_o, 0, batch, s) + _dilated_group(slopes, z_o, 1, batch, s)
        y_b = _dilated_group(slopes, z_o, 2, batch, s, prior=prior)

        q_c = _mla_q(z_o, qa_g, wq_all, li, cos, sin, batch)
        k_c, v_c = _mla_kv(z_t, kva_g, w_ukv_b, li, cos, sin, batch)
        y_c = _flash(q_c, k_c, v_c).reshape(n, MLA_OUT_DIM)

        x1, x1_tm = _out_ln(y_a, y_b, y_c, w_out_b, xf, *ln1, li, alpha)

        idx, gate = _router(x1, router_wt, router_bt)
        blk_e, meta, rows = _dispatch(idx)
        y2 = _experts(blk_e, meta, rows, x1_tm, moe_w_gate, moe_w_up, moe_w_down, li)

        xf, xb = _final(x1, y2, gate.reshape(TOP_K * n, 1), *ln2, p_rows, ple_proj_b, ple_gate_b, li, alpha)
    return xf.reshape(batch, s, d)
```

```python
import functools
import math

import numpy as np
import jax
import jax.numpy as jnp
from jax import lax
from jax.experimental import pallas as pl
from jax.experimental.pallas import tpu as pltpu

F32 = jnp.float32
BF16 = jnp.bfloat16
HIGHEST = lax.Precision.HIGHEST

PLE_DIM = 256
RWKV_HEADS = 12
RWKV_HEAD_DIM = 64
RWKV_DIM = RWKV_HEADS * RWKV_HEAD_DIM
RWKV_LORA_PAD = 128
RWKV_GATE_LORA = 128
RWKV_IN_DIM = 3 * RWKV_DIM + RWKV_LORA_PAD + RWKV_GATE_LORA
RWKV_GN_EPS = 64e-5
RWKV_CHUNK = 64
DIL_PATTERNS = ((128, 1), (512, 4), (2048, 16))
DIL_GROUPS = 3
DIL_HEADS_PER_GROUP = 4
DIL_HEAD_DIM = 128
DIL_QKV_DIM = DIL_GROUPS * DIL_HEADS_PER_GROUP * DIL_HEAD_DIM
DIL_OUT_DIM = DIL_HEADS_PER_GROUP * DIL_HEAD_DIM
DIL_BLOCK = 128
ALIBI_MAX_BIAS = 8.0
MLA_HEADS = 6
MLA_NOPE_DIM = 128
MLA_ROPE_DIM = 64
MLA_V_DIM = 128
MLA_Q_LORA = 512
MLA_KV_LORA = 256
MLA_QK_DIM = MLA_NOPE_DIM + MLA_ROPE_DIM
MLA_OUT_DIM = MLA_HEADS * MLA_V_DIM
ROPE_THETA = 10000.0
N_EXPERTS = 32
N_EXPERT_GROUPS = 8
EXPERTS_PER_GROUP = 4
TOP_K = 2
D_EXPERT = 512
MOE_BLOCK = 128
LANES = 128
LN_EPS = 1e-5
RMS_EPS = 1e-6
NEG_INF = -1e30

ZO_DQ = 0
ZO_DK = DIL_QKV_DIM
ZO_DV = 2 * DIL_QKV_DIM
ZO_QA = 3 * DIL_QKV_DIM
ZO_DIM = ZO_QA + MLA_Q_LORA
ZT_KVA = 0
ZT_KR = MLA_KV_LORA
ZT_DIM = ZT_KR + 256

V7X_VMEM_LIMIT_BYTES = 48 * 1024 * 1024


def _params(*sem):
    return pltpu.CompilerParams(dimension_semantics=sem, vmem_limit_bytes=V7X_VMEM_LIMIT_BYTES)


def _sigmoid(x):
    return 1.0 / (1.0 + jnp.exp(-x))


def _mm_body(x_ref, w_ref, o_ref):
    o_ref[...] = jnp.dot(x_ref[...], w_ref[...], preferred_element_type=F32).astype(o_ref.dtype)


def _layer_spec(arr, layer):
    zeros = (0,) * (arr.ndim - 1)
    return pl.BlockSpec((None,) + arr.shape[1:], lambda *_: (layer,) + zeros)


def _matmul(x, w_all, layer, out_dtype, tm, tn):
    m, k = x.shape
    n = w_all.shape[2]
    return pl.pallas_call(
        _mm_body,
        grid=(m // tm, n // tn),
        in_specs=[pl.BlockSpec((tm, k), lambda i, j: (i, 0)),
                  pl.BlockSpec((None, k, tn), lambda i, j: (layer, 0, j))],
        out_specs=pl.BlockSpec((tm, tn), lambda i, j: (i, j)),
        out_shape=jax.ShapeDtypeStruct((m, n), out_dtype),
        compiler_params=_params("parallel", "arbitrary"),
        name="in_proj_tail",
    )(x, w_all)


def _in_proj_body(x_ref, w_ref, o_ref, wb_ref):
    @pl.when(pl.program_id(1) == 0)
    def _():
        wb_ref[...] = w_ref[0].astype(BF16)

    o_ref[...] = jnp.dot(x_ref[...], wb_ref[...], preferred_element_type=F32).astype(o_ref.dtype)


def _in_proj(x, w_all, layer, col0, ncols, out_dtype, tm=1024, tn=512):
    m, k = x.shape
    assert col0 % tn == 0 and ncols % tn == 0 and col0 + ncols <= w_all.shape[2]
    cb0 = col0 // tn
    return pl.pallas_call(
        _in_proj_body,
        grid=(ncols // tn, m // tm),
        in_specs=[pl.BlockSpec((tm, k), lambda j, i: (i, 0)),
                  pl.BlockSpec((1, k, tn), lambda j, i: (layer, 0, cb0 + j))],
        out_specs=pl.BlockSpec((tm, tn), lambda j, i: (i, j)),
        out_shape=jax.ShapeDtypeStruct((m, ncols), out_dtype),
        scratch_shapes=[pltpu.VMEM((k, tn), BF16)],
        compiler_params=_params("parallel", "arbitrary"),
        name="in_proj",
    )(x, w_all)


def _bf16_parts(x, n):
    parts = []
    for _ in range(n):
        part = x.astype(BF16)
        parts.append(part)
        x = x - part.astype(F32)
    return parts


def _bdot(a, b, dims):
    return lax.dot_general(a.astype(BF16), b.astype(BF16), (dims, ((0,), (0,))), preferred_element_type=F32)


def _bnt(a, b):
    return _bdot(a, b, ((2,), (2,)))


def _bnn(a, b):
    return _bdot(a, b, ((2,), (1,)))


def _btn(a, b):
    return _bdot(a, b, ((1,), (1,)))


RWKV_PAIRS = RWKV_HEADS // 2
PAIR_LANES = 2 * RWKV_HEAD_DIM


def _rwkv_body(z_ref, mu_ref, w0_ref, a0_ref, kk_ref, ka_ref, wup_ref, aup_ref, gup_ref, rk_ref, gng_ref, gnb_ref,
               y_ref, st_ref, carry_ref):
    @pl.when(pl.program_id(0) == 0)
    def _():
        st_ref[...] = jnp.zeros_like(st_ref)
        carry_ref[...] = jnp.zeros_like(carry_ref)

    for bi in range(z_ref.shape[0]):
        _rwkv_chunk(bi, z_ref, mu_ref, w0_ref, a0_ref, kk_ref, ka_ref, wup_ref, aup_ref, gup_ref, rk_ref, gng_ref,
                    gnb_ref, y_ref, st_ref, carry_ref)


def _rwkv_chunk(bi, z_ref, mu_ref, w0_ref, a0_ref, kk_ref, ka_ref, wup_ref, aup_ref, gup_ref, rk_ref, gng_ref, gnb_ref,
                y_ref, st_ref, carry_ref):
    t, d, np_, pl_ = RWKV_CHUNK, RWKV_DIM, RWKV_PAIRS, PAIR_LANES

    z = z_ref[bi]
    row = lax.broadcasted_iota(jnp.int32, z.shape, 0)
    zprev = jnp.where(row == 0, carry_ref[bi], pltpu.roll(z, 1, axis=0))
    carry_ref[bi] = z[t - 1:t, :]
    zs = z + (zprev - z) * mu_ref[...]
    r_w, k_w, v_w = zs[:, 0:d], zs[:, d:2 * d], zs[:, 2 * d:3 * d]
    lora = zs[:, 3 * d:3 * d + RWKV_LORA_PAD]
    gd = zs[:, 3 * d + RWKV_LORA_PAD:]
    th_hi, th_lo = _bf16_parts(jnp.tanh(lora), 2)
    u = (w0_ref[...] + jnp.dot(th_hi, wup_ref[0], preferred_element_type=F32)
         + jnp.dot(th_hi, wup_ref[1], preferred_element_type=F32)
         + jnp.dot(th_lo, wup_ref[0], preferred_element_type=F32))
    softplus = jnp.maximum(-u, 0.0) + jnp.log(1.0 + jnp.exp(-jnp.abs(u)))
    lw_w = -jnp.exp(-softplus - 0.5)
    a_w = _sigmoid(a0_ref[...] + jnp.dot(lora.astype(BF16), aup_ref[...], preferred_element_type=F32))
    g_w = jnp.dot(_sigmoid(gd).astype(BF16), gup_ref[...], preferred_element_type=F32)
    kmod_w = k_w * (1.0 + (a_w - 1.0) * ka_ref[...])
    kk_w = k_w * kk_ref[...]
    ti = lax.broadcasted_iota(jnp.int32, (t, t), 0)
    si = lax.broadcasted_iota(jnp.int32, (t, t), 1)
    incl = si <= ti
    strict = si < ti
    tri = incl.astype(BF16)
    lp_w = sum(jnp.dot(tri, part, preferred_element_type=F32) for part in _bf16_parts(lw_w, 3))

    pairs = lambda x: jnp.stack([x[:, p * pl_:(p + 1) * pl_] for p in range(np_)], axis=0)
    r, k, v, a, lw, lp, g = (pairs(x) for x in (r_w, kmod_w, v_w, a_w, lw_w, lp_w, g_w))
    li = lax.broadcasted_iota(jnp.int32, (pl_, pl_), 0)
    lj = lax.broadcasted_iota(jnp.int32, (pl_, pl_), 1)
    same_head = (li // RWKV_HEAD_DIM) == (lj // RWKV_HEAD_DIM)
    head_ones = same_head.astype(BF16)

    def head_sum(x):
        parts = _bf16_parts(x.reshape(np_ * t, pl_), 2)
        return sum(jnp.dot(part, head_ones, preferred_element_type=F32) for part in parts).reshape(np_, t, pl_)

    kk = pairs(kk_w)
    kk = kk / jnp.maximum(jnp.sqrt(head_sum(kk * kk)), 1e-12)
    lp_end = lp[:, t - 1:t, :]
    p_inv = jnp.exp(-lp)
    at = -kk * jnp.exp(lp - lw)
    bt = kk * a * p_inv
    kt = k * p_inv
    rt = r * jnp.exp(lp)
    to_end = jnp.exp(lp_end - lp)
    b_end = kk * a * to_end
    k_end = k * to_end

    lane = lax.broadcasted_iota(jnp.int32, (1, 1, pl_), 2)
    m0 = (lane < RWKV_HEAD_DIM).astype(F32)
    msk = jnp.concatenate([jnp.broadcast_to(m0, (np_, 1, pl_)), jnp.broadcast_to(1.0 - m0, (np_, 1, pl_))], axis=0)
    dup = lambda x: jnp.concatenate([x, x], axis=0)
    fold = lambda x: x[:np_] + x[np_:]
    lhs_a = dup(at) * msk
    lhs_r = dup(rt) * msk
    v2 = dup(v) * msk
    ar = jnp.concatenate([lhs_a, lhs_r], axis=1)
    x_b = _bnt(ar, dup(bt))
    x_k = _bnt(ar, dup(kt))
    a_ab = jnp.where(strict[None], x_b[:, :t], 0.0)
    a_rb = jnp.where(incl[None], x_b[:, t:], 0.0)
    a_ak = jnp.where(strict[None], x_k[:, :t], 0.0)
    a_rk = jnp.where(incl[None], x_k[:, t:], 0.0)

    sub = 16
    same_blk = ((ti // sub) == (si // sub))[None]
    eye = (ti == si).astype(F32)[None]
    ld = jnp.where(same_blk, a_ab, 0.0)
    lo = a_ab - ld
    dinv = eye + ld
    pw = ld
    for _ in range(3):
        pw = _bnn(pw, pw)
        dinv = dinv + _bnn(dinv, pw)
    n1 = _bnn(dinv, lo)
    n2 = _bnn(n1, n1)
    tinv = eye + n1 + n2 + _bnn(n1, n2)
    tinv = _bnn(tinv, dinv)

    akv = _bnn(a_ak, v2)
    wu = _bnn(tinv, jnp.concatenate([lhs_a, akv], axis=2))
    qy = _bnn(a_rb, wu)
    q = fold(lhs_r + qy[:, :, :pl_])
    y0 = fold(qy[:, :, pl_:] + _bnn(a_rk, v2))
    wt = fold(wu[:, :, :pl_])
    u0 = fold(wu[:, :, pl_:])

    s0 = st_ref[bi]
    y = _bnn(q, s0) + y0
    diag_end = jnp.where((li == lj)[None], jnp.exp(lp_end), 0.0)
    m_t = diag_end + jnp.where(same_head[None], _btn(b_end, wt), 0.0)
    c_t = jnp.where(same_head[None], _btn(b_end, u0) + _btn(k_end, v), 0.0)
    st_ref[bi] = _bnn(m_t, s0) + c_t

    inv_e = 1.0 / RWKV_HEAD_DIM
    yc = y - head_sum(y) * inv_e
    yv = head_sum(yc * yc) * inv_e
    yn = yc * lax.rsqrt(yv + RWKV_GN_EPS) * gng_ref[...] + gnb_ref[...]
    out = (yn + head_sum(r * k * rk_ref[...]) * v) * g
    for p in range(np_):
        y_ref[bi, :, p * pl_:(p + 1) * pl_] = out[p].astype(y_ref.dtype)


def _rwkv_mix(z, layer, *params):
    b, s, zin = z.shape
    d, t = RWKV_DIM, RWKV_CHUNK
    return pl.pallas_call(
        _rwkv_body,
        grid=(s // t,),
        in_specs=[pl.BlockSpec((b, t, zin), lambda c: (0, c, 0))] + [_layer_spec(a, layer) for a in params],
        out_specs=pl.BlockSpec((b, t, d), lambda c: (0, c, 0)),
        out_shape=jax.ShapeDtypeStruct((b, s, d), BF16),
        scratch_shapes=[pltpu.VMEM((b, RWKV_PAIRS, PAIR_LANES, PAIR_LANES), F32), pltpu.VMEM((b, 1, zin), F32)],
        compiler_params=_params("arbitrary"),
        name="rwkv_mix",
    )(z, *params)


def _dil_body(slopes_ref, q_ref, kc_ref, kp_ref, vc_ref, vp_ref, *rest, group, dil, n_sub, n_heads, merge):
    nblk = pl.program_id(3)
    scale = DIL_HEAD_DIM ** -0.5
    nt = (((1,), (1,)), ((), ()))
    qi = lax.broadcasted_iota(jnp.int32, (DIL_BLOCK, DIL_BLOCK), 0)
    ki = lax.broadcasted_iota(jnp.int32, (DIL_BLOCK, DIL_BLOCK), 1)
    rel_c = qi - ki
    if merge:
        o0_ref, l0_ref, o1_ref, l1_ref, y_ref = rest
    else:
        o_ref, lse_ref = rest
    for hu in range(n_heads * n_sub):
        hh, u = divmod(hu, n_sub)
        rows = slice(u * DIL_BLOCK, (u + 1) * DIL_BLOCK)
        cols = slice(hh * DIL_HEAD_DIM, (hh + 1) * DIL_HEAD_DIM)
        head = pl.program_id(2) * n_heads + hh
        bias = slopes_ref[group * DIL_HEADS_PER_GROUP + head] * float(dil)
        dist_c = bias * rel_c.astype(F32)
        dist_p = bias * (rel_c + DIL_BLOCK).astype(F32)
        q = q_ref[rows, cols]
        if u == 0:
            k_prev, v_prev = kp_ref[:, cols], vp_ref[:, cols]
            prev_lim = jnp.where(nblk == 0, -2 * DIL_BLOCK, 0)
        else:
            prows = slice((u - 1) * DIL_BLOCK, u * DIL_BLOCK)
            k_prev, v_prev = kc_ref[prows, cols], vc_ref[prows, cols]
            prev_lim = 0
        s_c = lax.dot_general(q, kc_ref[rows, cols], nt, preferred_element_type=F32) * scale
        s_p = lax.dot_general(q, k_prev, nt, preferred_element_type=F32) * scale
        s_c = jnp.where(rel_c >= 0, s_c - dist_c, NEG_INF)
        s_p = jnp.where(rel_c <= prev_lim, s_p - dist_p, NEG_INF)
        m = jnp.maximum(jnp.max(s_c, axis=-1, keepdims=True), jnp.max(s_p, axis=-1, keepdims=True))
        e_c = jnp.exp(s_c - m)
        e_p = jnp.exp(s_p - m)
        den = jnp.sum(e_c, axis=-1, keepdims=True) + jnp.sum(e_p, axis=-1, keepdims=True)
        acc = (jnp.dot(e_c.astype(BF16), vc_ref[rows, cols], preferred_element_type=F32)
               + jnp.dot(e_p.astype(BF16), v_prev, preferred_element_type=F32))
        o = acc / den
        lse = jnp.broadcast_to(m + jnp.log(den), (DIL_BLOCK, DIL_HEAD_DIM))
        if merge:
            l0, l1 = l0_ref[rows, cols], l1_ref[rows, cols]
            top = jnp.maximum(jnp.maximum(l0, l1), lse)
            w0, w1, w2 = jnp.exp(l0 - top), jnp.exp(l1 - top), jnp.exp(lse - top)
            y = (w0 * o0_ref[rows, cols] + w1 * o1_ref[rows, cols] + w2 * o) / (w0 + w1 + w2)
            y_ref[rows, cols] = y.astype(y_ref.dtype)
        else:
            o_ref[rows, cols] = o.astype(o_ref.dtype)
            lse_ref[rows, cols] = lse


def _dilated_group(slopes, zo, group, batch, s, prior=None):
    n = batch * s
    dil = DIL_PATTERNS[group][1]
    nb = s // dil // DIL_BLOCK
    n_sub = min(nb, 8)
    n_heads = min(8 // n_sub, DIL_HEADS_PER_GROUP)
    steps = nb // n_sub
    ocols = DIL_OUT_DIM // DIL_HEAD_DIM // n_heads
    rows = n_sub * DIL_BLOCK
    width = n_heads * DIL_HEAD_DIM
    gcol = lambda col0: col0 + group * DIL_OUT_DIM
    if dil == 1:
        zv, qkv_cols = zo, (gcol(ZO_DQ), gcol(ZO_DK), gcol(ZO_DV))
    else:
        zv = jnp.concatenate([zo[:, gcol(c):gcol(c) + DIL_OUT_DIM] for c in (ZO_DQ, ZO_DK, ZO_DV)], axis=1)
        zv, qkv_cols = zv.reshape(n // dil, dil * 3 * DIL_OUT_DIM), (0, DIL_OUT_DIM, 2 * DIL_OUT_DIM)
    zcols = zv.shape[1] // dil // width

    def cur(col0):
        c = col0 // width
        return pl.BlockSpec((rows, width), lambda b, r, h, i, sl: (b * steps + i, r * zcols + c + h))

    def prev(col0):
        c = col0 // width
        return pl.BlockSpec((DIL_BLOCK, width),
                            lambda b, r, h, i, sl: (jnp.maximum(b * nb + i * n_sub - 1, 0), r * zcols + c + h))

    ospec = pl.BlockSpec((rows, width), lambda b, r, h, i, sl: (b * steps + i, r * ocols + h))
    oshape = lambda dt: jax.ShapeDtypeStruct((n // dil, dil * DIL_OUT_DIM), dt)
    merge = prior is not None
    extra = [t.reshape(n // dil, dil * DIL_OUT_DIM) for t in prior] if merge else []
    qc, kc, vc = qkv_cols
    out = pl.pallas_call(
        functools.partial(_dil_body, group=group, dil=dil, n_sub=n_sub, n_heads=n_heads, merge=merge),
        grid_spec=pltpu.PrefetchScalarGridSpec(
            num_scalar_prefetch=1, grid=(batch, dil, DIL_HEADS_PER_GROUP // n_heads, steps),
            in_specs=[cur(qc), cur(kc), prev(kc), cur(vc), prev(vc)] + [ospec] * len(extra),
            out_specs=ospec if merge else [ospec, ospec]),
        out_shape=oshape(BF16) if merge else [oshape(BF16), oshape(F32)],
        compiler_params=_params("parallel", "parallel", "parallel", "arbitrary"),
        name="dilated_attn_g%d" % group,
    )(slopes, zv, zv, zv, zv, zv, *extra)
    if merge:
        return out.reshape(n, DIL_OUT_DIM)
    return [t.reshape(n, DIL_OUT_DIM) for t in out]


def _rms(x_bf16, g):
    x = x_bf16.astype(F32)
    return (x * lax.rsqrt(jnp.mean(x * x, axis=-1, keepdims=True) + RMS_EPS) * g).astype(BF16)


def _mla_q_body(x_ref, g_ref, w_ref, cos_ref, sin_ref, q_ref):
    acc = jnp.dot(_rms(x_ref[...], g_ref[...]), w_ref[...], preferred_element_type=F32)
    scale = MLA_QK_DIM ** -0.5
    cos, sin = cos_ref[...], sin_ref[...]
    nope_w = MLA_HEADS * MLA_NOPE_DIM
    for h in range(MLA_HEADS):
        q_ref[0, h, :, 0:MLA_NOPE_DIM] = (acc[:, h * 128:(h + 1) * 128] * scale).astype(BF16)
        base = nope_w + h * 256
        rope = acc[:, base:base + 128] * cos + acc[:, base + 128:base + 256] * sin
        q_ref[0, h, :, MLA_NOPE_DIM:MLA_QK_DIM] = (rope[:, 0:MLA_ROPE_DIM] * scale).astype(BF16)


def _mla_q(zo, g, w, layer, cos, sin, batch, tm=512):
    n = zo.shape[0]
    s = n // batch
    nblk = s // tm
    return pl.pallas_call(
        _mla_q_body,
        grid=(batch, nblk),
        in_specs=[pl.BlockSpec((tm, MLA_Q_LORA), lambda b, i: (b * nblk + i, ZO_QA // MLA_Q_LORA)),
                  _layer_spec(g, layer), _layer_spec(w, layer),
                  pl.BlockSpec((tm, 128), lambda b, i: (i, 0)),
                  pl.BlockSpec((tm, 128), lambda b, i: (i, 0))],
        out_specs=pl.BlockSpec((1, MLA_HEADS, tm, MLA_QK_DIM), lambda b, i: (b, 0, i, 0)),
        out_shape=jax.ShapeDtypeStruct((batch, MLA_HEADS, s, MLA_QK_DIM), BF16),
        compiler_params=_params("parallel", "parallel"),
        name="mla_q_proj",
    )(zo, g, w, cos, sin)


def _mla_kv_body(x_ref, kr_ref, g_ref, w_ref, cos_ref, sin_ref, k_ref, v_ref):
    acc = jnp.dot(_rms(x_ref[...], g_ref[...]), w_ref[...], preferred_element_type=F32)
    kr = kr_ref[...].astype(F32)
    rope = (kr[:, 0:128] * cos_ref[...] + kr[:, 128:256] * sin_ref[...])[:, 0:MLA_ROPE_DIM].astype(BF16)
    for h in range(MLA_HEADS):
        k_ref[0, h, :, 0:MLA_NOPE_DIM] = acc[:, h * 256:h * 256 + 128].astype(BF16)
        k_ref[0, h, :, MLA_NOPE_DIM:MLA_QK_DIM] = rope
        v_ref[0, h] = acc[:, h * 256 + 128:(h + 1) * 256].astype(BF16)


def _mla_kv(zo, g, w, layer, cos, sin, batch, tm=512):
    n = zo.shape[0]
    s = n // batch
    nblk = s // tm
    return pl.pallas_call(
        _mla_kv_body,
        grid=(batch, nblk),
        in_specs=[pl.BlockSpec((tm, MLA_KV_LORA), lambda b, i: (b * nblk + i, ZT_KVA // MLA_KV_LORA)),
                  pl.BlockSpec((tm, 256), lambda b, i: (b * nblk + i, ZT_KR // 256)),
                  _layer_spec(g, layer), _layer_spec(w, layer),
                  pl.BlockSpec((tm, 128), lambda b, i: (i, 0)),
                  pl.BlockSpec((tm, 128), lambda b, i: (i, 0))],
        out_specs=[pl.BlockSpec((1, MLA_HEADS, tm, MLA_QK_DIM), lambda b, i: (b, 0, i, 0)),
                   pl.BlockSpec((1, MLA_HEADS, tm, MLA_V_DIM), lambda b, i: (b, 0, i, 0))],
        out_shape=[jax.ShapeDtypeStruct((batch, MLA_HEADS, s, MLA_QK_DIM), BF16),
                   jax.ShapeDtypeStruct((batch, MLA_HEADS, s, MLA_V_DIM), BF16)],
        compiler_params=_params("parallel", "parallel"),
        name="mla_kv_proj",
    )(zo, zo, g, w, cos, sin)


def _flash_body(qi_ref, kj_ref, q_ref, k_ref, v_ref, o_ref, m_ref, l_ref, acc_ref, *, tq):
    p = pl.program_id(1)
    i = qi_ref[p]
    j = kj_ref[p]
    heads, dv = acc_ref.shape[0], acc_ref.shape[2]

    @pl.when(j == 0)
    def _():
        m_ref[...] = jnp.full_like(m_ref, NEG_INF)
        l_ref[...] = jnp.zeros_like(l_ref)
        acc_ref[...] = jnp.zeros_like(acc_ref)

    def update(masked):
        for h in range(heads):
            s = lax.dot_general(q_ref[0, h], k_ref[0, h], (((1,), (1,)), ((), ())), preferred_element_type=F32)
            if masked:
                qpos = lax.broadcasted_iota(jnp.int32, (tq, tq), 0)
                kpos = lax.broadcasted_iota(jnp.int32, (tq, tq), 1)
                s = jnp.where(kpos <= qpos, s, NEG_INF)
            m_old = m_ref[h]
            m_new = jnp.maximum(m_old, jnp.max(s, axis=-1, keepdims=True))
            alpha = jnp.exp(m_old - m_new)
            e = jnp.exp(s - jnp.concatenate([m_new] * (tq // dv), axis=1))
            l_ref[h] = alpha * l_ref[h] + jnp.sum(e, axis=-1, keepdims=True)
            acc_ref[h] = alpha * acc_ref[h] + jnp.dot(e.astype(BF16), v_ref[0, h], preferred_element_type=F32)
            m_ref[h] = m_new

    @pl.when(j < i)
    def _():
        update(False)

    @pl.when(j == i)
    def _():
        update(True)
        for h in range(heads):
            o_ref[0, :, h * dv:(h + 1) * dv] = (acc_ref[h] / l_ref[h]).astype(o_ref.dtype)


def _flash(q, k, v, tq=512):
    b, h, s, dq = q.shape
    dv = v.shape[-1]
    nq = s // tq
    pairs = [(i, j) for i in range(nq) for j in range(i + 1)]
    qi = jnp.asarray([pr[0] for pr in pairs], jnp.int32)
    kj = jnp.asarray([pr[1] for pr in pairs], jnp.int32)
    return pl.pallas_call(
        functools.partial(_flash_body, tq=tq),
        grid_spec=pltpu.PrefetchScalarGridSpec(
            num_scalar_prefetch=2, grid=(b, len(pairs)),
            in_specs=[pl.BlockSpec((1, h, tq, dq), lambda bi, p, qi, kj: (bi, 0, qi[p], 0)),
                      pl.BlockSpec((1, h, tq, dq), lambda bi, p, qi, kj: (bi, 0, kj[p], 0)),
                      pl.BlockSpec((1, h, tq, dv), lambda bi, p, qi, kj: (bi, 0, kj[p], 0))],
            out_specs=pl.BlockSpec((1, tq, h * dv), lambda bi, p, qi, kj: (bi, qi[p], 0)),
            scratch_shapes=[pltpu.VMEM((h, tq, dv), F32), pltpu.VMEM((h, tq, dv), F32), pltpu.VMEM((h, tq, dv), F32)]),
        out_shape=jax.ShapeDtypeStruct((b, s, h * dv), BF16),
        compiler_params=_params("parallel", "arbitrary"),
        name="mla_flash",
    )(qi, kj, q, k, v)


def _layernorm(h, g, b):
    mu = jnp.mean(h, axis=-1, keepdims=True)
    hc = h - mu
    var = jnp.mean(hc * hc, axis=-1, keepdims=True)
    return hc * lax.rsqrt(var + LN_EPS) * g + b


def _store_token_major(ref, val):
    tiles = val.shape[1] // LANES
    for c in range(tiles):
        ref[pl.ds(c, val.shape[0], stride=tiles), :] = val[:, c * LANES:(c + 1) * LANES]


def _load_token_major(ref, rows):
    tiles = ref.shape[0] // rows
    return jnp.concatenate([ref[pl.ds(c, rows, stride=tiles), :] for c in range(tiles)], axis=1)


def _out_ln_body(ya_ref, yb_ref, yc_ref, w_ref, x_ref, g_ref, b_ref, o_ref, otm_ref, *, alpha):
    ka, kb = ya_ref.shape[1], yb_ref.shape[1]
    acc = (jnp.dot(ya_ref[...], w_ref[0:ka, :], preferred_element_type=F32)
           + jnp.dot(yb_ref[...], w_ref[ka:ka + kb, :], preferred_element_type=F32)
           + jnp.dot(yc_ref[...], w_ref[ka + kb:, :], preferred_element_type=F32))
    x1 = _layernorm(alpha * x_ref[...] + acc, g_ref[...], b_ref[...])
    o_ref[...] = x1
    _store_token_major(otm_ref, x1)


def _out_ln(ya, yb, yc, w, x, g, b, layer, alpha, tm=256):
    n, d = x.shape
    row = lambda c: pl.BlockSpec((tm, c), lambda i: (i, 0))
    return pl.pallas_call(
        functools.partial(_out_ln_body, alpha=alpha),
        grid=(n // tm,),
        in_specs=[row(ya.shape[1]), row(yb.shape[1]), row(yc.shape[1]),
                  _layer_spec(w, layer), row(d), _layer_spec(g, layer), _layer_spec(b, layer)],
        out_specs=[row(d), pl.BlockSpec((tm * (d // LANES), LANES), lambda i: (i, 0))],
        out_shape=[jax.ShapeDtypeStruct((n, d), F32), jax.ShapeDtypeStruct((n * (d // LANES), LANES), F32)],
        compiler_params=_params("parallel"),
        name="out_proj_ln",
    )(ya, yb, yc, w, x, g, b)


def _router_body(x_ref, w_ref, b_ref, idx_ref, gate_ref):
    logits = lax.dot_general(w_ref[...], x_ref[...], (((1,), (1,)), ((), ())),
                             precision=HIGHEST, preferred_element_type=F32)
    scores = _sigmoid(logits)
    sel = scores + b_ref[...]
    ng = N_EXPERT_GROUPS
    sel_j = [sel[j * ng:(j + 1) * ng] for j in range(EXPERTS_PER_GROUP)]
    sc_j = [scores[j * ng:(j + 1) * ng] for j in range(EXPERTS_PER_GROUP)]
    grp = None
    for p in range(EXPERTS_PER_GROUP):
        for q in range(p + 1, EXPERTS_PER_GROUP):
            pair = sel_j[p] + sel_j[q]
            grp = pair if grp is None else jnp.maximum(grp, pair)
    gi = lax.broadcasted_iota(jnp.int32, grp.shape, 0)
    gmax = jnp.max(grp, axis=0, keepdims=True)
    gbest = jnp.min(jnp.where(grp == gmax, gi, ng), axis=0, keepdims=True)
    pick = gi == gbest
    mem = [jnp.sum(jnp.where(pick, t, 0.0), axis=0, keepdims=True) for t in sel_j]
    msc = [jnp.sum(jnp.where(pick, t, 0.0), axis=0, keepdims=True) for t in sc_j]

    def first_argmax(vals, exclude):
        best = jnp.full_like(vals[0], -jnp.inf)
        bi = jnp.zeros(vals[0].shape, jnp.int32)
        bs = jnp.zeros_like(vals[0])
        for j in range(EXPERTS_PER_GROUP):
            ok = vals[j] > best
            if exclude is not None:
                ok = ok & (exclude != j)
            best = jnp.where(ok, vals[j], best)
            bi = jnp.where(ok, j, bi)
            bs = jnp.where(ok, msc[j], bs)
        return bi, bs

    i1, s1 = first_argmax(mem, None)
    i2, s2 = first_argmax(mem, i1)
    tot = s1 + s2
    base = gbest * EXPERTS_PER_GROUP
    idx_ref[0:1, :] = base + i1
    idx_ref[1:2, :] = base + i2
    gate_ref[0:1, :] = s1 / tot
    gate_ref[1:2, :] = s2 / tot


def _router(x, w_t, bias, tm=512):
    n, d = x.shape
    return pl.pallas_call(
        _router_body,
        grid=(n // tm,),
        in_specs=[pl.BlockSpec((tm, d), lambda i: (i, 0)),
                  pl.BlockSpec((N_EXPERTS, d), lambda i: (0, 0)),
                  pl.BlockSpec((N_EXPERTS, 1), lambda i: (0, 0))],
        out_specs=[pl.BlockSpec((TOP_K, tm), lambda i: (0, i)), pl.BlockSpec((TOP_K, tm), lambda i: (0, i))],
        out_shape=[jax.ShapeDtypeStruct((TOP_K, n), jnp.int32), jax.ShapeDtypeStruct((TOP_K, n), F32)],
        compiler_params=_params("parallel"),
        name="moe_router",
    )(x, w_t, bias)


def _expert_body(blk_e_ref, meta_ref, rows_ref, x_hbm, wg_ref, wu_ref, wd_ref, out_hbm,
                 xg0_ref, xg1_ref, xg2_ref, yo0_ref, yo1_ref, yo2_ref, wgb_ref, wub_ref, wdb_ref, sem_in, sem_out,
                 *, tok_bits, n_blocks):
    i = pl.program_id(0)
    n_act = meta_ref[0]
    xg = (xg0_ref, xg1_ref, xg2_ref)
    yo = (yo0_ref, yo1_ref, yo2_ref)
    last = n_blocks - 1
    tpr = wgb_ref.shape[0] // LANES

    def token_rows(ref, index):
        return ref.at[pl.ds(pl.multiple_of(index * tpr, tpr), tpr)]

    def start_gather(blk, par):
        for r in range(MOE_BLOCK):
            tok = rows_ref[blk * MOE_BLOCK + r] & ((1 << tok_bits) - 1)
            pltpu.make_async_copy(token_rows(x_hbm, tok), token_rows(xg[par], r), sem_in.at[par]).start(priority=r % 2)

    def wait_gather(par):
        for r in range(MOE_BLOCK):
            pltpu.make_async_copy(token_rows(x_hbm, 0), token_rows(xg[par], 0), sem_in.at[par]).wait()

    def start_scatter(blk, par):
        for r in range(MOE_BLOCK):
            row = rows_ref[blk * MOE_BLOCK + r] >> tok_bits
            pltpu.make_async_copy(token_rows(yo[par], r), token_rows(out_hbm, row), sem_out.at[par]).start(priority=r % 2)

    def wait_scatter(par):
        for r in range(MOE_BLOCK):
            pltpu.make_async_copy(token_rows(yo[par], 0), token_rows(out_hbm, 0), sem_out.at[par]).wait()

    def compute(par):
        xb = _load_token_major(xg[par], MOE_BLOCK).astype(BF16)
        gt = jnp.dot(xb, wgb_ref[...], preferred_element_type=F32)
        up = jnp.dot(xb, wub_ref[...], preferred_element_type=F32)
        hid = (gt * _sigmoid(gt) * up).astype(BF16)
        _store_token_major(yo[par], jnp.dot(hid, wdb_ref[...], preferred_element_type=F32))

    @pl.when((i == 0) | ((i < n_act) & (blk_e_ref[i] != blk_e_ref[jnp.maximum(i - 1, 0)])))
    def _():
        wgb_ref[...] = wg_ref[0, 0].astype(BF16)
        wub_ref[...] = wu_ref[0, 0].astype(BF16)
        wdb_ref[...] = wd_ref[0, 0].astype(BF16)

    @pl.when(i == 0)
    def _():
        start_gather(0, 0)
        start_gather(1, 1)
        yo0_ref[...] = jnp.zeros_like(yo0_ref)
        fill_rows = MOE_BLOCK * tpr
        spare0 = out_hbm.shape[0] - 2 * fill_rows
        fills = [pltpu.make_async_copy(yo0_ref, out_hbm.at[pl.ds(spare0 + h * fill_rows, fill_rows)], sem_out.at[0])
                 for h in range(2)]
        for cp in fills:
            cp.start()
        for cp in fills:
            cp.wait()
        wait_gather(0)
        start_gather(2, 2)
        compute(0)

    res = lax.rem(i, 3)
    for q in range(3):
        prv, nxt = (q + 2) % 3, (q + 1) % 3

        @pl.when((i > 0) & (i < n_act) & (res == q))
        def _(q=q, prv=prv):
            wait_gather(q)

            @pl.when(i >= 3)
            def _():
                wait_scatter(q)

            start_gather(jnp.minimum(i + 2, last), prv)
            start_scatter(i - 1, prv)
            compute(q)

    for q in range(3):
        prv, nxt = (q + 2) % 3, (q + 1) % 3

        @pl.when((i == n_act - 1) & (res == q))
        def _(q=q, prv=prv, nxt=nxt):
            start_scatter(i, q)

            @pl.when(i >= 2)
            def _():
                wait_scatter(nxt)

            wait_scatter(prv)
            wait_scatter(q)
            wait_gather(nxt)
            wait_gather(prv)


def _experts(blk_e, meta, rows, x_tm, wg, wu, wd, layer):
    d = wg.shape[2]
    tpr = d // LANES
    n = x_tm.shape[0] // tpr
    nb = blk_e.shape[0]
    buf = pltpu.VMEM((MOE_BLOCK * tpr, LANES), F32)
    wspec = lambda shp: pl.BlockSpec((1, 1) + shp, lambda i, be, nv, sr: (layer, be[i], 0, 0))
    return pl.pallas_call(
        functools.partial(_expert_body, tok_bits=_tok_bits(n), n_blocks=nb),
        grid_spec=pltpu.PrefetchScalarGridSpec(
            num_scalar_prefetch=3, grid=(nb,),
            in_specs=[pl.BlockSpec(memory_space=pl.ANY),
                      wspec((d, D_EXPERT)), wspec((d, D_EXPERT)), wspec((D_EXPERT, d))],
            out_specs=pl.BlockSpec(memory_space=pl.ANY),
            scratch_shapes=[buf] * 6 + [
                pltpu.VMEM((d, D_EXPERT), BF16), pltpu.VMEM((d, D_EXPERT), BF16), pltpu.VMEM((D_EXPERT, d), BF16),
                pltpu.SemaphoreType.DMA((3,)), pltpu.SemaphoreType.DMA((3,))]),
        out_shape=jax.ShapeDtypeStruct(((TOP_K * n + 2 * MOE_BLOCK) * tpr, LANES), F32),
        compiler_params=_params("arbitrary"),
        name="moe_experts",
    )(blk_e, meta, rows, x_tm, wg, wu, wd)


def _tok_bits(n):
    return max((n - 1).bit_length(), 1)


def _dispatch(idx):
    n = idx.shape[1]
    a_tot = TOP_K * n
    e_flat = idx.reshape(a_tot)
    onehot = (e_flat[:, None] == jnp.arange(N_EXPERTS, dtype=jnp.int32)[None, :]).astype(F32)
    chunk = 128
    oh = onehot.reshape(a_tot // chunk, chunk, N_EXPERTS)
    tri = (jnp.arange(chunk)[None, :] <= jnp.arange(chunk)[:, None]).astype(F32)
    within = jnp.einsum('ts,csn->ctn', tri, oh)
    tot = within[:, -1, :]
    csum = (within + (jnp.cumsum(tot, axis=0) - tot)[:, None, :]).reshape(a_tot, N_EXPERTS)
    rank = jnp.sum(onehot * csum, axis=1).astype(jnp.int32) - 1
    counts = csum[-1].astype(jnp.int32)
    padded = (counts + MOE_BLOCK - 1) // MOE_BLOCK * MOE_BLOCK
    pad_end = jnp.cumsum(padded)
    pad_start = pad_end - padded
    dest = pad_start[e_flat] + rank
    nb = (a_tot + N_EXPERTS * (MOE_BLOCK - 1) + MOE_BLOCK - 1) // MOE_BLOCK
    p_rows = nb * MOE_BLOCK
    bits = _tok_bits(n)
    a_ids = jnp.arange(a_tot, dtype=jnp.int32)
    pr = jnp.arange(p_rows, dtype=jnp.int32)
    spare = a_tot + ((pr // MOE_BLOCK) % 2) * MOE_BLOCK + pr % MOE_BLOCK
    rows = (spare << bits).at[dest].set((a_ids % n) | (a_ids << bits))
    blk_start = jnp.arange(nb, dtype=jnp.int32) * MOE_BLOCK
    blk_e = jnp.sum((pad_end[None, :] <= blk_start[:, None]).astype(jnp.int32), axis=1)
    blk_e = jnp.minimum(blk_e, N_EXPERTS - 1)
    n_act = jnp.minimum(pad_end[-1] // MOE_BLOCK + 1, nb).astype(jnp.int32)
    return blk_e, n_act.reshape(1), rows


def _final_body(x_ref, ya_ref, yb_ref, ga_ref, gb_ref, g_ref, b_ref, p_ref, wp_ref, wg_ref, o_ref, ob_ref, *, alpha):
    rows = x_ref.shape[0]
    ffn = _load_token_major(ya_ref, rows) * ga_ref[...] + _load_token_major(yb_ref, rows) * gb_ref[...]
    x2 = _layernorm(alpha * x_ref[...] + ffn, g_ref[...], b_ref[...])
    gate = jnp.dot(x2.astype(BF16), wg_ref[...], preferred_element_type=F32)
    proj = jnp.dot(p_ref[...].astype(BF16), wp_ref[...], preferred_element_type=F32)
    out = x2 + _sigmoid(gate) * proj
    o_ref[...] = out
    ob_ref[...] = out.astype(BF16)


def _final(x, y2, gates, g, b, p, wp, wg, layer, alpha, tm=256):
    n, d = x.shape
    nblk = n // tm
    row = pl.BlockSpec((tm, d), lambda i: (i, 0))
    return pl.pallas_call(
        functools.partial(_final_body, alpha=alpha),
        grid=(nblk,),
        in_specs=[row, pl.BlockSpec((tm * (d // LANES), LANES), lambda i: (i, 0)),
                  pl.BlockSpec((tm * (d // LANES), LANES), lambda i: (i + nblk, 0)),
                  pl.BlockSpec((tm, 1), lambda i: (i, 0)), pl.BlockSpec((tm, 1), lambda i: (i + nblk, 0)),
                  _layer_spec(g, layer), _layer_spec(b, layer),
                  pl.BlockSpec((None, tm, PLE_DIM), lambda i: (layer, i, 0)),
                  _layer_spec(wp, layer), _layer_spec(wg, layer)],
        out_specs=[row, row],
        out_shape=[jax.ShapeDtypeStruct((n, d), F32), jax.ShapeDtypeStruct((n, d), BF16)],
        compiler_params=_params("parallel"),
        name="moe_combine_ln_ple",
    )(x, y2, y2, gates, gates, g, b, p, wp, wg)


def _rot_cols(w):
    half = MLA_ROPE_DIM // 2
    return jnp.concatenate([-w[..., half:], w[..., :half]], axis=-1)


def _pad_lanes(w, width=128):
    return jnp.pad(w, [(0, 0)] * (w.ndim - 1) + [(0, width - w.shape[-1])])


def _rope_tables(s):
    half = MLA_ROPE_DIM // 2
    inv = ROPE_THETA ** (-jnp.arange(half, dtype=F32) / half)
    ang = jnp.arange(s, dtype=F32)[:, None] * inv[None, :]
    cos = jnp.concatenate([jnp.cos(ang), jnp.cos(ang)], axis=-1)
    sin = jnp.concatenate([jnp.sin(ang), jnp.sin(ang)], axis=-1)
    return _pad_lanes(cos), _pad_lanes(sin)


def kernel(x, p, w_in, rwkv_mu, rwkv_w0, rwkv_w_up, rwkv_a0, rwkv_a_up, rwkv_g_up, rwkv_k_k, rwkv_k_a, rwkv_r_k, rwkv_gn_g, rwkv_gn_b, mla_qa_g, mla_w_uq, mla_kva_g, mla_w_ukv, w_out, ln1_g, ln1_b, router_w, router_b, moe_w_gate, moe_w_up, moe_w_down, ln2_g, ln2_b, ple_w_proj, ple_w_gate):
    batch, s, d = x.shape
    depth = w_in.shape[0]
    n = batch * s
    alpha = (2 * depth) ** 0.25
    assert s % (DIL_PATTERNS[-1][1] * DIL_BLOCK) == 0 and s % 512 == 0

    cos, sin = _rope_tables(s)
    slopes = jnp.exp2(-ALIBI_MAX_BIAS * jnp.arange(1, 13, dtype=F32) / 12)
    perm = np.array([4 * g + j for j in range(EXPERTS_PER_GROUP) for g in range(N_EXPERT_GROUPS)])
    router_wt = router_w.T[perm]
    router_bt = router_b[perm].reshape(N_EXPERTS, 1)

    row = lambda t: t.reshape(depth, 1, -1)
    hpar = lambda t: t.reshape(depth, RWKV_PAIRS, 1, PAIR_LANES)
    kva0 = RWKV_IN_DIM + ZO_DIM
    w_kr = w_in[:, :, kva0 + MLA_KV_LORA:]
    w_tail = jnp.concatenate([w_in[:, :, kva0:kva0 + MLA_KV_LORA], _pad_lanes(w_kr), _pad_lanes(_rot_cols(w_kr))],
                             axis=2).astype(BF16)
    w_up_pad = jnp.pad(rwkv_w_up, ((0, 0), (0, 64), (0, 0)))
    w_up_hi = w_up_pad.astype(BF16)
    w_up_parts = jnp.stack([w_up_hi, (w_up_pad - w_up_hi.astype(F32)).astype(BF16)], axis=1)
    a_up_pad = jnp.pad(rwkv_a_up, ((0, 0), (64, 0), (0, 0))).astype(BF16)
    rwkv_params = (row(rwkv_mu), row(rwkv_w0), row(rwkv_a0), row(rwkv_k_k), row(rwkv_k_a), w_up_parts, a_up_pad,
                   rwkv_g_up.astype(BF16), hpar(rwkv_r_k), hpar(rwkv_gn_g), hpar(rwkv_gn_b))
    wq = mla_w_uq.reshape(depth, MLA_Q_LORA, MLA_HEADS, MLA_QK_DIM)
    wq_rope = wq[..., MLA_NOPE_DIM:]
    wq_all = jnp.concatenate(
        [wq[..., :MLA_NOPE_DIM].reshape(depth, MLA_Q_LORA, -1),
         jnp.concatenate([_pad_lanes(wq_rope), _pad_lanes(_rot_cols(wq_rope))], axis=-1).reshape(depth, MLA_Q_LORA, -1)],
        axis=2).astype(BF16)
    w_ukv_b, w_out_b = mla_w_ukv.astype(BF16), w_out.astype(BF16)
    ple_proj_b, ple_gate_b = ple_w_proj.astype(BF16), ple_w_gate.astype(BF16)
    qa_g, kva_g = row(mla_qa_g), row(mla_kva_g)
    ln1 = (row(ln1_g), row(ln1_b))
    ln2 = (row(ln2_g), row(ln2_b))
    p_rows = p.reshape(depth, n, PLE_DIM)

    xf = x.reshape(n, d)
    xb = xf.astype(BF16)
    for li in range(depth):
        z_r = _in_proj(xb, w_in, li, 0, RWKV_IN_DIM, F32)
        z_o = _in_proj(xb, w_in, li, RWKV_IN_DIM, ZO_DIM, BF16)
        z_t = _matmul(xb, w_tail, li, BF16, 1024, ZT_DIM)

        y_a = _rwkv_mix(z_r.reshape(batch, s, RWKV_IN_DIM), li, *rwkv_params).reshape(n, RWKV_DIM)

        prior = _dilated_group(slopes, z_o, 0, batch, s) + _dilated_group(slopes, z_o, 1, batch, s)
        y_b = _dilated_group(slopes, z_o, 2, batch, s, prior=prior)

        q_c = _mla_q(z_o, qa_g, wq_all, li, cos, sin, batch)
        k_c, v_c = _mla_kv(z_t, kva_g, w_ukv_b, li, cos, sin, batch)
        y_c = _flash(q_c, k_c, v_c).reshape(n, MLA_OUT_DIM)

        x1, x1_tm = _out_ln(y_a, y_b, y_c, w_out_b, xf, *ln1, li, alpha)

        idx, gate = _router(x1, router_wt, router_bt)
        blk_e, meta, rows = _dispatch(idx)
        y2 = _experts(blk_e, meta, rows, x1_tm, moe_w_gate, moe_w_up, moe_w_down, li)

        xf, xb = _final(x1, y2, gate.reshape(TOP_K * n, 1), *ln2, p_rows, ple_proj_b, ple_gate_b, li, alpha)
    return xf.reshape(batch, s, d)
```

```python
import functools
import math

import numpy as np
import jax
import jax.numpy as jnp
from jax import lax
from jax.experimental import pallas as pl
from jax.experimental.pallas import tpu as pltpu

F32 = jnp.float32
BF16 = jnp.bfloat16
HIGHEST = lax.Precision.HIGHEST

PLE_DIM = 256
RWKV_HEADS = 12
RWKV_HEAD_DIM = 64
RWKV_DIM = RWKV_HEADS * RWKV_HEAD_DIM
RWKV_LORA_PAD = 128
RWKV_GATE_LORA = 128
RWKV_IN_DIM = 3 * RWKV_DIM + RWKV_LORA_PAD + RWKV_GATE_LORA
RWKV_GN_EPS = 64e-5
RWKV_CHUNK = 64
DIL_PATTERNS = ((128, 1), (512, 4), (2048, 16))
DIL_GROUPS = 3
DIL_HEADS_PER_GROUP = 4
DIL_HEAD_DIM = 128
DIL_QKV_DIM = DIL_GROUPS * DIL_HEADS_PER_GROUP * DIL_HEAD_DIM
DIL_OUT_DIM = DIL_HEADS_PER_GROUP * DIL_HEAD_DIM
DIL_BLOCK = 128
ALIBI_MAX_BIAS = 8.0
MLA_HEADS = 6
MLA_NOPE_DIM = 128
MLA_ROPE_DIM = 64
MLA_V_DIM = 128
MLA_Q_LORA = 512
MLA_KV_LORA = 256
MLA_QK_DIM = MLA_NOPE_DIM + MLA_ROPE_DIM
MLA_OUT_DIM = MLA_HEADS * MLA_V_DIM
ROPE_THETA = 10000.0
N_EXPERTS = 32
N_EXPERT_GROUPS = 8
EXPERTS_PER_GROUP = 4
TOP_K = 2
D_EXPERT = 512
MOE_BLOCK = 128
LANES = 128
LN_EPS = 1e-5
RMS_EPS = 1e-6
NEG_INF = -1e30

ZO_DQ = 0
ZO_DK = DIL_QKV_DIM
ZO_DV = 2 * DIL_QKV_DIM
ZO_QA = 3 * DIL_QKV_DIM
ZO_DIM = ZO_QA + MLA_Q_LORA
ZT_KVA = 0
ZT_KR = MLA_KV_LORA
ZT_DIM = ZT_KR + 256

V7X_VMEM_LIMIT_BYTES = 48 * 1024 * 1024


def _params(*sem):
    return pltpu.CompilerParams(dimension_semantics=sem, vmem_limit_bytes=V7X_VMEM_LIMIT_BYTES)


def _sigmoid(x):
    return 1.0 / (1.0 + jnp.exp(-x))


def _mm_body(x_ref, w_ref, o_ref):
    o_ref[...] = jnp.dot(x_ref[...], w_ref[...], preferred_element_type=F32).astype(o_ref.dtype)


def _layer_spec(arr, layer):
    zeros = (0,) * (arr.ndim - 1)
    return pl.BlockSpec((None,) + arr.shape[1:], lambda *_: (layer,) + zeros)


def _matmul(x, w_all, layer, out_dtype, tm, tn):
    m, k = x.shape
    n = w_all.shape[2]
    return pl.pallas_call(
        _mm_body,
        grid=(m // tm, n // tn),
        in_specs=[pl.BlockSpec((tm, k), lambda i, j: (i, 0)),
                  pl.BlockSpec((None, k, tn), lambda i, j: (layer, 0, j))],
        out_specs=pl.BlockSpec((tm, tn), lambda i, j: (i, j)),
        out_shape=jax.ShapeDtypeStruct((m, n), out_dtype),
        compiler_params=_params("parallel", "arbitrary"),
        name="in_proj_tail",
    )(x, w_all)


def _in_proj_body(x_ref, w_ref, o_ref, wb_ref):
    @pl.when(pl.program_id(1) == 0)
    def _():
        wb_ref[...] = w_ref[0].astype(BF16)

    o_ref[...] = jnp.dot(x_ref[...], wb_ref[...], preferred_element_type=F32).astype(o_ref.dtype)


def _in_proj(x, w_all, layer, col0, ncols, out_dtype, tm=1024, tn=512):
    m, k = x.shape
    assert col0 % tn == 0 and ncols % tn == 0 and col0 + ncols <= w_all.shape[2]
    cb0 = col0 // tn
    return pl.pallas_call(
        _in_proj_body,
        grid=(ncols // tn, m // tm),
        in_specs=[pl.BlockSpec((tm, k), lambda j, i: (i, 0)),
                  pl.BlockSpec((1, k, tn), lambda j, i: (layer, 0, cb0 + j))],
        out_specs=pl.BlockSpec((tm, tn), lambda j, i: (i, j)),
        out_shape=jax.ShapeDtypeStruct((m, ncols), out_dtype),
        scratch_shapes=[pltpu.VMEM((k, tn), BF16)],
        compiler_params=_params("parallel", "arbitrary"),
        name="in_proj",
    )(x, w_all)


def _bf16_parts(x, n):
    parts = []
    for _ in range(n):
        part = x.astype(BF16)
        parts.append(part)
        x = x - part.astype(F32)
    return parts


def _bdot(a, b, dims):
    return lax.dot_general(a.astype(BF16), b.astype(BF16), (dims, ((0,), (0,))), preferred_element_type=F32)


def _bnt(a, b):
    return _bdot(a, b, ((2,), (2,)))


def _bnn(a, b):
    return _bdot(a, b, ((2,), (1,)))


def _btn(a, b):
    return _bdot(a, b, ((1,), (1,)))


RWKV_PAIRS = RWKV_HEADS // 2
PAIR_LANES = 2 * RWKV_HEAD_DIM


def _rwkv_body(z_ref, mu_ref, w0_ref, a0_ref, kk_ref, ka_ref, wup_ref, aup_ref, gup_ref, rk_ref, gng_ref, gnb_ref,
               y_ref, st_ref, carry_ref):
    @pl.when(pl.program_id(0) == 0)
    def _():
        st_ref[...] = jnp.zeros_like(st_ref)
        carry_ref[...] = jnp.zeros_like(carry_ref)

    for bi in range(z_ref.shape[0]):
        _rwkv_chunk(bi, z_ref, mu_ref, w0_ref, a0_ref, kk_ref, ka_ref, wup_ref, aup_ref, gup_ref, rk_ref, gng_ref,
                    gnb_ref, y_ref, st_ref, carry_ref)


def _rwkv_chunk(bi, z_ref, mu_ref, w0_ref, a0_ref, kk_ref, ka_ref, wup_ref, aup_ref, gup_ref, rk_ref, gng_ref, gnb_ref,
                y_ref, st_ref, carry_ref):
    t, d, np_, pl_ = RWKV_CHUNK, RWKV_DIM, RWKV_PAIRS, PAIR_LANES

    z = z_ref[bi]
    row = lax.broadcasted_iota(jnp.int32, z.shape, 0)
    zprev = jnp.where(row == 0, carry_ref[bi], pltpu.roll(z, 1, axis=0))
    carry_ref[bi] = z[t - 1:t, :]
    zs = z + (zprev - z) * mu_ref[...]
    r_w, k_w, v_w = zs[:, 0:d], zs[:, d:2 * d], zs[:, 2 * d:3 * d]
    lora = zs[:, 3 * d:3 * d + RWKV_LORA_PAD]
    gd = zs[:, 3 * d + RWKV_LORA_PAD:]
    th_hi, th_lo = _bf16_parts(jnp.tanh(lora), 2)
    u = (w0_ref[...] + jnp.dot(th_hi, wup_ref[0], preferred_element_type=F32)
         + jnp.dot(th_hi, wup_ref[1], preferred_element_type=F32)
         + jnp.dot(th_lo, wup_ref[0], preferred_element_type=F32))
    softplus = jnp.maximum(-u, 0.0) + jnp.log(1.0 + jnp.exp(-jnp.abs(u)))
    lw_w = -jnp.exp(-softplus - 0.5)
    a_w = _sigmoid(a0_ref[...] + jnp.dot(lora.astype(BF16), aup_ref[...], preferred_element_type=F32))
    g_w = jnp.dot(_sigmoid(gd).astype(BF16), gup_ref[...], preferred_element_type=F32)
    kmod_w = k_w * (1.0 + (a_w - 1.0) * ka_ref[...])
    kk_w = k_w * kk_ref[...]
    ti = lax.broadcasted_iota(jnp.int32, (t, t), 0)
    si = lax.broadcasted_iota(jnp.int32, (t, t), 1)
    incl = si <= ti
    strict = si < ti
    tri = incl.astype(BF16)
    lp_w = sum(jnp.dot(tri, part, preferred_element_type=F32) for part in _bf16_parts(lw_w, 3))

    pairs = lambda x: jnp.stack([x[:, p * pl_:(p + 1) * pl_] for p in range(np_)], axis=0)
    r, k, v, a, lw, lp, g = (pairs(x) for x in (r_w, kmod_w, v_w, a_w, lw_w, lp_w, g_w))
    li = lax.broadcasted_iota(jnp.int32, (pl_, pl_), 0)
    lj = lax.broadcasted_iota(jnp.int32, (pl_, pl_), 1)
    same_head = (li // RWKV_HEAD_DIM) == (lj // RWKV_HEAD_DIM)
    head_ones = same_head.astype(BF16)

    def head_sum(x):
        parts = _bf16_parts(x.reshape(np_ * t, pl_), 2)
        return sum(jnp.dot(part, head_ones, preferred_element_type=F32) for part in parts).reshape(np_, t, pl_)

    kk = pairs(kk_w)
    kk = kk / jnp.maximum(jnp.sqrt(head_sum(kk * kk)), 1e-12)
    lp_end = lp[:, t - 1:t, :]
    p_inv = jnp.exp(-lp)
    at = -kk * jnp.exp(lp - lw)
    bt = kk * a * p_inv
    kt = k * p_inv
    rt = r * jnp.exp(lp)
    to_end = jnp.exp(lp_end - lp)
    b_end = kk * a * to_end
    k_end = k * to_end

    lane = lax.broadcasted_iota(jnp.int32, (1, 1, pl_), 2)
    m0 = (lane < RWKV_HEAD_DIM).astype(F32)
    msk = jnp.concatenate([jnp.broadcast_to(m0, (np_, 1, pl_)), jnp.broadcast_to(1.0 - m0, (np_, 1, pl_))], axis=0)
    dup = lambda x: jnp.concatenate([x, x], axis=0)
    fold = lambda x: x[:np_] + x[np_:]
    lhs_a = dup(at) * msk
    lhs_r = dup(rt) * msk
    v2 = dup(v) * msk
    ar = jnp.concatenate([lhs_a, lhs_r], axis=1)
    x_b = _bnt(ar, dup(bt))
    x_k = _bnt(ar, dup(kt))
    a_ab = jnp.where(strict[None], x_b[:, :t], 0.0)
    a_rb = jnp.where(incl[None], x_b[:, t:], 0.0)
    a_ak = jnp.where(strict[None], x_k[:, :t], 0.0)
    a_rk = jnp.where(incl[None], x_k[:, t:], 0.0)

    sub = 16
    same_blk = ((ti // sub) == (si // sub))[None]
    eye = (ti == si).astype(F32)[None]
    ld = jnp.where(same_blk, a_ab, 0.0)
    lo = a_ab - ld
    dinv = eye + ld
    pw = ld
    for _ in range(3):
        pw = _bnn(pw, pw)
        dinv = dinv + _bnn(dinv, pw)
    n1 = _bnn(dinv, lo)
    n2 = _bnn(n1, n1)
    tinv = eye + n1 + n2 + _bnn(n1, n2)
    tinv = _bnn(tinv, dinv)

    akv = _bnn(a_ak, v2)
    wu = _bnn(tinv, jnp.concatenate([lhs_a, akv], axis=2))
    qy = _bnn(a_rb, wu)
    q = fold(lhs_r + qy[:, :, :pl_])
    y0 = fold(qy[:, :, pl_:] + _bnn(a_rk, v2))
    wt = fold(wu[:, :, :pl_])
    u0 = fold(wu[:, :, pl_:])

    s0 = st_ref[bi]
    y = _bnn(q, s0) + y0
    diag_end = jnp.where((li == lj)[None], jnp.exp(lp_end), 0.0)
    m_t = diag_end + jnp.where(same_head[None], _btn(b_end, wt), 0.0)
    c_t = jnp.where(same_head[None], _btn(b_end, u0) + _btn(k_end, v), 0.0)
    st_ref[bi] = _bnn(m_t, s0) + c_t

    inv_e = 1.0 / RWKV_HEAD_DIM
    yc = y - head_sum(y) * inv_e
    yv = head_sum(yc * yc) * inv_e
    yn = yc * lax.rsqrt(yv + RWKV_GN_EPS) * gng_ref[...] + gnb_ref[...]
    out = (yn + head_sum(r * k * rk_ref[...]) * v) * g
    for p in range(np_):
        y_ref[bi, :, p * pl_:(p + 1) * pl_] = out[p].astype(y_ref.dtype)


def _rwkv_mix(z, layer, *params):
    b, s, zin = z.shape
    d, t = RWKV_DIM, RWKV_CHUNK
    return pl.pallas_call(
        _rwkv_body,
        grid=(s // t,),
        in_specs=[pl.BlockSpec((b, t, zin), lambda c: (0, c, 0))] + [_layer_spec(a, layer) for a in params],
        out_specs=pl.BlockSpec((b, t, d), lambda c: (0, c, 0)),
        out_shape=jax.ShapeDtypeStruct((b, s, d), BF16),
        scratch_shapes=[pltpu.VMEM((b, RWKV_PAIRS, PAIR_LANES, PAIR_LANES), F32), pltpu.VMEM((b, 1, zin), F32)],
        compiler_params=_params("arbitrary"),
        name="rwkv_mix",
    )(z, *params)


def _dil_body(slopes_ref, q_ref, kc_ref, kp_ref, vc_ref, vp_ref, *rest, group, dil, n_sub, n_heads, merge):
    nblk = pl.program_id(3)
    scale = DIL_HEAD_DIM ** -0.5
    nt = (((1,), (1,)), ((), ()))
    qi = lax.broadcasted_iota(jnp.int32, (DIL_BLOCK, DIL_BLOCK), 0)
    ki = lax.broadcasted_iota(jnp.int32, (DIL_BLOCK, DIL_BLOCK), 1)
    rel_c = qi - ki
    if merge:
        o0_ref, l0_ref, o1_ref, l1_ref, y_ref = rest
    else:
        o_ref, lse_ref = rest
    for hu in range(n_heads * n_sub):
        hh, u = divmod(hu, n_sub)
        rows = slice(u * DIL_BLOCK, (u + 1) * DIL_BLOCK)
        cols = slice(hh * DIL_HEAD_DIM, (hh + 1) * DIL_HEAD_DIM)
        head = pl.program_id(2) * n_heads + hh
        bias = slopes_ref[group * DIL_HEADS_PER_GROUP + head] * float(dil)
        dist_c = bias * rel_c.astype(F32)
        dist_p = bias * (rel_c + DIL_BLOCK).astype(F32)
        q = q_ref[rows, cols]
        if u == 0:
            k_prev, v_prev = kp_ref[:, cols], vp_ref[:, cols]
            prev_lim = jnp.where(nblk == 0, -2 * DIL_BLOCK, 0)
        else:
            prows = slice((u - 1) * DIL_BLOCK, u * DIL_BLOCK)
            k_prev, v_prev = kc_ref[prows, cols], vc_ref[prows, cols]
            prev_lim = 0
        s_c = lax.dot_general(q, kc_ref[rows, cols], nt, preferred_element_type=F32) * scale
        s_p = lax.dot_general(q, k_prev, nt, preferred_element_type=F32) * scale
        s_c = jnp.where(rel_c >= 0, s_c - dist_c, NEG_INF)
        s_p = jnp.where(rel_c <= prev_lim, s_p - dist_p, NEG_INF)
        m = jnp.maximum(jnp.max(s_c, axis=-1, keepdims=True), jnp.max(s_p, axis=-1, keepdims=True))
        e_c = jnp.exp(s_c - m)
        e_p = jnp.exp(s_p - m)
        den = jnp.sum(e_c, axis=-1, keepdims=True) + jnp.sum(e_p, axis=-1, keepdims=True)
        acc = (jnp.dot(e_c.astype(BF16), vc_ref[rows, cols], preferred_element_type=F32)
               + jnp.dot(e_p.astype(BF16), v_prev, preferred_element_type=F32))
        o = acc / den
        lse = jnp.broadcast_to(m + jnp.log(den), (DIL_BLOCK, DIL_HEAD_DIM))
        if merge:
            l0, l1 = l0_ref[rows, cols], l1_ref[rows, cols]
            top = jnp.maximum(jnp.maximum(l0, l1), lse)
            w0, w1, w2 = jnp.exp(l0 - top), jnp.exp(l1 - top), jnp.exp(lse - top)
            y = (w0 * o0_ref[rows, cols] + w1 * o1_ref[rows, cols] + w2 * o) / (w0 + w1 + w2)
            y_ref[rows, cols] = y.astype(y_ref.dtype)
        else:
            o_ref[rows, cols] = o.astype(o_ref.dtype)
            lse_ref[rows, cols] = lse


def _dilated_group(slopes, zo, group, batch, s, prior=None):
    n = batch * s
    dil = DIL_PATTERNS[group][1]
    nb = s // dil // DIL_BLOCK
    n_sub = min(nb, 8)
    n_heads = min(8 // n_sub, DIL_HEADS_PER_GROUP)
    steps = nb // n_sub
    ocols = DIL_OUT_DIM // DIL_HEAD_DIM // n_heads
    rows = n_sub * DIL_BLOCK
    width = n_heads * DIL_HEAD_DIM
    gcol = lambda col0: col0 + group * DIL_OUT_DIM
    if dil == 1:
        zv, qkv_cols = zo, (gcol(ZO_DQ), gcol(ZO_DK), gcol(ZO_DV))
    else:
        zv = jnp.concatenate([zo[:, gcol(c):gcol(c) + DIL_OUT_DIM] for c in (ZO_DQ, ZO_DK, ZO_DV)], axis=1)
        zv, qkv_cols = zv.reshape(n // dil, dil * 3 * DIL_OUT_DIM), (0, DIL_OUT_DIM, 2 * DIL_OUT_DIM)
    zcols = zv.shape[1] // dil // width

    def cur(col0):
        c = col0 // width
        return pl.BlockSpec((rows, width), lambda b, r, h, i, sl: (b * steps + i, r * zcols + c + h))

    def prev(col0):
        c = col0 // width
        return pl.BlockSpec((DIL_BLOCK, width),
                            lambda b, r, h, i, sl: (jnp.maximum(b * nb + i * n_sub - 1, 0), r * zcols + c + h))

    ospec = pl.BlockSpec((rows, width), lambda b, r, h, i, sl: (b * steps + i, r * ocols + h))
    oshape = lambda dt: jax.ShapeDtypeStruct((n // dil, dil * DIL_OUT_DIM), dt)
    merge = prior is not None
    extra = [t.reshape(n // dil, dil * DIL_OUT_DIM) for t in prior] if merge else []
    qc, kc, vc = qkv_cols
    out = pl.pallas_call(
        functools.partial(_dil_body, group=group, dil=dil, n_sub=n_sub, n_heads=n_heads, merge=merge),
        grid_spec=pltpu.PrefetchScalarGridSpec(
            num_scalar_prefetch=1, grid=(batch, dil, DIL_HEADS_PER_GROUP // n_heads, steps),
            in_specs=[cur(qc), cur(kc), prev(kc), cur(vc), prev(vc)] + [ospec] * len(extra),
            out_specs=ospec if merge else [ospec, ospec]),
        out_shape=oshape(BF16) if merge else [oshape(BF16), oshape(F32)],
        compiler_params=_params("parallel", "parallel", "parallel", "arbitrary"),
        name="dilated_attn_g%d" % group,
    )(slopes, zv, zv, zv, zv, zv, *extra)
    if merge:
        return out.reshape(n, DIL_OUT_DIM)
    return [t.reshape(n, DIL_OUT_DIM) for t in out]


def _rms(x_bf16, g):
    x = x_bf16.astype(F32)
    return (x * lax.rsqrt(jnp.mean(x * x, axis=-1, keepdims=True) + RMS_EPS) * g).astype(BF16)


def _mla_q_body(x_ref, g_ref, w_ref, cos_ref, sin_ref, q_ref):
    acc = jnp.dot(_rms(x_ref[...], g_ref[...]), w_ref[...], preferred_element_type=F32)
    scale = MLA_QK_DIM ** -0.5
    cos, sin = cos_ref[...], sin_ref[...]
    nope_w = MLA_HEADS * MLA_NOPE_DIM
    for h in range(MLA_HEADS):
        q_ref[0, h, :, 0:MLA_NOPE_DIM] = (acc[:, h * 128:(h + 1) * 128] * scale).astype(BF16)
        base = nope_w + h * 256
        rope = acc[:, base:base + 128] * cos + acc[:, base + 128:base + 256] * sin
        q_ref[0, h, :, MLA_NOPE_DIM:MLA_QK_DIM] = (rope[:, 0:MLA_ROPE_DIM] * scale).astype(BF16)


def _mla_q(zo, g, w, layer, cos, sin, batch, tm=512):
    n = zo.shape[0]
    s = n // batch
    nblk = s // tm
    return pl.pallas_call(
        _mla_q_body,
        grid=(batch, nblk),
        in_specs=[pl.BlockSpec((tm, MLA_Q_LORA), lambda b, i: (b * nblk + i, ZO_QA // MLA_Q_LORA)),
                  _layer_spec(g, layer), _layer_spec(w, layer),
                  pl.BlockSpec((tm, 128), lambda b, i: (i, 0)),
                  pl.BlockSpec((tm, 128), lambda b, i: (i, 0))],
        out_specs=pl.BlockSpec((1, MLA_HEADS, tm, MLA_QK_DIM), lambda b, i: (b, 0, i, 0)),
        out_shape=jax.ShapeDtypeStruct((batch, MLA_HEADS, s, MLA_QK_DIM), BF16),
        compiler_params=_params("parallel", "parallel"),
        name="mla_q_proj",
    )(zo, g, w, cos, sin)


def _mla_kv_body(x_ref, kr_ref, g_ref, w_ref, cos_ref, sin_ref, k_ref, v_ref):
    acc = jnp.dot(_rms(x_ref[...], g_ref[...]), w_ref[...], preferred_element_type=F32)
    kr = kr_ref[...].astype(F32)
    rope = (kr[:, 0:128] * cos_ref[...] + kr[:, 128:256] * sin_ref[...])[:, 0:MLA_ROPE_DIM].astype(BF16)
    for h in range(MLA_HEADS):
        k_ref[0, h, :, 0:MLA_NOPE_DIM] = acc[:, h * 256:h * 256 + 128].astype(BF16)
        k_ref[0, h, :, MLA_NOPE_DIM:MLA_QK_DIM] = rope
        v_ref[0, h] = acc[:, h * 256 + 128:(h + 1) * 256].astype(BF16)


def _mla_kv(zo, g, w, layer, cos, sin, batch, tm=512):
    n = zo.shape[0]
    s = n // batch
    nblk = s // tm
    return pl.pallas_call(
        _mla_kv_body,
        grid=(batch, nblk),
        in_specs=[pl.BlockSpec((tm, MLA_KV_LORA), lambda b, i: (b * nblk + i, ZT_KVA // MLA_KV_LORA)),
                  pl.BlockSpec((tm, 256), lambda b, i: (b * nblk + i, ZT_KR // 256)),
                  _layer_spec(g, layer), _layer_spec(w, layer),
                  pl.BlockSpec((tm, 128), lambda b, i: (i, 0)),
                  pl.BlockSpec((tm, 128), lambda b, i: (i, 0))],
        out_specs=[pl.BlockSpec((1, MLA_HEADS, tm, MLA_QK_DIM), lambda b, i: (b, 0, i, 0)),
                   pl.BlockSpec((1, MLA_HEADS, tm, MLA_V_DIM), lambda b, i: (b, 0, i, 0))],
        out_shape=[jax.ShapeDtypeStruct((batch, MLA_HEADS, s, MLA_QK_DIM), BF16),
                   jax.ShapeDtypeStruct((batch, MLA_HEADS, s, MLA_V_DIM), BF16)],
        compiler_params=_params("parallel", "parallel"),
        name="mla_kv_proj",
    )(zo, zo, g, w, cos, sin)


def _flash_body(qi_ref, kj_ref, q_ref, k_ref, v_ref, o_ref, m_ref, l_ref, acc_ref, *, tq):
    p = pl.program_id(1)
    i = qi_ref[p]
    j = kj_ref[p]
    heads, dv = acc_ref.shape[0], acc_ref.shape[2]

    @pl.when(j == 0)
    def _():
        m_ref[...] = jnp.full_like(m_ref, NEG_INF)
        l_ref[...] = jnp.zeros_like(l_ref)
        acc_ref[...] = jnp.zeros_like(acc_ref)

    def update(masked):
        for h in range(heads):
            s = lax.dot_general(q_ref[0, h], k_ref[0, h], (((1,), (1,)), ((), ())), preferred_element_type=F32)
            if masked:
                qpos = lax.broadcasted_iota(jnp.int32, (tq, tq), 0)
                kpos = lax.broadcasted_iota(jnp.int32, (tq, tq), 1)
                s = jnp.where(kpos <= qpos, s, NEG_INF)
            m_old = m_ref[h]
            m_new = jnp.maximum(m_old, jnp.max(s, axis=-1, keepdims=True))
            alpha = jnp.exp(m_old - m_new)
            e = jnp.exp(s - jnp.concatenate([m_new] * (tq // dv), axis=1))
            l_ref[h] = alpha * l_ref[h] + jnp.sum(e, axis=-1, keepdims=True)
            acc_ref[h] = alpha * acc_ref[h] + jnp.dot(e.astype(BF16), v_ref[0, h], preferred_element_type=F32)
            m_ref[h] = m_new

    @pl.when(j < i)
    def _():
        update(False)

    @pl.when(j == i)
    def _():
        update(True)
        for h in range(heads):
            o_ref[0, :, h * dv:(h + 1) * dv] = (acc_ref[h] / l_ref[h]).astype(o_ref.dtype)


def _flash(q, k, v, tq=512):
    b, h, s, dq = q.shape
    dv = v.shape[-1]
    nq = s // tq
    pairs = [(i, j) for i in range(nq) for j in range(i + 1)]
    qi = jnp.asarray([pr[0] for pr in pairs], jnp.int32)
    kj = jnp.asarray([pr[1] for pr in pairs], jnp.int32)
    return pl.pallas_call(
        functools.partial(_flash_body, tq=tq),
        grid_spec=pltpu.PrefetchScalarGridSpec(
            num_scalar_prefetch=2, grid=(b, len(pairs)),
            in_specs=[pl.BlockSpec((1, h, tq, dq), lambda bi, p, qi, kj: (bi, 0, qi[p], 0)),
                      pl.BlockSpec((1, h, tq, dq), lambda bi, p, qi, kj: (bi, 0, kj[p], 0)),
                      pl.BlockSpec((1, h, tq, dv), lambda bi, p, qi, kj: (bi, 0, kj[p], 0))],
            out_specs=pl.BlockSpec((1, tq, h * dv), lambda bi, p, qi, kj: (bi, qi[p], 0)),
            scratch_shapes=[pltpu.VMEM((h, tq, dv), F32), pltpu.VMEM((h, tq, dv), F32), pltpu.VMEM((h, tq, dv), F32)]),
        out_shape=jax.ShapeDtypeStruct((b, s, h * dv), BF16),
        compiler_params=_params("parallel", "arbitrary"),
        name="mla_flash",
    )(qi, kj, q, k, v)


def _layernorm(h, g, b):
    mu = jnp.mean(h, axis=-1, keepdims=True)
    hc = h - mu
    var = jnp.mean(hc * hc, axis=-1, keepdims=True)
    return hc * lax.rsqrt(var + LN_EPS) * g + b


def _store_token_major(ref, val):
    tiles = val.shape[1] // LANES
    for c in range(tiles):
        ref[pl.ds(c, val.shape[0], stride=tiles), :] = val[:, c * LANES:(c + 1) * LANES]


def _load_token_major(ref, rows):
    tiles = ref.shape[0] // rows
    return jnp.concatenate([ref[pl.ds(c, rows, stride=tiles), :] for c in range(tiles)], axis=1)


def _out_ln_body(ya_ref, yb_ref, yc_ref, w_ref, x_ref, g_ref, b_ref, o_ref, otm_ref, *, alpha):
    ka, kb = ya_ref.shape[1], yb_ref.shape[1]
    acc = (jnp.dot(ya_ref[...], w_ref[0:ka, :], preferred_element_type=F32)
           + jnp.dot(yb_ref[...], w_ref[ka:ka + kb, :], preferred_element_type=F32)
           + jnp.dot(yc_ref[...], w_ref[ka + kb:, :], preferred_element_type=F32))
    x1 = _layernorm(alpha * x_ref[...] + acc, g_ref[...], b_ref[...])
    o_ref[...] = x1
    _store_token_major(otm_ref, x1)


def _out_ln(ya, yb, yc, w, x, g, b, layer, alpha, tm=256):
    n, d = x.shape
    row = lambda c: pl.BlockSpec((tm, c), lambda i: (i, 0))
    return pl.pallas_call(
        functools.partial(_out_ln_body, alpha=alpha),
        grid=(n // tm,),
        in_specs=[row(ya.shape[1]), row(yb.shape[1]), row(yc.shape[1]),
                  _layer_spec(w, layer), row(d), _layer_spec(g, layer), _layer_spec(b, layer)],
        out_specs=[row(d), pl.BlockSpec((tm * (d // LANES), LANES), lambda i: (i, 0))],
        out_shape=[jax.ShapeDtypeStruct((n, d), F32), jax.ShapeDtypeStruct((n * (d // LANES), LANES), F32)],
        compiler_params=_params("parallel"),
        name="out_proj_ln",
    )(ya, yb, yc, w, x, g, b)


def _router_body(x_ref, w_ref, b_ref, idx_ref, gate_ref):
    logits = lax.dot_general(w_ref[...], x_ref[...], (((1,), (1,)), ((), ())),
                             precision=HIGHEST, preferred_element_type=F32)
    scores = _sigmoid(logits)
    sel = scores + b_ref[...]
    ng = N_EXPERT_GROUPS
    sel_j = [sel[j * ng:(j + 1) * ng] for j in range(EXPERTS_PER_GROUP)]
    sc_j = [scores[j * ng:(j + 1) * ng] for j in range(EXPERTS_PER_GROUP)]
    grp = None
    for p in range(EXPERTS_PER_GROUP):
        for q in range(p + 1, EXPERTS_PER_GROUP):
            pair = sel_j[p] + sel_j[q]
            grp = pair if grp is None else jnp.maximum(grp, pair)
    gi = lax.broadcasted_iota(jnp.int32, grp.shape, 0)
    gmax = jnp.max(grp, axis=0, keepdims=True)
    gbest = jnp.min(jnp.where(grp == gmax, gi, ng), axis=0, keepdims=True)
    pick = gi == gbest
    mem = [jnp.sum(jnp.where(pick, t, 0.0), axis=0, keepdims=True) for t in sel_j]
    msc = [jnp.sum(jnp.where(pick, t, 0.0), axis=0, keepdims=True) for t in sc_j]

    def first_argmax(vals, exclude):
        best = jnp.full_like(vals[0], -jnp.inf)
        bi = jnp.zeros(vals[0].shape, jnp.int32)
        bs = jnp.zeros_like(vals[0])
        for j in range(EXPERTS_PER_GROUP):
            ok = vals[j] > best
            if exclude is not None:
                ok = ok & (exclude != j)
            best = jnp.where(ok, vals[j], best)
            bi = jnp.where(ok, j, bi)
            bs = jnp.where(ok, msc[j], bs)
        return bi, bs

    i1, s1 = first_argmax(mem, None)
    i2, s2 = first_argmax(mem, i1)
    tot = s1 + s2
    base = gbest * EXPERTS_PER_GROUP
    idx_ref[0:1, :] = base + i1
    idx_ref[1:2, :] = base + i2
    gate_ref[0:1, :] = s1 / tot
    gate_ref[1:2, :] = s2 / tot


def _router(x, w_t, bias, tm=512):
    n, d = x.shape
    return pl.pallas_call(
        _router_body,
        grid=(n // tm,),
        in_specs=[pl.BlockSpec((tm, d), lambda i: (i, 0)),
                  pl.BlockSpec((N_EXPERTS, d), lambda i: (0, 0)),
                  pl.BlockSpec((N_EXPERTS, 1), lambda i: (0, 0))],
        out_specs=[pl.BlockSpec((TOP_K, tm), lambda i: (0, i)), pl.BlockSpec((TOP_K, tm), lambda i: (0, i))],
        out_shape=[jax.ShapeDtypeStruct((TOP_K, n), jnp.int32), jax.ShapeDtypeStruct((TOP_K, n), F32)],
        compiler_params=_params("parallel"),
        name="moe_router",
    )(x, w_t, bias)


def _expert_body(plan_ref, meta_ref, rows_ref, x_hbm, wg_hbm, wu_hbm, wd_hbm, out_hbm,
                 xg0_ref, xg1_ref, xg2_ref, yo0_ref, yo1_ref, yo2_ref, wfg_ref, wfu_ref, wfd_ref,
                 wgb_ref, wub_ref, wdb_ref, sem_in, sem_out, sem_w, *, tok_bits, n_blocks, layer):
    i = pl.program_id(0)
    n_act = meta_ref[0]
    xg = (xg0_ref, xg1_ref, xg2_ref)
    yo = (yo0_ref, yo1_ref, yo2_ref)
    last = n_blocks - 1
    tpr = wgb_ref.shape[0] // LANES

    def token_rows(ref, index):
        return ref.at[pl.ds(pl.multiple_of(index * tpr, tpr), tpr)]

    def start_gather(blk, par):
        for r in range(MOE_BLOCK):
            tok = rows_ref[blk * MOE_BLOCK + r] & ((1 << tok_bits) - 1)
            pltpu.make_async_copy(token_rows(x_hbm, tok), token_rows(xg[par], r), sem_in.at[par]).start(priority=r % 2)

    def wait_gather(par):
        for r in range(MOE_BLOCK):
            pltpu.make_async_copy(token_rows(x_hbm, 0), token_rows(xg[par], 0), sem_in.at[par]).wait()

    def start_scatter(blk, par):
        for r in range(MOE_BLOCK):
            row = rows_ref[blk * MOE_BLOCK + r] >> tok_bits
            pltpu.make_async_copy(token_rows(yo[par], r), token_rows(out_hbm, row), sem_out.at[par]).start(priority=r % 2)

    def wait_scatter(par):
        for r in range(MOE_BLOCK):
            pltpu.make_async_copy(token_rows(yo[par], 0), token_rows(out_hbm, 0), sem_out.at[par]).wait()

    def compute(par):
        xb = _load_token_major(xg[par], MOE_BLOCK).astype(BF16)
        gt = jnp.dot(xb, wgb_ref[...], preferred_element_type=F32)
        up = jnp.dot(xb, wub_ref[...], preferred_element_type=F32)
        hid = (gt * _sigmoid(gt) * up).astype(BF16)
        _store_token_major(yo[par], jnp.dot(hid, wdb_ref[...], preferred_element_type=F32))

    e_cur, first, wslot, e_next = (plan_ref[k * n_blocks + i] for k in range(4))

    def weight_copies(e, slot):
        pairs = ((wg_hbm, wfg_ref), (wu_hbm, wfu_ref), (wd_hbm, wfd_ref))
        return [pltpu.make_async_copy(w.at[layer, e], wf.at[slot], sem_w.at[slot]) for w, wf in pairs]

    @pl.when((i < n_act) & (first == 1))
    def _():
        @pl.when(i == 0)
        def _():
            for cp in weight_copies(e_cur, wslot):
                cp.start()

        for cp in weight_copies(e_cur, wslot):
            cp.wait()

        @pl.when(e_next >= 0)
        def _():
            for cp in weight_copies(e_next, 1 - wslot):
                cp.start()

        wgb_ref[...] = wfg_ref[wslot].astype(BF16)
        wub_ref[...] = wfu_ref[wslot].astype(BF16)
        wdb_ref[...] = wfd_ref[wslot].astype(BF16)

    @pl.when(i == 0)
    def _():
        start_gather(0, 0)
        start_gather(1, 1)
        yo0_ref[...] = jnp.zeros_like(yo0_ref)
        fill_rows = MOE_BLOCK * tpr
        spare0 = out_hbm.shape[0] - 2 * fill_rows
        fills = [pltpu.make_async_copy(yo0_ref, out_hbm.at[pl.ds(spare0 + h * fill_rows, fill_rows)], sem_out.at[0])
                 for h in range(2)]
        for cp in fills:
            cp.start()
        for cp in fills:
            cp.wait()
        wait_gather(0)
        start_gather(2, 2)
        compute(0)

    res = lax.rem(i, 3)
    for q in range(3):
        prv, nxt = (q + 2) % 3, (q + 1) % 3

        @pl.when((i > 0) & (i < n_act) & (res == q))
        def _(q=q, prv=prv):
            wait_gather(q)

            @pl.when(i >= 3)
            def _():
                wait_scatter(q)

            start_gather(jnp.minimum(i + 2, last), prv)
            start_scatter(i - 1, prv)
            compute(q)

    for q in range(3):
        prv, nxt = (q + 2) % 3, (q + 1) % 3

        @pl.when((i == n_act - 1) & (res == q))
        def _(q=q, prv=prv, nxt=nxt):
            start_scatter(i, q)

            @pl.when(i >= 2)
            def _():
                wait_scatter(nxt)

            wait_scatter(prv)
            wait_scatter(q)
            wait_gather(nxt)
            wait_gather(prv)


def _experts(plan, meta, rows, x_tm, wg, wu, wd, layer):
    d = wg.shape[2]
    tpr = d // LANES
    n = x_tm.shape[0] // tpr
    nb = plan.shape[0] // 4
    buf = pltpu.VMEM((MOE_BLOCK * tpr, LANES), F32)
    hbm = pl.BlockSpec(memory_space=pl.ANY)
    return pl.pallas_call(
        functools.partial(_expert_body, tok_bits=_tok_bits(n), n_blocks=nb, layer=layer),
        grid_spec=pltpu.PrefetchScalarGridSpec(
            num_scalar_prefetch=3, grid=(nb,),
            in_specs=[hbm, hbm, hbm, hbm],
            out_specs=hbm,
            scratch_shapes=[buf] * 6 + [
                pltpu.VMEM((2, d, D_EXPERT), F32), pltpu.VMEM((2, d, D_EXPERT), F32), pltpu.VMEM((2, D_EXPERT, d), F32),
                pltpu.VMEM((d, D_EXPERT), BF16), pltpu.VMEM((d, D_EXPERT), BF16), pltpu.VMEM((D_EXPERT, d), BF16),
                pltpu.SemaphoreType.DMA((3,)), pltpu.SemaphoreType.DMA((3,)), pltpu.SemaphoreType.DMA((2,))]),
        out_shape=jax.ShapeDtypeStruct(((TOP_K * n + 2 * MOE_BLOCK) * tpr, LANES), F32),
        compiler_params=_params("arbitrary"),
        name="moe_experts",
    )(plan, meta, rows, x_tm, wg, wu, wd)


def _tok_bits(n):
    return max((n - 1).bit_length(), 1)


def _dispatch(idx):
    n = idx.shape[1]
    a_tot = TOP_K * n
    e_flat = idx.reshape(a_tot)
    onehot = (e_flat[:, None] == jnp.arange(N_EXPERTS, dtype=jnp.int32)[None, :]).astype(F32)
    chunk = 128
    oh = onehot.reshape(a_tot // chunk, chunk, N_EXPERTS)
    tri = (jnp.arange(chunk)[None, :] <= jnp.arange(chunk)[:, None]).astype(F32)
    within = jnp.einsum('ts,csn->ctn', tri, oh)
    tot = within[:, -1, :]
    csum = (within + (jnp.cumsum(tot, axis=0) - tot)[:, None, :]).reshape(a_tot, N_EXPERTS)
    rank = jnp.sum(onehot * csum, axis=1).astype(jnp.int32) - 1
    counts = csum[-1].astype(jnp.int32)
    padded = (counts + MOE_BLOCK - 1) // MOE_BLOCK * MOE_BLOCK
    pad_end = jnp.cumsum(padded)
    pad_start = pad_end - padded
    dest = pad_start[e_flat] + rank
    nb = (a_tot + N_EXPERTS * (MOE_BLOCK - 1) + MOE_BLOCK - 1) // MOE_BLOCK
    p_rows = nb * MOE_BLOCK
    bits = _tok_bits(n)
    a_ids = jnp.arange(a_tot, dtype=jnp.int32)
    pr = jnp.arange(p_rows, dtype=jnp.int32)
    spare = a_tot + ((pr // MOE_BLOCK) % 2) * MOE_BLOCK + pr % MOE_BLOCK
    rows = (spare << bits).at[dest].set((a_ids % n) | (a_ids << bits))
    blk_start = jnp.arange(nb, dtype=jnp.int32) * MOE_BLOCK
    blk_e = jnp.sum((pad_end[None, :] <= blk_start[:, None]).astype(jnp.int32), axis=1)
    blk_e = jnp.minimum(blk_e, N_EXPERTS - 1)
    n_act = jnp.minimum(pad_end[-1] // MOE_BLOCK + 1, nb).astype(jnp.int32)
    steps = jnp.arange(nb, dtype=jnp.int32)
    first = (steps < n_act) & ((steps == 0) | (blk_e != jnp.roll(blk_e, 1)))
    wslot = (jnp.cumsum(first.astype(jnp.int32)) - 1) & 1
    next_first = lax.cummin(jnp.where(first, steps, nb), reverse=True)
    next_first = jnp.concatenate([next_first[1:], jnp.full((1,), nb, jnp.int32)])
    e_next = jnp.where(next_first < nb, blk_e[jnp.minimum(next_first, nb - 1)], -1)
    plan = jnp.concatenate([blk_e, first.astype(jnp.int32), wslot, e_next]).astype(jnp.int32)
    return plan, n_act.reshape(1), rows


def _final_body(x_ref, ya_ref, yb_ref, ga_ref, gb_ref, g_ref, b_ref, p_ref, wp_ref, wg_ref, o_ref, ob_ref, *, alpha):
    rows = x_ref.shape[0]
    ffn = _load_token_major(ya_ref, rows) * ga_ref[...] + _load_token_major(yb_ref, rows) * gb_ref[...]
    x2 = _layernorm(alpha * x_ref[...] + ffn, g_ref[...], b_ref[...])
    gate = jnp.dot(x2.astype(BF16), wg_ref[...], preferred_element_type=F32)
    proj = jnp.dot(p_ref[...].astype(BF16), wp_ref[...], preferred_element_type=F32)
    out = x2 + _sigmoid(gate) * proj
    o_ref[...] = out
    ob_ref[...] = out.astype(BF16)


def _final(x, y2, gates, g, b, p, wp, wg, layer, alpha, tm=256):
    n, d = x.shape
    nblk = n // tm
    row = pl.BlockSpec((tm, d), lambda i: (i, 0))
    return pl.pallas_call(
        functools.partial(_final_body, alpha=alpha),
        grid=(nblk,),
        in_specs=[row, pl.BlockSpec((tm * (d // LANES), LANES), lambda i: (i, 0)),
                  pl.BlockSpec((tm * (d // LANES), LANES), lambda i: (i + nblk, 0)),
                  pl.BlockSpec((tm, 1), lambda i: (i, 0)), pl.BlockSpec((tm, 1), lambda i: (i + nblk, 0)),
                  _layer_spec(g, layer), _layer_spec(b, layer),
                  pl.BlockSpec((None, tm, PLE_DIM), lambda i: (layer, i, 0)),
                  _layer_spec(wp, layer), _layer_spec(wg, layer)],
        out_specs=[row, row],
        out_shape=[jax.ShapeDtypeStruct((n, d), F32), jax.ShapeDtypeStruct((n, d), BF16)],
        compiler_params=_params("parallel"),
        name="moe_combine_ln_ple",
    )(x, y2, y2, gates, gates, g, b, p, wp, wg)


def _rot_cols(w):
    half = MLA_ROPE_DIM // 2
    return jnp.concatenate([-w[..., half:], w[..., :half]], axis=-1)


def _pad_lanes(w, width=128):
    return jnp.pad(w, [(0, 0)] * (w.ndim - 1) + [(0, width - w.shape[-1])])


def _rope_tables(s):
    half = MLA_ROPE_DIM // 2
    inv = ROPE_THETA ** (-jnp.arange(half, dtype=F32) / half)
    ang = jnp.arange(s, dtype=F32)[:, None] * inv[None, :]
    cos = jnp.concatenate([jnp.cos(ang), jnp.cos(ang)], axis=-1)
    sin = jnp.concatenate([jnp.sin(ang), jnp.sin(ang)], axis=-1)
    return _pad_lanes(cos), _pad_lanes(sin)


def kernel(x, p, w_in, rwkv_mu, rwkv_w0, rwkv_w_up, rwkv_a0, rwkv_a_up, rwkv_g_up, rwkv_k_k, rwkv_k_a, rwkv_r_k, rwkv_gn_g, rwkv_gn_b, mla_qa_g, mla_w_uq, mla_kva_g, mla_w_ukv, w_out, ln1_g, ln1_b, router_w, router_b, moe_w_gate, moe_w_up, moe_w_down, ln2_g, ln2_b, ple_w_proj, ple_w_gate):
    batch, s, d = x.shape
    depth = w_in.shape[0]
    n = batch * s
    alpha = (2 * depth) ** 0.25
    assert s % (DIL_PATTERNS[-1][1] * DIL_BLOCK) == 0 and s % 512 == 0

    cos, sin = _rope_tables(s)
    slopes = jnp.exp2(-ALIBI_MAX_BIAS * jnp.arange(1, 13, dtype=F32) / 12)
    perm = np.array([4 * g + j for j in range(EXPERTS_PER_GROUP) for g in range(N_EXPERT_GROUPS)])
    router_wt = router_w.T[perm]
    router_bt = router_b[perm].reshape(N_EXPERTS, 1)

    row = lambda t: t.reshape(depth, 1, -1)
    hpar = lambda t: t.reshape(depth, RWKV_PAIRS, 1, PAIR_LANES)
    kva0 = RWKV_IN_DIM + ZO_DIM
    w_kr = w_in[:, :, kva0 + MLA_KV_LORA:]
    w_tail = jnp.concatenate([w_in[:, :, kva0:kva0 + MLA_KV_LORA], _pad_lanes(w_kr), _pad_lanes(_rot_cols(w_kr))],
                             axis=2).astype(BF16)
    w_up_pad = jnp.pad(rwkv_w_up, ((0, 0), (0, 64), (0, 0)))
    w_up_hi = w_up_pad.astype(BF16)
    w_up_parts = jnp.stack([w_up_hi, (w_up_pad - w_up_hi.astype(F32)).astype(BF16)], axis=1)
    a_up_pad = jnp.pad(rwkv_a_up, ((0, 0), (64, 0), (0, 0))).astype(BF16)
    rwkv_params = (row(rwkv_mu), row(rwkv_w0), row(rwkv_a0), row(rwkv_k_k), row(rwkv_k_a), w_up_parts, a_up_pad,
                   rwkv_g_up.astype(BF16), hpar(rwkv_r_k), hpar(rwkv_gn_g), hpar(rwkv_gn_b))
    wq = mla_w_uq.reshape(depth, MLA_Q_LORA, MLA_HEADS, MLA_QK_DIM)
    wq_rope = wq[..., MLA_NOPE_DIM:]
    wq_all = jnp.concatenate(
        [wq[..., :MLA_NOPE_DIM].reshape(depth, MLA_Q_LORA, -1),
         jnp.concatenate([_pad_lanes(wq_rope), _pad_lanes(_rot_cols(wq_rope))], axis=-1).reshape(depth, MLA_Q_LORA, -1)],
        axis=2).astype(BF16)
    w_ukv_b, w_out_b = mla_w_ukv.astype(BF16), w_out.astype(BF16)
    ple_proj_b, ple_gate_b = ple_w_proj.astype(BF16), ple_w_gate.astype(BF16)
    qa_g, kva_g = row(mla_qa_g), row(mla_kva_g)
    ln1 = (row(ln1_g), row(ln1_b))
    ln2 = (row(ln2_g), row(ln2_b))
    p_rows = p.reshape(depth, n, PLE_DIM)

    xf = x.reshape(n, d)
    xb = xf.astype(BF16)
    for li in range(depth):
        z_r = _in_proj(xb, w_in, li, 0, RWKV_IN_DIM, F32)
        z_o = _in_proj(xb, w_in, li, RWKV_IN_DIM, ZO_DIM, BF16)
        z_t = _matmul(xb, w_tail, li, BF16, 1024, ZT_DIM)

        y_a = _rwkv_mix(z_r.reshape(batch, s, RWKV_IN_DIM), li, *rwkv_params).reshape(n, RWKV_DIM)

        prior = _dilated_group(slopes, z_o, 0, batch, s) + _dilated_group(slopes, z_o, 1, batch, s)
        y_b = _dilated_group(slopes, z_o, 2, batch, s, prior=prior)

        q_c = _mla_q(z_o, qa_g, wq_all, li, cos, sin, batch)
        k_c, v_c = _mla_kv(z_t, kva_g, w_ukv_b, li, cos, sin, batch)
        y_c = _flash(q_c, k_c, v_c).reshape(n, MLA_OUT_DIM)

        x1, x1_tm = _out_ln(y_a, y_b, y_c, w_out_b, xf, *ln1, li, alpha)

        idx, gate = _router(x1, router_wt, router_bt)
        plan, meta, rows = _dispatch(idx)
        y2 = _experts(plan, meta, rows, x1_tm, moe_w_gate, moe_w_up, moe_w_down, li)

        xf, xb = _final(x1, y2, gate.reshape(TOP_K * n, 1), *ln2, p_rows, ple_proj_b, ple_gate_b, li, alpha)
    return xf.reshape(batch, s, d)
```

```python
import functools
import math

import numpy as np
import jax
import jax.numpy as jnp
from jax import lax
from jax.experimental import pallas as pl
from jax.experimental.pallas import tpu as pltpu

F32 = jnp.float32
BF16 = jnp.bfloat16
HIGHEST = lax.Precision.HIGHEST

PLE_DIM = 256
RWKV_HEADS = 12
RWKV_HEAD_DIM = 64
RWKV_DIM = RWKV_HEADS * RWKV_HEAD_DIM
RWKV_LORA_PAD = 128
RWKV_GATE_LORA = 128
RWKV_IN_DIM = 3 * RWKV_DIM + RWKV_LORA_PAD + RWKV_GATE_LORA
RWKV_GN_EPS = 64e-5
RWKV_CHUNK = 64
DIL_PATTERNS = ((128, 1), (512, 4), (2048, 16))
DIL_GROUPS = 3
DIL_HEADS_PER_GROUP = 4
DIL_HEAD_DIM = 128
DIL_QKV_DIM = DIL_GROUPS * DIL_HEADS_PER_GROUP * DIL_HEAD_DIM
DIL_OUT_DIM = DIL_HEADS_PER_GROUP * DIL_HEAD_DIM
DIL_BLOCK = 128
ALIBI_MAX_BIAS = 8.0
MLA_HEADS = 6
MLA_NOPE_DIM = 128
MLA_ROPE_DIM = 64
MLA_V_DIM = 128
MLA_Q_LORA = 512
MLA_KV_LORA = 256
MLA_QK_DIM = MLA_NOPE_DIM + MLA_ROPE_DIM
MLA_OUT_DIM = MLA_HEADS * MLA_V_DIM
ROPE_THETA = 10000.0
N_EXPERTS = 32
N_EXPERT_GROUPS = 8
EXPERTS_PER_GROUP = 4
TOP_K = 2
D_EXPERT = 512
MOE_BLOCK = 128
LANES = 128
LN_EPS = 1e-5
RMS_EPS = 1e-6
NEG_INF = -1e30

ZO_DQ = 0
ZO_DK = DIL_QKV_DIM
ZO_DV = 2 * DIL_QKV_DIM
ZO_QA = 3 * DIL_QKV_DIM
ZO_DIM = ZO_QA + MLA_Q_LORA
ZT_KVA = 0
ZT_KR = MLA_KV_LORA
ZT_DIM = ZT_KR + 256

V7X_VMEM_LIMIT_BYTES = 48 * 1024 * 1024


def _params(*sem):
    return pltpu.CompilerParams(dimension_semantics=sem, vmem_limit_bytes=V7X_VMEM_LIMIT_BYTES)


def _sigmoid(x):
    return 1.0 / (1.0 + jnp.exp(-x))


def _mm_body(x_ref, w_ref, o_ref):
    o_ref[...] = jnp.dot(x_ref[...], w_ref[...], preferred_element_type=F32).astype(o_ref.dtype)


def _layer_spec(arr, layer):
    zeros = (0,) * (arr.ndim - 1)
    return pl.BlockSpec((None,) + arr.shape[1:], lambda *_: (layer,) + zeros)


def _matmul(x, w_all, layer, out_dtype, tm, tn):
    m, k = x.shape
    n = w_all.shape[2]
    return pl.pallas_call(
        _mm_body,
        grid=(m // tm, n // tn),
        in_specs=[pl.BlockSpec((tm, k), lambda i, j: (i, 0)),
                  pl.BlockSpec((None, k, tn), lambda i, j: (layer, 0, j))],
        out_specs=pl.BlockSpec((tm, tn), lambda i, j: (i, j)),
        out_shape=jax.ShapeDtypeStruct((m, n), out_dtype),
        compiler_params=_params("parallel", "arbitrary"),
        name="in_proj_tail",
    )(x, w_all)


def _in_proj_body(x_ref, w_ref, o_ref, wb_ref):
    @pl.when(pl.program_id(1) == 0)
    def _():
        wb_ref[...] = w_ref[0].astype(BF16)

    o_ref[...] = jnp.dot(x_ref[...], wb_ref[...], preferred_element_type=F32).astype(o_ref.dtype)


def _in_proj(x, w_all, layer, col0, ncols, out_dtype, tm=1024, tn=512):
    m, k = x.shape
    assert col0 % tn == 0 and ncols % tn == 0 and col0 + ncols <= w_all.shape[2]
    cb0 = col0 // tn
    return pl.pallas_call(
        _in_proj_body,
        grid=(ncols // tn, m // tm),
        in_specs=[pl.BlockSpec((tm, k), lambda j, i: (i, 0)),
                  pl.BlockSpec((1, k, tn), lambda j, i: (layer, 0, cb0 + j))],
        out_specs=pl.BlockSpec((tm, tn), lambda j, i: (i, j)),
        out_shape=jax.ShapeDtypeStruct((m, ncols), out_dtype),
        scratch_shapes=[pltpu.VMEM((k, tn), BF16)],
        compiler_params=_params("parallel", "arbitrary"),
        name="in_proj",
    )(x, w_all)


def _bf16_parts(x, n):
    parts = []
    for _ in range(n):
        part = x.astype(BF16)
        parts.append(part)
        x = x - part.astype(F32)
    return parts


def _bdot(a, b, dims):
    return lax.dot_general(a.astype(BF16), b.astype(BF16), (dims, ((0,), (0,))), preferred_element_type=F32)


def _bnt(a, b):
    return _bdot(a, b, ((2,), (2,)))


def _bnn(a, b):
    return _bdot(a, b, ((2,), (1,)))


def _btn(a, b):
    return _bdot(a, b, ((1,), (1,)))


RWKV_PAIRS = RWKV_HEADS // 2
PAIR_LANES = 2 * RWKV_HEAD_DIM


def _rwkv_body(z_ref, mu_ref, w0_ref, a0_ref, kk_ref, ka_ref, wup_ref, aup_ref, gup_ref, rk_ref, gng_ref, gnb_ref,
               y_ref, st_ref, carry_ref):
    @pl.when(pl.program_id(0) == 0)
    def _():
        st_ref[...] = jnp.zeros_like(st_ref)
        carry_ref[...] = jnp.zeros_like(carry_ref)

    for bi in range(z_ref.shape[0]):
        _rwkv_chunk(bi, z_ref, mu_ref, w0_ref, a0_ref, kk_ref, ka_ref, wup_ref, aup_ref, gup_ref, rk_ref, gng_ref,
                    gnb_ref, y_ref, st_ref, carry_ref)


def _rwkv_chunk(bi, z_ref, mu_ref, w0_ref, a0_ref, kk_ref, ka_ref, wup_ref, aup_ref, gup_ref, rk_ref, gng_ref, gnb_ref,
                y_ref, st_ref, carry_ref):
    t, d, np_, pl_ = RWKV_CHUNK, RWKV_DIM, RWKV_PAIRS, PAIR_LANES

    z = z_ref[bi]
    row = lax.broadcasted_iota(jnp.int32, z.shape, 0)
    zprev = jnp.where(row == 0, carry_ref[bi], pltpu.roll(z, 1, axis=0))
    carry_ref[bi] = z[t - 1:t, :]
    zs = z + (zprev - z) * mu_ref[...]
    r_w, k_w, v_w = zs[:, 0:d], zs[:, d:2 * d], zs[:, 2 * d:3 * d]
    lora = zs[:, 3 * d:3 * d + RWKV_LORA_PAD]
    gd = zs[:, 3 * d + RWKV_LORA_PAD:]
    th_hi, th_lo = _bf16_parts(jnp.tanh(lora), 2)
    u = (w0_ref[...] + jnp.dot(th_hi, wup_ref[0], preferred_element_type=F32)
         + jnp.dot(th_hi, wup_ref[1], preferred_element_type=F32)
         + jnp.dot(th_lo, wup_ref[0], preferred_element_type=F32))
    softplus = jnp.maximum(-u, 0.0) + jnp.log(1.0 + jnp.exp(-jnp.abs(u)))
    lw_w = -jnp.exp(-softplus - 0.5)
    a_w = _sigmoid(a0_ref[...] + jnp.dot(lora.astype(BF16), aup_ref[...], preferred_element_type=F32))
    g_w = jnp.dot(_sigmoid(gd).astype(BF16), gup_ref[...], preferred_element_type=F32)
    kmod_w = k_w * (1.0 + (a_w - 1.0) * ka_ref[...])
    kk_w = k_w * kk_ref[...]
    ti = lax.broadcasted_iota(jnp.int32, (t, t), 0)
    si = lax.broadcasted_iota(jnp.int32, (t, t), 1)
    incl = si <= ti
    strict = si < ti
    tri = incl.astype(BF16)
    lp_w = sum(jnp.dot(tri, part, preferred_element_type=F32) for part in _bf16_parts(lw_w, 3))

    pairs = lambda x: jnp.stack([x[:, p * pl_:(p + 1) * pl_] for p in range(np_)], axis=0)
    r, k, v, a, lw, lp, g = (pairs(x) for x in (r_w, kmod_w, v_w, a_w, lw_w, lp_w, g_w))
    li = lax.broadcasted_iota(jnp.int32, (pl_, pl_), 0)
    lj = lax.broadcasted_iota(jnp.int32, (pl_, pl_), 1)
    same_head = (li // RWKV_HEAD_DIM) == (lj // RWKV_HEAD_DIM)
    head_ones = same_head.astype(BF16)

    def head_sum(x):
        parts = _bf16_parts(x.reshape(np_ * t, pl_), 2)
        return sum(jnp.dot(part, head_ones, preferred_element_type=F32) for part in parts).reshape(np_, t, pl_)

    kk = pairs(kk_w)
    kk = kk / jnp.maximum(jnp.sqrt(head_sum(kk * kk)), 1e-12)
    lp_end = lp[:, t - 1:t, :]
    p_inv = jnp.exp(-lp)
    at = -kk * jnp.exp(lp - lw)
    bt = kk * a * p_inv
    kt = k * p_inv
    rt = r * jnp.exp(lp)
    to_end = jnp.exp(lp_end - lp)
    b_end = kk * a * to_end
    k_end = k * to_end

    lane = lax.broadcasted_iota(jnp.int32, (1, 1, pl_), 2)
    m0 = (lane < RWKV_HEAD_DIM).astype(F32)
    msk = jnp.concatenate([jnp.broadcast_to(m0, (np_, 1, pl_)), jnp.broadcast_to(1.0 - m0, (np_, 1, pl_))], axis=0)
    dup = lambda x: jnp.concatenate([x, x], axis=0)
    fold = lambda x: x[:np_] + x[np_:]
    lhs_a = dup(at) * msk
    lhs_r = dup(rt) * msk
    v2 = dup(v) * msk
    ar = jnp.concatenate([lhs_a, lhs_r], axis=1)
    x_b = _bnt(ar, dup(bt))
    x_k = _bnt(ar, dup(kt))
    a_ab = jnp.where(strict[None], x_b[:, :t], 0.0)
    a_rb = jnp.where(incl[None], x_b[:, t:], 0.0)
    a_ak = jnp.where(strict[None], x_k[:, :t], 0.0)
    a_rk = jnp.where(incl[None], x_k[:, t:], 0.0)

    sub = 16
    same_blk = ((ti // sub) == (si // sub))[None]
    eye = (ti == si).astype(F32)[None]
    ld = jnp.where(same_blk, a_ab, 0.0)
    lo = a_ab - ld
    dinv = eye + ld
    pw = ld
    for _ in range(3):
        pw = _bnn(pw, pw)
        dinv = dinv + _bnn(dinv, pw)
    n1 = _bnn(dinv, lo)
    n2 = _bnn(n1, n1)
    tinv = eye + n1 + n2 + _bnn(n1, n2)
    tinv = _bnn(tinv, dinv)

    akv = _bnn(a_ak, v2)
    wu = _bnn(tinv, jnp.concatenate([lhs_a, akv], axis=2))
    qy = _bnn(a_rb, wu)
    q = fold(lhs_r + qy[:, :, :pl_])
    y0 = fold(qy[:, :, pl_:] + _bnn(a_rk, v2))
    wt = fold(wu[:, :, :pl_])
    u0 = fold(wu[:, :, pl_:])

    s0 = st_ref[bi]
    y = _bnn(q, s0) + y0
    diag_end = jnp.where((li == lj)[None], jnp.exp(lp_end), 0.0)
    m_t = diag_end + jnp.where(same_head[None], _btn(b_end, wt), 0.0)
    c_t = jnp.where(same_head[None], _btn(b_end, u0) + _btn(k_end, v), 0.0)
    st_ref[bi] = _bnn(m_t, s0) + c_t

    inv_e = 1.0 / RWKV_HEAD_DIM
    yc = y - head_sum(y) * inv_e
    yv = head_sum(yc * yc) * inv_e
    yn = yc * lax.rsqrt(yv + RWKV_GN_EPS) * gng_ref[...] + gnb_ref[...]
    out = (yn + head_sum(r * k * rk_ref[...]) * v) * g
    for p in range(np_):
        y_ref[bi, :, p * pl_:(p + 1) * pl_] = out[p].astype(y_ref.dtype)


def _rwkv_mix(z, layer, *params):
    b, s, zin = z.shape
    d, t = RWKV_DIM, RWKV_CHUNK
    return pl.pallas_call(
        _rwkv_body,
        grid=(s // t,),
        in_specs=[pl.BlockSpec((b, t, zin), lambda c: (0, c, 0))] + [_layer_spec(a, layer) for a in params],
        out_specs=pl.BlockSpec((b, t, d), lambda c: (0, c, 0)),
        out_shape=jax.ShapeDtypeStruct((b, s, d), BF16),
        scratch_shapes=[pltpu.VMEM((b, RWKV_PAIRS, PAIR_LANES, PAIR_LANES), F32), pltpu.VMEM((b, 1, zin), F32)],
        compiler_params=_params("arbitrary"),
        name="rwkv_mix",
    )(z, *params)


def _dil_body(slopes_ref, q_ref, kc_ref, kp_ref, vc_ref, vp_ref, *rest, group, dil, n_sub, n_heads, merge):
    nblk = pl.program_id(3)
    scale = DIL_HEAD_DIM ** -0.5
    nt = (((1,), (1,)), ((), ()))
    qi = lax.broadcasted_iota(jnp.int32, (DIL_BLOCK, DIL_BLOCK), 0)
    ki = lax.broadcasted_iota(jnp.int32, (DIL_BLOCK, DIL_BLOCK), 1)
    rel_c = qi - ki
    if merge:
        o0_ref, l0_ref, o1_ref, l1_ref, y_ref = rest
    else:
        o_ref, lse_ref = rest
    for hu in range(n_heads * n_sub):
        hh, u = divmod(hu, n_sub)
        rows = slice(u * DIL_BLOCK, (u + 1) * DIL_BLOCK)
        cols = slice(hh * DIL_HEAD_DIM, (hh + 1) * DIL_HEAD_DIM)
        head = pl.program_id(2) * n_heads + hh
        bias = slopes_ref[group * DIL_HEADS_PER_GROUP + head] * float(dil)
        dist_c = bias * rel_c.astype(F32)
        dist_p = bias * (rel_c + DIL_BLOCK).astype(F32)
        q = q_ref[rows, cols]
        if u == 0:
            k_prev, v_prev = kp_ref[:, cols], vp_ref[:, cols]
            prev_lim = jnp.where(nblk == 0, -2 * DIL_BLOCK, 0)
        else:
            prows = slice((u - 1) * DIL_BLOCK, u * DIL_BLOCK)
            k_prev, v_prev = kc_ref[prows, cols], vc_ref[prows, cols]
            prev_lim = 0
        s_c = lax.dot_general(q, kc_ref[rows, cols], nt, preferred_element_type=F32) * scale
        s_p = lax.dot_general(q, k_prev, nt, preferred_element_type=F32) * scale
        s_c = jnp.where(rel_c >= 0, s_c - dist_c, NEG_INF)
        s_p = jnp.where(rel_c <= prev_lim, s_p - dist_p, NEG_INF)
        m = jnp.maximum(jnp.max(s_c, axis=-1, keepdims=True), jnp.max(s_p, axis=-1, keepdims=True))
        e_c = jnp.exp(s_c - m)
        e_p = jnp.exp(s_p - m)
        den = jnp.sum(e_c, axis=-1, keepdims=True) + jnp.sum(e_p, axis=-1, keepdims=True)
        acc = (jnp.dot(e_c.astype(BF16), vc_ref[rows, cols], preferred_element_type=F32)
               + jnp.dot(e_p.astype(BF16), v_prev, preferred_element_type=F32))
        o = acc / den
        lse = jnp.broadcast_to(m + jnp.log(den), (DIL_BLOCK, DIL_HEAD_DIM))
        if merge:
            l0, l1 = l0_ref[rows, cols], l1_ref[rows, cols]
            top = jnp.maximum(jnp.maximum(l0, l1), lse)
            w0, w1, w2 = jnp.exp(l0 - top), jnp.exp(l1 - top), jnp.exp(lse - top)
            y = (w0 * o0_ref[rows, cols] + w1 * o1_ref[rows, cols] + w2 * o) / (w0 + w1 + w2)
            y_ref[rows, cols] = y.astype(y_ref.dtype)
        else:
            o_ref[rows, cols] = o.astype(o_ref.dtype)
            lse_ref[rows, cols] = lse


def _dilated_group(slopes, zo, group, batch, s, prior=None):
    n = batch * s
    dil = DIL_PATTERNS[group][1]
    nb = s // dil // DIL_BLOCK
    n_sub = min(nb, 8)
    n_heads = min(8 // n_sub, DIL_HEADS_PER_GROUP)
    steps = nb // n_sub
    ocols = DIL_OUT_DIM // DIL_HEAD_DIM // n_heads
    rows = n_sub * DIL_BLOCK
    width = n_heads * DIL_HEAD_DIM
    gcol = lambda col0: col0 + group * DIL_OUT_DIM
    if dil == 1:
        zv, qkv_cols = zo, (gcol(ZO_DQ), gcol(ZO_DK), gcol(ZO_DV))
    else:
        zv = jnp.concatenate([zo[:, gcol(c):gcol(c) + DIL_OUT_DIM] for c in (ZO_DQ, ZO_DK, ZO_DV)], axis=1)
        zv, qkv_cols = zv.reshape(n // dil, dil * 3 * DIL_OUT_DIM), (0, DIL_OUT_DIM, 2 * DIL_OUT_DIM)
    zcols = zv.shape[1] // dil // width

    def cur(col0):
        c = col0 // width
        return pl.BlockSpec((rows, width), lambda b, r, h, i, sl: (b * steps + i, r * zcols + c + h))

    def prev(col0):
        c = col0 // width
        return pl.BlockSpec((DIL_BLOCK, width),
                            lambda b, r, h, i, sl: (jnp.maximum(b * nb + i * n_sub - 1, 0), r * zcols + c + h))

    ospec = pl.BlockSpec((rows, width), lambda b, r, h, i, sl: (b * steps + i, r * ocols + h))
    oshape = lambda dt: jax.ShapeDtypeStruct((n // dil, dil * DIL_OUT_DIM), dt)
    merge = prior is not None
    extra = [t.reshape(n // dil, dil * DIL_OUT_DIM) for t in prior] if merge else []
    qc, kc, vc = qkv_cols
    out = pl.pallas_call(
        functools.partial(_dil_body, group=group, dil=dil, n_sub=n_sub, n_heads=n_heads, merge=merge),
        grid_spec=pltpu.PrefetchScalarGridSpec(
            num_scalar_prefetch=1, grid=(batch, dil, DIL_HEADS_PER_GROUP // n_heads, steps),
            in_specs=[cur(qc), cur(kc), prev(kc), cur(vc), prev(vc)] + [ospec] * len(extra),
            out_specs=ospec if merge else [ospec, ospec]),
        out_shape=oshape(BF16) if merge else [oshape(BF16), oshape(F32)],
        compiler_params=_params("parallel", "parallel", "parallel", "arbitrary"),
        name="dilated_attn_g%d" % group,
    )(slopes, zv, zv, zv, zv, zv, *extra)
    if merge:
        return out.reshape(n, DIL_OUT_DIM)
    return [t.reshape(n, DIL_OUT_DIM) for t in out]


def _rms(x_bf16, g):
    x = x_bf16.astype(F32)
    return (x * lax.rsqrt(jnp.mean(x * x, axis=-1, keepdims=True) + RMS_EPS) * g).astype(BF16)


def _mla_q_body(x_ref, g_ref, w_ref, cos_ref, sin_ref, q_ref):
    acc = jnp.dot(_rms(x_ref[...], g_ref[...]), w_ref[...], preferred_element_type=F32)
    scale = MLA_QK_DIM ** -0.5
    cos, sin = cos_ref[...], sin_ref[...]
    nope_w = MLA_HEADS * MLA_NOPE_DIM
    for h in range(MLA_HEADS):
        q_ref[0, h, :, 0:MLA_NOPE_DIM] = (acc[:, h * 128:(h + 1) * 128] * scale).astype(BF16)
        base = nope_w + h * 256
        rope = acc[:, base:base + 128] * cos + acc[:, base + 128:base + 256] * sin
        q_ref[0, h, :, MLA_NOPE_DIM:MLA_QK_DIM] = (rope[:, 0:MLA_ROPE_DIM] * scale).astype(BF16)


def _mla_q(zo, g, w, layer, cos, sin, batch, tm=512):
    n = zo.shape[0]
    s = n // batch
    nblk = s // tm
    return pl.pallas_call(
        _mla_q_body,
        grid=(batch, nblk),
        in_specs=[pl.BlockSpec((tm, MLA_Q_LORA), lambda b, i: (b * nblk + i, ZO_QA // MLA_Q_LORA)),
                  _layer_spec(g, layer), _layer_spec(w, layer),
                  pl.BlockSpec((tm, 128), lambda b, i: (i, 0)),
                  pl.BlockSpec((tm, 128), lambda b, i: (i, 0))],
        out_specs=pl.BlockSpec((1, MLA_HEADS, tm, MLA_QK_DIM), lambda b, i: (b, 0, i, 0)),
        out_shape=jax.ShapeDtypeStruct((batch, MLA_HEADS, s, MLA_QK_DIM), BF16),
        compiler_params=_params("parallel", "parallel"),
        name="mla_q_proj",
    )(zo, g, w, cos, sin)


def _mla_kv_body(x_ref, kr_ref, g_ref, w_ref, cos_ref, sin_ref, k_ref, v_ref):
    acc = jnp.dot(_rms(x_ref[...], g_ref[...]), w_ref[...], preferred_element_type=F32)
    kr = kr_ref[...].astype(F32)
    rope = (kr[:, 0:128] * cos_ref[...] + kr[:, 128:256] * sin_ref[...])[:, 0:MLA_ROPE_DIM].astype(BF16)
    for h in range(MLA_HEADS):
        k_ref[0, h, :, 0:MLA_NOPE_DIM] = acc[:, h * 256:h * 256 + 128].astype(BF16)
        k_ref[0, h, :, MLA_NOPE_DIM:MLA_QK_DIM] = rope
        v_ref[0, h] = acc[:, h * 256 + 128:(h + 1) * 256].astype(BF16)


def _mla_kv(zo, g, w, layer, cos, sin, batch, tm=512):
    n = zo.shape[0]
    s = n // batch
    nblk = s // tm
    return pl.pallas_call(
        _mla_kv_body,
        grid=(batch, nblk),
        in_specs=[pl.BlockSpec((tm, MLA_KV_LORA), lambda b, i: (b * nblk + i, ZT_KVA // MLA_KV_LORA)),
                  pl.BlockSpec((tm, 256), lambda b, i: (b * nblk + i, ZT_KR // 256)),
                  _layer_spec(g, layer), _layer_spec(w, layer),
                  pl.BlockSpec((tm, 128), lambda b, i: (i, 0)),
                  pl.BlockSpec((tm, 128), lambda b, i: (i, 0))],
        out_specs=[pl.BlockSpec((1, MLA_HEADS, tm, MLA_QK_DIM), lambda b, i: (b, 0, i, 0)),
                   pl.BlockSpec((1, MLA_HEADS, tm, MLA_V_DIM), lambda b, i: (b, 0, i, 0))],
        out_shape=[jax.ShapeDtypeStruct((batch, MLA_HEADS, s, MLA_QK_DIM), BF16),
                   jax.ShapeDtypeStruct((batch, MLA_HEADS, s, MLA_V_DIM), BF16)],
        compiler_params=_params("parallel", "parallel"),
        name="mla_kv_proj",
    )(zo, zo, g, w, cos, sin)


def _flash_body(qi_ref, kj_ref, q_ref, k_ref, v_ref, o_ref, m_ref, l_ref, acc_ref, *, tq):
    p = pl.program_id(1)
    i = qi_ref[p]
    j = kj_ref[p]
    heads, dv = acc_ref.shape[0], acc_ref.shape[2]

    @pl.when(j == 0)
    def _():
        m_ref[...] = jnp.full_like(m_ref, NEG_INF)
        l_ref[...] = jnp.zeros_like(l_ref)
        acc_ref[...] = jnp.zeros_like(acc_ref)

    def update(masked):
        for h in range(heads):
            s = lax.dot_general(q_ref[0, h], k_ref[0, h], (((1,), (1,)), ((), ())), preferred_element_type=F32)
            if masked:
                qpos = lax.broadcasted_iota(jnp.int32, (tq, tq), 0)
                kpos = lax.broadcasted_iota(jnp.int32, (tq, tq), 1)
                s = jnp.where(kpos <= qpos, s, NEG_INF)
            m_old = m_ref[h]
            m_new = jnp.maximum(m_old, jnp.max(s, axis=-1, keepdims=True))
            alpha = jnp.exp(m_old - m_new)
            e = jnp.exp(s - jnp.concatenate([m_new] * (tq // dv), axis=1))
            l_ref[h] = alpha * l_ref[h] + jnp.sum(e, axis=-1, keepdims=True)
            acc_ref[h] = alpha * acc_ref[h] + jnp.dot(e.astype(BF16), v_ref[0, h], preferred_element_type=F32)
            m_ref[h] = m_new

    @pl.when(j < i)
    def _():
        update(False)

    @pl.when(j == i)
    def _():
        update(True)
        for h in range(heads):
            o_ref[0, :, h * dv:(h + 1) * dv] = (acc_ref[h] / l_ref[h]).astype(o_ref.dtype)


def _flash(q, k, v, tq=512):
    b, h, s, dq = q.shape
    dv = v.shape[-1]
    nq = s // tq
    pairs = [(i, j) for i in range(nq) for j in range(i + 1)]
    qi = jnp.asarray([pr[0] for pr in pairs], jnp.int32)
    kj = jnp.asarray([pr[1] for pr in pairs], jnp.int32)
    return pl.pallas_call(
        functools.partial(_flash_body, tq=tq),
        grid_spec=pltpu.PrefetchScalarGridSpec(
            num_scalar_prefetch=2, grid=(b, len(pairs)),
            in_specs=[pl.BlockSpec((1, h, tq, dq), lambda bi, p, qi, kj: (bi, 0, qi[p], 0)),
                      pl.BlockSpec((1, h, tq, dq), lambda bi, p, qi, kj: (bi, 0, kj[p], 0)),
                      pl.BlockSpec((1, h, tq, dv), lambda bi, p, qi, kj: (bi, 0, kj[p], 0))],
            out_specs=pl.BlockSpec((1, tq, h * dv), lambda bi, p, qi, kj: (bi, qi[p], 0)),
            scratch_shapes=[pltpu.VMEM((h, tq, dv), F32), pltpu.VMEM((h, tq, dv), F32), pltpu.VMEM((h, tq, dv), F32)]),
        out_shape=jax.ShapeDtypeStruct((b, s, h * dv), BF16),
        compiler_params=_params("parallel", "arbitrary"),
        name="mla_flash",
    )(qi, kj, q, k, v)


def _layernorm(h, g, b):
    mu = jnp.mean(h, axis=-1, keepdims=True)
    hc = h - mu
    var = jnp.mean(hc * hc, axis=-1, keepdims=True)
    return hc * lax.rsqrt(var + LN_EPS) * g + b


def _store_token_major(ref, val):
    tiles = val.shape[1] // LANES
    for c in range(tiles):
        ref[pl.ds(c, val.shape[0], stride=tiles), :] = val[:, c * LANES:(c + 1) * LANES]


def _load_token_major(ref, rows):
    tiles = ref.shape[0] // rows
    return jnp.concatenate([ref[pl.ds(c, rows, stride=tiles), :] for c in range(tiles)], axis=1)


def _out_ln_body(ya_ref, yb_ref, yc_ref, w_ref, x_ref, g_ref, b_ref, o_ref, otm_ref, *, alpha):
    ka, kb = ya_ref.shape[1], yb_ref.shape[1]
    acc = (jnp.dot(ya_ref[...], w_ref[0:ka, :], preferred_element_type=F32)
           + jnp.dot(yb_ref[...], w_ref[ka:ka + kb, :], preferred_element_type=F32)
           + jnp.dot(yc_ref[...], w_ref[ka + kb:, :], preferred_element_type=F32))
    x1 = _layernorm(alpha * x_ref[...] + acc, g_ref[...], b_ref[...])
    o_ref[...] = x1
    _store_token_major(otm_ref, x1)


def _out_ln(ya, yb, yc, w, x, g, b, layer, alpha, tm=256):
    n, d = x.shape
    row = lambda c: pl.BlockSpec((tm, c), lambda i: (i, 0))
    return pl.pallas_call(
        functools.partial(_out_ln_body, alpha=alpha),
        grid=(n // tm,),
        in_specs=[row(ya.shape[1]), row(yb.shape[1]), row(yc.shape[1]),
                  _layer_spec(w, layer), row(d), _layer_spec(g, layer), _layer_spec(b, layer)],
        out_specs=[row(d), pl.BlockSpec((tm * (d // LANES), LANES), lambda i: (i, 0))],
        out_shape=[jax.ShapeDtypeStruct((n, d), F32), jax.ShapeDtypeStruct((n * (d // LANES), LANES), F32)],
        compiler_params=_params("parallel"),
        name="out_proj_ln",
    )(ya, yb, yc, w, x, g, b)


def _router_body(x_ref, w_ref, b_ref, idx_ref, gate_ref):
    logits = lax.dot_general(w_ref[...], x_ref[...], (((1,), (1,)), ((), ())),
                             precision=HIGHEST, preferred_element_type=F32)
    scores = _sigmoid(logits)
    sel = scores + b_ref[...]
    ng = N_EXPERT_GROUPS
    sel_j = [sel[j * ng:(j + 1) * ng] for j in range(EXPERTS_PER_GROUP)]
    sc_j = [scores[j * ng:(j + 1) * ng] for j in range(EXPERTS_PER_GROUP)]
    grp = None
    for p in range(EXPERTS_PER_GROUP):
        for q in range(p + 1, EXPERTS_PER_GROUP):
            pair = sel_j[p] + sel_j[q]
            grp = pair if grp is None else jnp.maximum(grp, pair)
    gi = lax.broadcasted_iota(jnp.int32, grp.shape, 0)
    gmax = jnp.max(grp, axis=0, keepdims=True)
    gbest = jnp.min(jnp.where(grp == gmax, gi, ng), axis=0, keepdims=True)
    pick = gi == gbest
    mem = [jnp.sum(jnp.where(pick, t, 0.0), axis=0, keepdims=True) for t in sel_j]
    msc = [jnp.sum(jnp.where(pick, t, 0.0), axis=0, keepdims=True) for t in sc_j]

    def first_argmax(vals, exclude):
        best = jnp.full_like(vals[0], -jnp.inf)
        bi = jnp.zeros(vals[0].shape, jnp.int32)
        bs = jnp.zeros_like(vals[0])
        for j in range(EXPERTS_PER_GROUP):
            ok = vals[j] > best
            if exclude is not None:
                ok = ok & (exclude != j)
            best = jnp.where(ok, vals[j], best)
            bi = jnp.where(ok, j, bi)
            bs = jnp.where(ok, msc[j], bs)
        return bi, bs

    i1, s1 = first_argmax(mem, None)
    i2, s2 = first_argmax(mem, i1)
    tot = s1 + s2
    base = gbest * EXPERTS_PER_GROUP
    idx_ref[0:1, :] = base + i1
    idx_ref[1:2, :] = base + i2
    gate_ref[0:1, :] = s1 / tot
    gate_ref[1:2, :] = s2 / tot


def _router(x, w_t, bias, tm=512):
    n, d = x.shape
    return pl.pallas_call(
        _router_body,
        grid=(n // tm,),
        in_specs=[pl.BlockSpec((tm, d), lambda i: (i, 0)),
                  pl.BlockSpec((N_EXPERTS, d), lambda i: (0, 0)),
                  pl.BlockSpec((N_EXPERTS, 1), lambda i: (0, 0))],
        out_specs=[pl.BlockSpec((TOP_K, tm), lambda i: (0, i)), pl.BlockSpec((TOP_K, tm), lambda i: (0, i))],
        out_shape=[jax.ShapeDtypeStruct((TOP_K, n), jnp.int32), jax.ShapeDtypeStruct((TOP_K, n), F32)],
        compiler_params=_params("parallel"),
        name="moe_router",
    )(x, w_t, bias)


def _expert_body(plan_ref, meta_ref, rows_ref, x_hbm, wg_hbm, wu_hbm, wd_hbm, out_hbm,
                 xg0_ref, xg1_ref, xg2_ref, yo0_ref, yo1_ref, yo2_ref, wfg_ref, wfu_ref, wfd_ref,
                 wgb_ref, wub_ref, wdb_ref, sem_in, sem_out, sem_w, *, tok_bits, n_blocks, layer):
    i = pl.program_id(0)
    n_act = meta_ref[0]
    xg = (xg0_ref, xg1_ref, xg2_ref)
    yo = (yo0_ref, yo1_ref, yo2_ref)
    last = n_blocks - 1
    tpr = wgb_ref.shape[0] // LANES

    def token_rows(ref, index):
        return ref.at[pl.ds(pl.multiple_of(index * tpr, tpr), tpr)]

    def start_gather(blk, par):
        for r in range(MOE_BLOCK):
            tok = rows_ref[blk * MOE_BLOCK + r] & ((1 << tok_bits) - 1)
            pltpu.make_async_copy(token_rows(x_hbm, tok), token_rows(xg[par], r), sem_in.at[par]).start()

    def wait_gather(par):
        for r in range(MOE_BLOCK):
            pltpu.make_async_copy(token_rows(x_hbm, 0), token_rows(xg[par], 0), sem_in.at[par]).wait()

    def start_scatter(blk, par):
        for r in range(MOE_BLOCK):
            row = rows_ref[blk * MOE_BLOCK + r] >> tok_bits
            pltpu.make_async_copy(token_rows(yo[par], r), token_rows(out_hbm, row), sem_out.at[par]).start()

    def wait_scatter(par):
        for r in range(MOE_BLOCK):
            pltpu.make_async_copy(token_rows(yo[par], 0), token_rows(out_hbm, 0), sem_out.at[par]).wait()

    def compute(par):
        xb = _load_token_major(xg[par], MOE_BLOCK).astype(BF16)
        gt = jnp.dot(xb, wgb_ref[...], preferred_element_type=F32)
        up = jnp.dot(xb, wub_ref[...], preferred_element_type=F32)
        hid = (gt * _sigmoid(gt) * up).astype(BF16)
        _store_token_major(yo[par], jnp.dot(hid, wdb_ref[...], preferred_element_type=F32))

    e_cur, first, wslot, e_next = (plan_ref[k * n_blocks + i] for k in range(4))

    def weight_copies(e, slot):
        pairs = ((wg_hbm, wfg_ref), (wu_hbm, wfu_ref), (wd_hbm, wfd_ref))
        return [pltpu.make_async_copy(w.at[layer, e], wf.at[slot], sem_w.at[slot]) for w, wf in pairs]

    @pl.when((i < n_act) & (first == 1))
    def _():
        @pl.when(i == 0)
        def _():
            for cp in weight_copies(e_cur, wslot):
                cp.start(priority=1)

        for cp in weight_copies(e_cur, wslot):
            cp.wait()

        @pl.when(e_next >= 0)
        def _():
            for cp in weight_copies(e_next, 1 - wslot):
                cp.start(priority=1)

        wgb_ref[...] = wfg_ref[wslot].astype(BF16)
        wub_ref[...] = wfu_ref[wslot].astype(BF16)
        wdb_ref[...] = wfd_ref[wslot].astype(BF16)

    @pl.when(i == 0)
    def _():
        start_gather(0, 0)
        start_gather(1, 1)
        yo0_ref[...] = jnp.zeros_like(yo0_ref)
        fill_rows = MOE_BLOCK * tpr
        spare0 = out_hbm.shape[0] - 2 * fill_rows
        fills = [pltpu.make_async_copy(yo0_ref, out_hbm.at[pl.ds(spare0 + h * fill_rows, fill_rows)], sem_out.at[0])
                 for h in range(2)]
        for cp in fills:
            cp.start()
        for cp in fills:
            cp.wait()
        wait_gather(0)
        start_gather(2, 2)
        compute(0)

    res = lax.rem(i, 3)
    for q in range(3):
        prv, nxt = (q + 2) % 3, (q + 1) % 3

        @pl.when((i > 0) & (i < n_act) & (res == q))
        def _(q=q, prv=prv):
            wait_gather(q)

            @pl.when(i >= 3)
            def _():
                wait_scatter(q)

            start_gather(jnp.minimum(i + 2, last), prv)
            start_scatter(i - 1, prv)
            compute(q)

    for q in range(3):
        prv, nxt = (q + 2) % 3, (q + 1) % 3

        @pl.when((i == n_act - 1) & (res == q))
        def _(q=q, prv=prv, nxt=nxt):
            start_scatter(i, q)

            @pl.when(i >= 2)
            def _():
                wait_scatter(nxt)

            wait_scatter(prv)
            wait_scatter(q)
            wait_gather(nxt)
            wait_gather(prv)


def _experts(plan, meta, rows, x_tm, wg, wu, wd, layer):
    d = wg.shape[2]
    tpr = d // LANES
    n = x_tm.shape[0] // tpr
    nb = plan.shape[0] // 4
    buf = pltpu.VMEM((MOE_BLOCK * tpr, LANES), F32)
    hbm = pl.BlockSpec(memory_space=pl.ANY)
    return pl.pallas_call(
        functools.partial(_expert_body, tok_bits=_tok_bits(n), n_blocks=nb, layer=layer),
        grid_spec=pltpu.PrefetchScalarGridSpec(
            num_scalar_prefetch=3, grid=(nb,),
            in_specs=[hbm, hbm, hbm, hbm],
            out_specs=hbm,
            scratch_shapes=[buf] * 6 + [
                pltpu.VMEM((2, d, D_EXPERT), F32), pltpu.VMEM((2, d, D_EXPERT), F32), pltpu.VMEM((2, D_EXPERT, d), F32),
                pltpu.VMEM((d, D_EXPERT), BF16), pltpu.VMEM((d, D_EXPERT), BF16), pltpu.VMEM((D_EXPERT, d), BF16),
                pltpu.SemaphoreType.DMA((3,)), pltpu.SemaphoreType.DMA((3,)), pltpu.SemaphoreType.DMA((2,))]),
        out_shape=jax.ShapeDtypeStruct(((TOP_K * n + 2 * MOE_BLOCK) * tpr, LANES), F32),
        compiler_params=_params("arbitrary"),
        name="moe_experts",
    )(plan, meta, rows, x_tm, wg, wu, wd)


def _tok_bits(n):
    return max((n - 1).bit_length(), 1)


def _dispatch(idx):
    n = idx.shape[1]
    a_tot = TOP_K * n
    e_flat = idx.reshape(a_tot)
    onehot = (e_flat[:, None] == jnp.arange(N_EXPERTS, dtype=jnp.int32)[None, :]).astype(F32)
    chunk = 128
    oh = onehot.reshape(a_tot // chunk, chunk, N_EXPERTS)
    tri = (jnp.arange(chunk)[None, :] <= jnp.arange(chunk)[:, None]).astype(F32)
    within = jnp.einsum('ts,csn->ctn', tri, oh)
    tot = within[:, -1, :]
    csum = (within + (jnp.cumsum(tot, axis=0) - tot)[:, None, :]).reshape(a_tot, N_EXPERTS)
    rank = jnp.sum(onehot * csum, axis=1).astype(jnp.int32) - 1
    counts = csum[-1].astype(jnp.int32)
    padded = (counts + MOE_BLOCK - 1) // MOE_BLOCK * MOE_BLOCK
    pad_end = jnp.cumsum(padded)
    pad_start = pad_end - padded
    dest = pad_start[e_flat] + rank
    nb = (a_tot + N_EXPERTS * (MOE_BLOCK - 1) + MOE_BLOCK - 1) // MOE_BLOCK
    p_rows = nb * MOE_BLOCK
    bits = _tok_bits(n)
    a_ids = jnp.arange(a_tot, dtype=jnp.int32)
    pr = jnp.arange(p_rows, dtype=jnp.int32)
    spare = a_tot + ((pr // MOE_BLOCK) % 2) * MOE_BLOCK + pr % MOE_BLOCK
    rows = (spare << bits).at[dest].set((a_ids % n) | (a_ids << bits))
    blk_start = jnp.arange(nb, dtype=jnp.int32) * MOE_BLOCK
    blk_e = jnp.sum((pad_end[None, :] <= blk_start[:, None]).astype(jnp.int32), axis=1)
    blk_e = jnp.minimum(blk_e, N_EXPERTS - 1)
    n_act = jnp.minimum(pad_end[-1] // MOE_BLOCK + 1, nb).astype(jnp.int32)
    steps = jnp.arange(nb, dtype=jnp.int32)
    first = (steps < n_act) & ((steps == 0) | (blk_e != jnp.roll(blk_e, 1)))
    wslot = (jnp.cumsum(first.astype(jnp.int32)) - 1) & 1
    next_first = lax.cummin(jnp.where(first, steps, nb), reverse=True)
    next_first = jnp.concatenate([next_first[1:], jnp.full((1,), nb, jnp.int32)])
    e_next = jnp.where(next_first < nb, blk_e[jnp.minimum(next_first, nb - 1)], -1)
    plan = jnp.concatenate([blk_e, first.astype(jnp.int32), wslot, e_next]).astype(jnp.int32)
    return plan, n_act.reshape(1), rows


def _final_body(x_ref, ya_ref, yb_ref, ga_ref, gb_ref, g_ref, b_ref, p_ref, wp_ref, wg_ref, o_ref, ob_ref, *, alpha):
    rows = x_ref.shape[0]
    ffn = _load_token_major(ya_ref, rows) * ga_ref[...] + _load_token_major(yb_ref, rows) * gb_ref[...]
    x2 = _layernorm(alpha * x_ref[...] + ffn, g_ref[...], b_ref[...])
    gate = jnp.dot(x2.astype(BF16), wg_ref[...], preferred_element_type=F32)
    proj = jnp.dot(p_ref[...].astype(BF16), wp_ref[...], preferred_element_type=F32)
    out = x2 + _sigmoid(gate) * proj
    o_ref[...] = out
    ob_ref[...] = out.astype(BF16)


def _final(x, y2, gates, g, b, p, wp, wg, layer, alpha, tm=256):
    n, d = x.shape
    nblk = n // tm
    row = pl.BlockSpec((tm, d), lambda i: (i, 0))
    return pl.pallas_call(
        functools.partial(_final_body, alpha=alpha),
        grid=(nblk,),
        in_specs=[row, pl.BlockSpec((tm * (d // LANES), LANES), lambda i: (i, 0)),
                  pl.BlockSpec((tm * (d // LANES), LANES), lambda i: (i + nblk, 0)),
                  pl.BlockSpec((tm, 1), lambda i: (i, 0)), pl.BlockSpec((tm, 1), lambda i: (i + nblk, 0)),
                  _layer_spec(g, layer), _layer_spec(b, layer),
                  pl.BlockSpec((None, tm, PLE_DIM), lambda i: (layer, i, 0)),
                  _layer_spec(wp, layer), _layer_spec(wg, layer)],
        out_specs=[row, row],
        out_shape=[jax.ShapeDtypeStruct((n, d), F32), jax.ShapeDtypeStruct((n, d), BF16)],
        compiler_params=_params("parallel"),
        name="moe_combine_ln_ple",
    )(x, y2, y2, gates, gates, g, b, p, wp, wg)


def _rot_cols(w):
    half = MLA_ROPE_DIM // 2
    return jnp.concatenate([-w[..., half:], w[..., :half]], axis=-1)


def _pad_lanes(w, width=128):
    return jnp.pad(w, [(0, 0)] * (w.ndim - 1) + [(0, width - w.shape[-1])])


def _rope_tables(s):
    half = MLA_ROPE_DIM // 2
    inv = ROPE_THETA ** (-jnp.arange(half, dtype=F32) / half)
    ang = jnp.arange(s, dtype=F32)[:, None] * inv[None, :]
    cos = jnp.concatenate([jnp.cos(ang), jnp.cos(ang)], axis=-1)
    sin = jnp.concatenate([jnp.sin(ang), jnp.sin(ang)], axis=-1)
    return _pad_lanes(cos), _pad_lanes(sin)


def kernel(x, p, w_in, rwkv_mu, rwkv_w0, rwkv_w_up, rwkv_a0, rwkv_a_up, rwkv_g_up, rwkv_k_k, rwkv_k_a, rwkv_r_k, rwkv_gn_g, rwkv_gn_b, mla_qa_g, mla_w_uq, mla_kva_g, mla_w_ukv, w_out, ln1_g, ln1_b, router_w, router_b, moe_w_gate, moe_w_up, moe_w_down, ln2_g, ln2_b, ple_w_proj, ple_w_gate):
    batch, s, d = x.shape
    depth = w_in.shape[0]
    n = batch * s
    alpha = (2 * depth) ** 0.25
    assert s % (DIL_PATTERNS[-1][1] * DIL_BLOCK) == 0 and s % 512 == 0

    cos, sin = _rope_tables(s)
    slopes = jnp.exp2(-ALIBI_MAX_BIAS * jnp.arange(1, 13, dtype=F32) / 12)
    perm = np.array([4 * g + j for j in range(EXPERTS_PER_GROUP) for g in range(N_EXPERT_GROUPS)])
    router_wt = router_w.T[perm]
    router_bt = router_b[perm].reshape(N_EXPERTS, 1)

    row = lambda t: t.reshape(depth, 1, -1)
    hpar = lambda t: t.reshape(depth, RWKV_PAIRS, 1, PAIR_LANES)
    kva0 = RWKV_IN_DIM + ZO_DIM
    w_kr = w_in[:, :, kva0 + MLA_KV_LORA:]
    w_tail = jnp.concatenate([w_in[:, :, kva0:kva0 + MLA_KV_LORA], _pad_lanes(w_kr), _pad_lanes(_rot_cols(w_kr))],
                             axis=2).astype(BF16)
    w_up_pad = jnp.pad(rwkv_w_up, ((0, 0), (0, 64), (0, 0)))
    w_up_hi = w_up_pad.astype(BF16)
    w_up_parts = jnp.stack([w_up_hi, (w_up_pad - w_up_hi.astype(F32)).astype(BF16)], axis=1)
    a_up_pad = jnp.pad(rwkv_a_up, ((0, 0), (64, 0), (0, 0))).astype(BF16)
    rwkv_params = (row(rwkv_mu), row(rwkv_w0), row(rwkv_a0), row(rwkv_k_k), row(rwkv_k_a), w_up_parts, a_up_pad,
                   rwkv_g_up.astype(BF16), hpar(rwkv_r_k), hpar(rwkv_gn_g), hpar(rwkv_gn_b))
    wq = mla_w_uq.reshape(depth, MLA_Q_LORA, MLA_HEADS, MLA_QK_DIM)
    wq_rope = wq[..., MLA_NOPE_DIM:]
    wq_all = jnp.concatenate(
        [wq[..., :MLA_NOPE_DIM].reshape(depth, MLA_Q_LORA, -1),
         jnp.concatenate([_pad_lanes(wq_rope), _pad_lanes(_rot_cols(wq_rope))], axis=-1).reshape(depth, MLA_Q_LORA, -1)],
        axis=2).astype(BF16)
    w_ukv_b, w_out_b = mla_w_ukv.astype(BF16), w_out.astype(BF16)
    ple_proj_b, ple_gate_b = ple_w_proj.astype(BF16), ple_w_gate.astype(BF16)
    qa_g, kva_g = row(mla_qa_g), row(mla_kva_g)
    ln1 = (row(ln1_g), row(ln1_b))
    ln2 = (row(ln2_g), row(ln2_b))
    p_rows = p.reshape(depth, n, PLE_DIM)

    xf = x.reshape(n, d)
    xb = xf.astype(BF16)
    for li in range(depth):
        z_r = _in_proj(xb, w_in, li, 0, RWKV_IN_DIM, F32)
        z_o = _in_proj(xb, w_in, li, RWKV_IN_DIM, ZO_DIM, BF16)
        z_t = _matmul(xb, w_tail, li, BF16, 1024, ZT_DIM)

        y_a = _rwkv_mix(z_r.reshape(batch, s, RWKV_IN_DIM), li, *rwkv_params).reshape(n, RWKV_DIM)

        prior = _dilated_group(slopes, z_o, 0, batch, s) + _dilated_group(slopes, z_o, 1, batch, s)
        y_b = _dilated_group(slopes, z_o, 2, batch, s, prior=prior)

        q_c = _mla_q(z_o, qa_g, wq_all, li, cos, sin, batch)
        k_c, v_c = _mla_kv(z_t, kva_g, w_ukv_b, li, cos, sin, batch)
        y_c = _flash(q_c, k_c, v_c).reshape(n, MLA_OUT_DIM)

        x1, x1_tm = _out_ln(y_a, y_b, y_c, w_out_b, xf, *ln1, li, alpha)

        idx, gate = _router(x1, router_wt, router_bt)
        plan, meta, rows = _dispatch(idx)
        y2 = _experts(plan, meta, rows, x1_tm, moe_w_gate, moe_w_up, moe_w_down, li)

        xf, xb = _final(x1, y2, gate.reshape(TOP_K * n, 1), *ln2, p_rows, ple_proj_b, ple_gate_b, li, alpha)
    return xf.reshape(batch, s, d)
```

```python
import functools
import math

import numpy as np
import jax
import jax.numpy as jnp
from jax import lax
from jax.experimental import pallas as pl
from jax.experimental.pallas import tpu as pltpu

F32 = jnp.float32
BF16 = jnp.bfloat16
HIGHEST = lax.Precision.HIGHEST

PLE_DIM = 256
RWKV_HEADS = 12
RWKV_HEAD_DIM = 64
RWKV_DIM = RWKV_HEADS * RWKV_HEAD_DIM
RWKV_LORA_PAD = 128
RWKV_GATE_LORA = 128
RWKV_IN_DIM = 3 * RWKV_DIM + RWKV_LORA_PAD + RWKV_GATE_LORA
RWKV_GN_EPS = 64e-5
RWKV_CHUNK = 64
DIL_PATTERNS = ((128, 1), (512, 4), (2048, 16))
DIL_GROUPS = 3
DIL_HEADS_PER_GROUP = 4
DIL_HEAD_DIM = 128
DIL_QKV_DIM = DIL_GROUPS * DIL_HEADS_PER_GROUP * DIL_HEAD_DIM
DIL_OUT_DIM = DIL_HEADS_PER_GROUP * DIL_HEAD_DIM
DIL_BLOCK = 128
ALIBI_MAX_BIAS = 8.0
MLA_HEADS = 6
MLA_NOPE_DIM = 128
MLA_ROPE_DIM = 64
MLA_V_DIM = 128
MLA_Q_LORA = 512
MLA_KV_LORA = 256
MLA_QK_DIM = MLA_NOPE_DIM + MLA_ROPE_DIM
MLA_OUT_DIM = MLA_HEADS * MLA_V_DIM
ROPE_THETA = 10000.0
N_EXPERTS = 32
N_EXPERT_GROUPS = 8
EXPERTS_PER_GROUP = 4
TOP_K = 2
D_EXPERT = 512
MOE_BLOCK = 128
LANES = 128
LN_EPS = 1e-5
RMS_EPS = 1e-6
NEG_INF = -1e30

ZO_DQ = 0
ZO_DK = DIL_QKV_DIM
ZO_DV = 2 * DIL_QKV_DIM
ZO_QA = 3 * DIL_QKV_DIM
ZO_DIM = ZO_QA + MLA_Q_LORA
ZT_KVA = 0
ZT_KR = MLA_KV_LORA
ZT_DIM = ZT_KR + 256

V7X_VMEM_LIMIT_BYTES = 48 * 1024 * 1024


def _params(*sem):
    return pltpu.CompilerParams(dimension_semantics=sem, vmem_limit_bytes=V7X_VMEM_LIMIT_BYTES)


def _sigmoid(x):
    return 1.0 / (1.0 + jnp.exp(-x))


def _mm_body(x_ref, w_ref, o_ref):
    o_ref[...] = jnp.dot(x_ref[...], w_ref[...], preferred_element_type=F32).astype(o_ref.dtype)


def _layer_spec(arr, layer):
    zeros = (0,) * (arr.ndim - 1)
    return pl.BlockSpec((None,) + arr.shape[1:], lambda *_: (layer,) + zeros)


def _matmul(x, w_all, layer, out_dtype, tm, tn):
    m, k = x.shape
    n = w_all.shape[2]
    return pl.pallas_call(
        _mm_body,
        grid=(m // tm, n // tn),
        in_specs=[pl.BlockSpec((tm, k), lambda i, j: (i, 0)),
                  pl.BlockSpec((None, k, tn), lambda i, j: (layer, 0, j))],
        out_specs=pl.BlockSpec((tm, tn), lambda i, j: (i, j)),
        out_shape=jax.ShapeDtypeStruct((m, n), out_dtype),
        compiler_params=_params("parallel", "arbitrary"),
        name="in_proj_tail",
    )(x, w_all)


def _in_proj_body(x_ref, w_ref, o_ref, wb_ref):
    @pl.when(pl.program_id(1) == 0)
    def _():
        wb_ref[...] = w_ref[0].astype(BF16)

    o_ref[...] = jnp.dot(x_ref[...], wb_ref[...], preferred_element_type=F32).astype(o_ref.dtype)


def _in_proj(x, w_all, layer, col0, ncols, out_dtype, tm=1024, tn=512):
    m, k = x.shape
    assert col0 % tn == 0 and ncols % tn == 0 and col0 + ncols <= w_all.shape[2]
    cb0 = col0 // tn
    return pl.pallas_call(
        _in_proj_body,
        grid=(ncols // tn, m // tm),
        in_specs=[pl.BlockSpec((tm, k), lambda j, i: (i, 0)),
                  pl.BlockSpec((1, k, tn), lambda j, i: (layer, 0, cb0 + j))],
        out_specs=pl.BlockSpec((tm, tn), lambda j, i: (i, j)),
        out_shape=jax.ShapeDtypeStruct((m, ncols), out_dtype),
        scratch_shapes=[pltpu.VMEM((k, tn), BF16)],
        compiler_params=_params("parallel", "arbitrary"),
        name="in_proj",
    )(x, w_all)


def _bf16_parts(x, n):
    parts = []
    for _ in range(n):
        part = x.astype(BF16)
        parts.append(part)
        x = x - part.astype(F32)
    return parts


def _bdot(a, b, dims):
    return lax.dot_general(a.astype(BF16), b.astype(BF16), (dims, ((0,), (0,))), preferred_element_type=F32)


def _bnt(a, b):
    return _bdot(a, b, ((2,), (2,)))


def _bnn(a, b):
    return _bdot(a, b, ((2,), (1,)))


def _btn(a, b):
    return _bdot(a, b, ((1,), (1,)))


RWKV_PAIRS = RWKV_HEADS // 2
PAIR_LANES = 2 * RWKV_HEAD_DIM


def _rwkv_body(z_ref, mu_ref, w0_ref, a0_ref, kk_ref, ka_ref, wup_ref, aup_ref, gup_ref, rk_ref, gng_ref, gnb_ref,
               y_ref, st_ref, carry_ref):
    @pl.when(pl.program_id(0) == 0)
    def _():
        st_ref[...] = jnp.zeros_like(st_ref)
        carry_ref[...] = jnp.zeros_like(carry_ref)

    for bi in range(z_ref.shape[0]):
        _rwkv_chunk(bi, z_ref, mu_ref, w0_ref, a0_ref, kk_ref, ka_ref, wup_ref, aup_ref, gup_ref, rk_ref, gng_ref,
                    gnb_ref, y_ref, st_ref, carry_ref)


def _rwkv_chunk(bi, z_ref, mu_ref, w0_ref, a0_ref, kk_ref, ka_ref, wup_ref, aup_ref, gup_ref, rk_ref, gng_ref, gnb_ref,
                y_ref, st_ref, carry_ref):
    t, d, np_, pl_ = RWKV_CHUNK, RWKV_DIM, RWKV_PAIRS, PAIR_LANES

    z = z_ref[bi]
    row = lax.broadcasted_iota(jnp.int32, z.shape, 0)
    zprev = jnp.where(row == 0, carry_ref[bi], pltpu.roll(z, 1, axis=0))
    carry_ref[bi] = z[t - 1:t, :]
    zs = z + (zprev - z) * mu_ref[...]
    r_w, k_w, v_w = zs[:, 0:d], zs[:, d:2 * d], zs[:, 2 * d:3 * d]
    lora = zs[:, 3 * d:3 * d + RWKV_LORA_PAD]
    gd = zs[:, 3 * d + RWKV_LORA_PAD:]
    th_hi, th_lo = _bf16_parts(jnp.tanh(lora), 2)
    u = (w0_ref[...] + jnp.dot(th_hi, wup_ref[0], preferred_element_type=F32)
         + jnp.dot(th_hi, wup_ref[1], preferred_element_type=F32)
         + jnp.dot(th_lo, wup_ref[0], preferred_element_type=F32))
    softplus = jnp.maximum(-u, 0.0) + jnp.log(1.0 + jnp.exp(-jnp.abs(u)))
    lw_w = -jnp.exp(-softplus - 0.5)
    a_w = _sigmoid(a0_ref[...] + jnp.dot(lora.astype(BF16), aup_ref[...], preferred_element_type=F32))
    g_w = jnp.dot(_sigmoid(gd).astype(BF16), gup_ref[...], preferred_element_type=F32)
    kmod_w = k_w * (1.0 + (a_w - 1.0) * ka_ref[...])
    kk_w = k_w * kk_ref[...]
    ti = lax.broadcasted_iota(jnp.int32, (t, t), 0)
    si = lax.broadcasted_iota(jnp.int32, (t, t), 1)
    incl = si <= ti
    strict = si < ti
    tri = incl.astype(BF16)
    lp_w = sum(jnp.dot(tri, part, preferred_element_type=F32) for part in _bf16_parts(lw_w, 3))

    pairs = lambda x: jnp.stack([x[:, p * pl_:(p + 1) * pl_] for p in range(np_)], axis=0)
    r, k, v, a, lw, lp, g = (pairs(x) for x in (r_w, kmod_w, v_w, a_w, lw_w, lp_w, g_w))
    li = lax.broadcasted_iota(jnp.int32, (pl_, pl_), 0)
    lj = lax.broadcasted_iota(jnp.int32, (pl_, pl_), 1)
    same_head = (li // RWKV_HEAD_DIM) == (lj // RWKV_HEAD_DIM)
    head_ones = same_head.astype(BF16)

    def head_sum(x):
        parts = _bf16_parts(x.reshape(np_ * t, pl_), 2)
        return sum(jnp.dot(part, head_ones, preferred_element_type=F32) for part in parts).reshape(np_, t, pl_)

    kk = pairs(kk_w)
    kk = kk / jnp.maximum(jnp.sqrt(head_sum(kk * kk)), 1e-12)
    lp_end = lp[:, t - 1:t, :]
    p_inv = jnp.exp(-lp)
    at = -kk * jnp.exp(lp - lw)
    bt = kk * a * p_inv
    kt = k * p_inv
    rt = r * jnp.exp(lp)
    to_end = jnp.exp(lp_end - lp)
    b_end = kk * a * to_end
    k_end = k * to_end

    lane = lax.broadcasted_iota(jnp.int32, (1, 1, pl_), 2)
    m0 = (lane < RWKV_HEAD_DIM).astype(F32)
    msk = jnp.concatenate([jnp.broadcast_to(m0, (np_, 1, pl_)), jnp.broadcast_to(1.0 - m0, (np_, 1, pl_))], axis=0)
    dup = lambda x: jnp.concatenate([x, x], axis=0)
    fold = lambda x: x[:np_] + x[np_:]
    lhs_a = dup(at) * msk
    lhs_r = dup(rt) * msk
    v2 = dup(v) * msk
    ar = jnp.concatenate([lhs_a, lhs_r], axis=1)
    x_b = _bnt(ar, dup(bt))
    x_k = _bnt(ar, dup(kt))
    a_ab = jnp.where(strict[None], x_b[:, :t], 0.0)
    a_rb = jnp.where(incl[None], x_b[:, t:], 0.0)
    a_ak = jnp.where(strict[None], x_k[:, :t], 0.0)
    a_rk = jnp.where(incl[None], x_k[:, t:], 0.0)

    sub = 16
    same_blk = ((ti // sub) == (si // sub))[None]
    eye = (ti == si).astype(F32)[None]
    ld = jnp.where(same_blk, a_ab, 0.0)
    lo = a_ab - ld
    dinv = eye + ld
    pw = ld
    for _ in range(3):
        pw = _bnn(pw, pw)
        dinv = dinv + _bnn(dinv, pw)
    n1 = _bnn(dinv, lo)
    n2 = _bnn(n1, n1)
    tinv = eye + n1 + n2 + _bnn(n1, n2)
    tinv = _bnn(tinv, dinv)

    akv = _bnn(a_ak, v2)
    wu = _bnn(tinv, jnp.concatenate([lhs_a, akv], axis=2))
    qy = _bnn(a_rb, wu)
    q = fold(lhs_r + qy[:, :, :pl_])
    y0 = fold(qy[:, :, pl_:] + _bnn(a_rk, v2))
    wt = fold(wu[:, :, :pl_])
    u0 = fold(wu[:, :, pl_:])

    s0 = st_ref[bi]
    y = _bnn(q, s0) + y0
    diag_end = jnp.where((li == lj)[None], jnp.exp(lp_end), 0.0)
    m_t = diag_end + jnp.where(same_head[None], _btn(b_end, wt), 0.0)
    c_t = jnp.where(same_head[None], _btn(b_end, u0) + _btn(k_end, v), 0.0)
    st_ref[bi] = _bnn(m_t, s0) + c_t

    inv_e = 1.0 / RWKV_HEAD_DIM
    yc = y - head_sum(y) * inv_e
    yv = head_sum(yc * yc) * inv_e
    yn = yc * lax.rsqrt(yv + RWKV_GN_EPS) * gng_ref[...] + gnb_ref[...]
    out = (yn + head_sum(r * k * rk_ref[...]) * v) * g
    for p in range(np_):
        y_ref[bi, :, p * pl_:(p + 1) * pl_] = out[p].astype(y_ref.dtype)


def _rwkv_mix(z, layer, *params):
    b, s, zin = z.shape
    d, t = RWKV_DIM, RWKV_CHUNK
    return pl.pallas_call(
        _rwkv_body,
        grid=(s // t,),
        in_specs=[pl.BlockSpec((b, t, zin), lambda c: (0, c, 0))] + [_layer_spec(a, layer) for a in params],
        out_specs=pl.BlockSpec((b, t, d), lambda c: (0, c, 0)),
        out_shape=jax.ShapeDtypeStruct((b, s, d), BF16),
        scratch_shapes=[pltpu.VMEM((b, RWKV_PAIRS, PAIR_LANES, PAIR_LANES), F32), pltpu.VMEM((b, 1, zin), F32)],
        compiler_params=_params("arbitrary"),
        name="rwkv_mix",
    )(z, *params)


def _dil_body(slopes_ref, q_ref, kc_ref, kp_ref, vc_ref, vp_ref, *rest, group, dil, n_sub, n_heads, merge):
    nblk = pl.program_id(3)
    scale = DIL_HEAD_DIM ** -0.5
    nt = (((1,), (1,)), ((), ()))
    qi = lax.broadcasted_iota(jnp.int32, (DIL_BLOCK, DIL_BLOCK), 0)
    ki = lax.broadcasted_iota(jnp.int32, (DIL_BLOCK, DIL_BLOCK), 1)
    rel_c = qi - ki
    if merge:
        o0_ref, l0_ref, o1_ref, l1_ref, y_ref = rest
    else:
        o_ref, lse_ref = rest
    for hu in range(n_heads * n_sub):
        hh, u = divmod(hu, n_sub)
        rows = slice(u * DIL_BLOCK, (u + 1) * DIL_BLOCK)
        cols = slice(hh * DIL_HEAD_DIM, (hh + 1) * DIL_HEAD_DIM)
        head = pl.program_id(2) * n_heads + hh
        bias = slopes_ref[group * DIL_HEADS_PER_GROUP + head] * float(dil)
        dist_c = bias * rel_c.astype(F32)
        dist_p = bias * (rel_c + DIL_BLOCK).astype(F32)
        q = q_ref[rows, cols]
        if u == 0:
            k_prev, v_prev = kp_ref[:, cols], vp_ref[:, cols]
            prev_lim = jnp.where(nblk == 0, -2 * DIL_BLOCK, 0)
        else:
            prows = slice((u - 1) * DIL_BLOCK, u * DIL_BLOCK)
            k_prev, v_prev = kc_ref[prows, cols], vc_ref[prows, cols]
            prev_lim = 0
        s_c = lax.dot_general(q, kc_ref[rows, cols], nt, preferred_element_type=F32) * scale
        s_p = lax.dot_general(q, k_prev, nt, preferred_element_type=F32) * scale
        s_c = jnp.where(rel_c >= 0, s_c - dist_c, NEG_INF)
        s_p = jnp.where(rel_c <= prev_lim, s_p - dist_p, NEG_INF)
        m = jnp.maximum(jnp.max(s_c, axis=-1, keepdims=True), jnp.max(s_p, axis=-1, keepdims=True))
        e_c = jnp.exp(s_c - m)
        e_p = jnp.exp(s_p - m)
        den = jnp.sum(e_c, axis=-1, keepdims=True) + jnp.sum(e_p, axis=-1, keepdims=True)
        acc = (jnp.dot(e_c.astype(BF16), vc_ref[rows, cols], preferred_element_type=F32)
               + jnp.dot(e_p.astype(BF16), v_prev, preferred_element_type=F32))
        o = acc / den
        lse = jnp.broadcast_to(m + jnp.log(den), (DIL_BLOCK, DIL_HEAD_DIM))
        if merge:
            l0, l1 = l0_ref[rows, cols], l1_ref[rows, cols]
            top = jnp.maximum(jnp.maximum(l0, l1), lse)
            w0, w1, w2 = jnp.exp(l0 - top), jnp.exp(l1 - top), jnp.exp(lse - top)
            y = (w0 * o0_ref[rows, cols] + w1 * o1_ref[rows, cols] + w2 * o) / (w0 + w1 + w2)
            y_ref[rows, cols] = y.astype(y_ref.dtype)
        else:
            o_ref[rows, cols] = o.astype(o_ref.dtype)
            lse_ref[rows, cols] = lse


def _dilated_group(slopes, zo, group, batch, s, prior=None):
    n = batch * s
    dil = DIL_PATTERNS[group][1]
    nb = s // dil // DIL_BLOCK
    n_sub = min(nb, 8)
    n_heads = min(8 // n_sub, DIL_HEADS_PER_GROUP)
    steps = nb // n_sub
    ocols = DIL_OUT_DIM // DIL_HEAD_DIM // n_heads
    rows = n_sub * DIL_BLOCK
    width = n_heads * DIL_HEAD_DIM
    gcol = lambda col0: col0 + group * DIL_OUT_DIM
    if dil == 1:
        zv, qkv_cols = zo, (gcol(ZO_DQ), gcol(ZO_DK), gcol(ZO_DV))
    else:
        zv = jnp.concatenate([zo[:, gcol(c):gcol(c) + DIL_OUT_DIM] for c in (ZO_DQ, ZO_DK, ZO_DV)], axis=1)
        zv, qkv_cols = zv.reshape(n // dil, dil * 3 * DIL_OUT_DIM), (0, DIL_OUT_DIM, 2 * DIL_OUT_DIM)
    zcols = zv.shape[1] // dil // width

    def cur(col0):
        c = col0 // width
        return pl.BlockSpec((rows, width), lambda b, r, h, i, sl: (b * steps + i, r * zcols + c + h))

    def prev(col0):
        c = col0 // width
        return pl.BlockSpec((DIL_BLOCK, width),
                            lambda b, r, h, i, sl: (jnp.maximum(b * nb + i * n_sub - 1, 0), r * zcols + c + h))

    ospec = pl.BlockSpec((rows, width), lambda b, r, h, i, sl: (b * steps + i, r * ocols + h))
    oshape = lambda dt: jax.ShapeDtypeStruct((n // dil, dil * DIL_OUT_DIM), dt)
    merge = prior is not None
    extra = [t.reshape(n // dil, dil * DIL_OUT_DIM) for t in prior] if merge else []
    qc, kc, vc = qkv_cols
    out = pl.pallas_call(
        functools.partial(_dil_body, group=group, dil=dil, n_sub=n_sub, n_heads=n_heads, merge=merge),
        grid_spec=pltpu.PrefetchScalarGridSpec(
            num_scalar_prefetch=1, grid=(batch, dil, DIL_HEADS_PER_GROUP // n_heads, steps),
            in_specs=[cur(qc), cur(kc), prev(kc), cur(vc), prev(vc)] + [ospec] * len(extra),
            out_specs=ospec if merge else [ospec, ospec]),
        out_shape=oshape(BF16) if merge else [oshape(BF16), oshape(F32)],
        compiler_params=_params("parallel", "parallel", "parallel", "arbitrary"),
        name="dilated_attn_g%d" % group,
    )(slopes, zv, zv, zv, zv, zv, *extra)
    if merge:
        return out.reshape(n, DIL_OUT_DIM)
    return [t.reshape(n, DIL_OUT_DIM) for t in out]


def _rms(x_bf16, g):
    x = x_bf16.astype(F32)
    return (x * lax.rsqrt(jnp.mean(x * x, axis=-1, keepdims=True) + RMS_EPS) * g).astype(BF16)


def _mla_q_body(x_ref, g_ref, w_ref, cos_ref, sin_ref, q_ref):
    acc = jnp.dot(_rms(x_ref[...], g_ref[...]), w_ref[...], preferred_element_type=F32)
    scale = MLA_QK_DIM ** -0.5
    cos, sin = cos_ref[...], sin_ref[...]
    nope_w = MLA_HEADS * MLA_NOPE_DIM
    for h in range(MLA_HEADS):
        q_ref[0, h, :, 0:MLA_NOPE_DIM] = (acc[:, h * 128:(h + 1) * 128] * scale).astype(BF16)
        base = nope_w + h * 256
        rope = acc[:, base:base + 128] * cos + acc[:, base + 128:base + 256] * sin
        q_ref[0, h, :, MLA_NOPE_DIM:MLA_QK_DIM] = (rope[:, 0:MLA_ROPE_DIM] * scale).astype(BF16)


def _mla_q(zo, g, w, layer, cos, sin, batch, tm=512):
    n = zo.shape[0]
    s = n // batch
    nblk = s // tm
    return pl.pallas_call(
        _mla_q_body,
        grid=(batch, nblk),
        in_specs=[pl.BlockSpec((tm, MLA_Q_LORA), lambda b, i: (b * nblk + i, ZO_QA // MLA_Q_LORA)),
                  _layer_spec(g, layer), _layer_spec(w, layer),
                  pl.BlockSpec((tm, 128), lambda b, i: (i, 0)),
                  pl.BlockSpec((tm, 128), lambda b, i: (i, 0))],
        out_specs=pl.BlockSpec((1, MLA_HEADS, tm, MLA_QK_DIM), lambda b, i: (b, 0, i, 0)),
        out_shape=jax.ShapeDtypeStruct((batch, MLA_HEADS, s, MLA_QK_DIM), BF16),
        compiler_params=_params("parallel", "parallel"),
        name="mla_q_proj",
    )(zo, g, w, cos, sin)


def _mla_kv_body(x_ref, kr_ref, g_ref, w_ref, cos_ref, sin_ref, k_ref, v_ref):
    acc = jnp.dot(_rms(x_ref[...], g_ref[...]), w_ref[...], preferred_element_type=F32)
    kr = kr_ref[...].astype(F32)
    rope = (kr[:, 0:128] * cos_ref[...] + kr[:, 128:256] * sin_ref[...])[:, 0:MLA_ROPE_DIM].astype(BF16)
    for h in range(MLA_HEADS):
        k_ref[0, h, :, 0:MLA_NOPE_DIM] = acc[:, h * 256:h * 256 + 128].astype(BF16)
        k_ref[0, h, :, MLA_NOPE_DIM:MLA_QK_DIM] = rope
        v_ref[0, h] = acc[:, h * 256 + 128:(h + 1) * 256].astype(BF16)


def _mla_kv(zo, g, w, layer, cos, sin, batch, tm=512):
    n = zo.shape[0]
    s = n // batch
    nblk = s // tm
    return pl.pallas_call(
        _mla_kv_body,
        grid=(batch, nblk),
        in_specs=[pl.BlockSpec((tm, MLA_KV_LORA), lambda b, i: (b * nblk + i, ZT_KVA // MLA_KV_LORA)),
                  pl.BlockSpec((tm, 256), lambda b, i: (b * nblk + i, ZT_KR // 256)),
                  _layer_spec(g, layer), _layer_spec(w, layer),
                  pl.BlockSpec((tm, 128), lambda b, i: (i, 0)),
                  pl.BlockSpec((tm, 128), lambda b, i: (i, 0))],
        out_specs=[pl.BlockSpec((1, MLA_HEADS, tm, MLA_QK_DIM), lambda b, i: (b, 0, i, 0)),
                   pl.BlockSpec((1, MLA_HEADS, tm, MLA_V_DIM), lambda b, i: (b, 0, i, 0))],
        out_shape=[jax.ShapeDtypeStruct((batch, MLA_HEADS, s, MLA_QK_DIM), BF16),
                   jax.ShapeDtypeStruct((batch, MLA_HEADS, s, MLA_V_DIM), BF16)],
        compiler_params=_params("parallel", "parallel"),
        name="mla_kv_proj",
    )(zo, zo, g, w, cos, sin)


def _flash_body(qi_ref, kj_ref, q_ref, k_ref, v_ref, o_ref, m_ref, l_ref, acc_ref, *, tq):
    p = pl.program_id(1)
    i = qi_ref[p]
    j = kj_ref[p]
    heads, dv = acc_ref.shape[0], acc_ref.shape[2]

    @pl.when(j == 0)
    def _():
        m_ref[...] = jnp.full_like(m_ref, NEG_INF)
        l_ref[...] = jnp.zeros_like(l_ref)
        acc_ref[...] = jnp.zeros_like(acc_ref)

    def update(masked):
        for h in range(heads):
            s = lax.dot_general(q_ref[0, h], k_ref[0, h], (((1,), (1,)), ((), ())), preferred_element_type=F32)
            if masked:
                qpos = lax.broadcasted_iota(jnp.int32, (tq, tq), 0)
                kpos = lax.broadcasted_iota(jnp.int32, (tq, tq), 1)
                s = jnp.where(kpos <= qpos, s, NEG_INF)
            m_old = m_ref[h]
            m_new = jnp.maximum(m_old, jnp.max(s, axis=-1, keepdims=True))
            alpha = jnp.exp(m_old - m_new)
            e = jnp.exp(s - jnp.concatenate([m_new] * (tq // dv), axis=1))
            l_ref[h] = alpha * l_ref[h] + jnp.sum(e, axis=-1, keepdims=True)
            acc_ref[h] = alpha * acc_ref[h] + jnp.dot(e.astype(BF16), v_ref[0, h], preferred_element_type=F32)
            m_ref[h] = m_new

    @pl.when(j < i)
    def _():
        update(False)

    @pl.when(j == i)
    def _():
        update(True)
        for h in range(heads):
            o_ref[0, :, h * dv:(h + 1) * dv] = (acc_ref[h] / l_ref[h]).astype(o_ref.dtype)


def _flash(q, k, v, tq=512):
    b, h, s, dq = q.shape
    dv = v.shape[-1]
    nq = s // tq
    pairs = [(i, j) for i in range(nq) for j in range(i + 1)]
    qi = jnp.asarray([pr[0] for pr in pairs], jnp.int32)
    kj = jnp.asarray([pr[1] for pr in pairs], jnp.int32)
    return pl.pallas_call(
        functools.partial(_flash_body, tq=tq),
        grid_spec=pltpu.PrefetchScalarGridSpec(
            num_scalar_prefetch=2, grid=(b, len(pairs)),
            in_specs=[pl.BlockSpec((1, h, tq, dq), lambda bi, p, qi, kj: (bi, 0, qi[p], 0)),
                      pl.BlockSpec((1, h, tq, dq), lambda bi, p, qi, kj: (bi, 0, kj[p], 0)),
                      pl.BlockSpec((1, h, tq, dv), lambda bi, p, qi, kj: (bi, 0, kj[p], 0))],
            out_specs=pl.BlockSpec((1, tq, h * dv), lambda bi, p, qi, kj: (bi, qi[p], 0)),
            scratch_shapes=[pltpu.VMEM((h, tq, dv), F32), pltpu.VMEM((h, tq, dv), F32), pltpu.VMEM((h, tq, dv), F32)]),
        out_shape=jax.ShapeDtypeStruct((b, s, h * dv), BF16),
        compiler_params=_params("parallel", "arbitrary"),
        name="mla_flash",
    )(qi, kj, q, k, v)


def _layernorm(h, g, b):
    mu = jnp.mean(h, axis=-1, keepdims=True)
    hc = h - mu
    var = jnp.mean(hc * hc, axis=-1, keepdims=True)
    return hc * lax.rsqrt(var + LN_EPS) * g + b


def _store_token_major(ref, val):
    tiles = val.shape[1] // LANES
    for c in range(tiles):
        ref[pl.ds(c, val.shape[0], stride=tiles), :] = val[:, c * LANES:(c + 1) * LANES]


def _load_token_major(ref, rows):
    tiles = ref.shape[0] // rows
    return jnp.concatenate([ref[pl.ds(c, rows, stride=tiles), :] for c in range(tiles)], axis=1)


def _out_ln_body(ya_ref, yb_ref, yc_ref, w_ref, x_ref, g_ref, b_ref, o_ref, otm_ref, *, alpha):
    ka, kb = ya_ref.shape[1], yb_ref.shape[1]
    acc = (jnp.dot(ya_ref[...], w_ref[0:ka, :], preferred_element_type=F32)
           + jnp.dot(yb_ref[...], w_ref[ka:ka + kb, :], preferred_element_type=F32)
           + jnp.dot(yc_ref[...], w_ref[ka + kb:, :], preferred_element_type=F32))
    x1 = _layernorm(alpha * x_ref[...] + acc, g_ref[...], b_ref[...])
    o_ref[...] = x1
    _store_token_major(otm_ref, x1)


def _out_ln(ya, yb, yc, w, x, g, b, layer, alpha, tm=256):
    n, d = x.shape
    row = lambda c: pl.BlockSpec((tm, c), lambda i: (i, 0))
    return pl.pallas_call(
        functools.partial(_out_ln_body, alpha=alpha),
        grid=(n // tm,),
        in_specs=[row(ya.shape[1]), row(yb.shape[1]), row(yc.shape[1]),
                  _layer_spec(w, layer), row(d), _layer_spec(g, layer), _layer_spec(b, layer)],
        out_specs=[row(d), pl.BlockSpec((tm * (d // LANES), LANES), lambda i: (i, 0))],
        out_shape=[jax.ShapeDtypeStruct((n, d), F32), jax.ShapeDtypeStruct((n * (d // LANES), LANES), F32)],
        compiler_params=_params("parallel"),
        name="out_proj_ln",
    )(ya, yb, yc, w, x, g, b)


def _router_body(x_ref, w_ref, b_ref, idx_ref, gate_ref):
    logits = lax.dot_general(w_ref[...], x_ref[...], (((1,), (1,)), ((), ())),
                             precision=HIGHEST, preferred_element_type=F32)
    scores = _sigmoid(logits)
    sel = scores + b_ref[...]
    ng = N_EXPERT_GROUPS
    sel_j = [sel[j * ng:(j + 1) * ng] for j in range(EXPERTS_PER_GROUP)]
    sc_j = [scores[j * ng:(j + 1) * ng] for j in range(EXPERTS_PER_GROUP)]
    grp = None
    for p in range(EXPERTS_PER_GROUP):
        for q in range(p + 1, EXPERTS_PER_GROUP):
            pair = sel_j[p] + sel_j[q]
            grp = pair if grp is None else jnp.maximum(grp, pair)
    gi = lax.broadcasted_iota(jnp.int32, grp.shape, 0)
    gmax = jnp.max(grp, axis=0, keepdims=True)
    gbest = jnp.min(jnp.where(grp == gmax, gi, ng), axis=0, keepdims=True)
    pick = gi == gbest
    mem = [jnp.sum(jnp.where(pick, t, 0.0), axis=0, keepdims=True) for t in sel_j]
    msc = [jnp.sum(jnp.where(pick, t, 0.0), axis=0, keepdims=True) for t in sc_j]

    def first_argmax(vals, exclude):
        best = jnp.full_like(vals[0], -jnp.inf)
        bi = jnp.zeros(vals[0].shape, jnp.int32)
        bs = jnp.zeros_like(vals[0])
        for j in range(EXPERTS_PER_GROUP):
            ok = vals[j] > best
            if exclude is not None:
                ok = ok & (exclude != j)
            best = jnp.where(ok, vals[j], best)
            bi = jnp.where(ok, j, bi)
            bs = jnp.where(ok, msc[j], bs)
        return bi, bs

    i1, s1 = first_argmax(mem, None)
    i2, s2 = first_argmax(mem, i1)
    tot = s1 + s2
    base = gbest * EXPERTS_PER_GROUP
    idx_ref[0:1, :] = base + i1
    idx_ref[1:2, :] = base + i2
    gate_ref[0:1, :] = s1 / tot
    gate_ref[1:2, :] = s2 / tot


def _router(x, w_t, bias, tm=512):
    n, d = x.shape
    return pl.pallas_call(
        _router_body,
        grid=(n // tm,),
        in_specs=[pl.BlockSpec((tm, d), lambda i: (i, 0)),
                  pl.BlockSpec((N_EXPERTS, d), lambda i: (0, 0)),
                  pl.BlockSpec((N_EXPERTS, 1), lambda i: (0, 0))],
        out_specs=[pl.BlockSpec((TOP_K, tm), lambda i: (0, i)), pl.BlockSpec((TOP_K, tm), lambda i: (0, i))],
        out_shape=[jax.ShapeDtypeStruct((TOP_K, n), jnp.int32), jax.ShapeDtypeStruct((TOP_K, n), F32)],
        compiler_params=_params("parallel"),
        name="moe_router",
    )(x, w_t, bias)


def _expert_body(plan_ref, meta_ref, rows_ref, x_hbm, wg_hbm, wu_hbm, wd_hbm, out_hbm,
                 xg0_ref, xg1_ref, xg2_ref, yo0_ref, yo1_ref, yo2_ref, wfg_ref, wfu_ref, wfd_ref,
                 wgb_ref, wub_ref, wdb_ref, sem_in, sem_out, sem_w, *, tok_bits, n_blocks, layer):
    i = pl.program_id(0)
    n_act = meta_ref[0]
    xg = (xg0_ref, xg1_ref, xg2_ref)
    yo = (yo0_ref, yo1_ref, yo2_ref)
    last = n_blocks - 1
    tpr = wgb_ref.shape[0] // LANES

    def token_rows(ref, index):
        return ref.at[pl.ds(pl.multiple_of(index * tpr, tpr), tpr)]

    def start_gather(blk, par):
        for r in range(MOE_BLOCK):
            tok = rows_ref[blk * MOE_BLOCK + r] & ((1 << tok_bits) - 1)
            pltpu.make_async_copy(token_rows(x_hbm, tok), token_rows(xg[par], r), sem_in.at[par]).start()

    def wait_gather(par):
        for r in range(MOE_BLOCK):
            pltpu.make_async_copy(token_rows(x_hbm, 0), token_rows(xg[par], 0), sem_in.at[par]).wait()

    def start_scatter(blk, par):
        for r in range(MOE_BLOCK):
            row = rows_ref[blk * MOE_BLOCK + r] >> tok_bits
            pltpu.make_async_copy(token_rows(yo[par], r), token_rows(out_hbm, row), sem_out.at[par]).start()

    def wait_scatter(par):
        for r in range(MOE_BLOCK):
            pltpu.make_async_copy(token_rows(yo[par], 0), token_rows(out_hbm, 0), sem_out.at[par]).wait()

    def compute(par):
        xb = _load_token_major(xg[par], MOE_BLOCK).astype(BF16)
        gt = jnp.dot(xb, wgb_ref[...], preferred_element_type=F32)
        up = jnp.dot(xb, wub_ref[...], preferred_element_type=F32)
        hid = (gt * _sigmoid(gt) * up).astype(BF16)
        _store_token_major(yo[par], jnp.dot(hid, wdb_ref[...], preferred_element_type=F32))

    e_cur, first, wslot, e_next = (plan_ref[k * n_blocks + i] for k in range(4))

    def weight_copies(e, slot):
        pairs = ((wg_hbm, wfg_ref), (wu_hbm, wfu_ref), (wd_hbm, wfd_ref))
        copies = []
        for w, wf in pairs:
            half = w.shape[2] // 2
            for h in range(2):
                rows_h = pl.ds(h * half, half)
                copies.append((pltpu.make_async_copy(w.at[layer, e, rows_h], wf.at[slot, rows_h], sem_w.at[slot]), h))
        return copies

    @pl.when((i < n_act) & (first == 1))
    def _():
        @pl.when(i == 0)
        def _():
            for cp, h in weight_copies(e_cur, wslot):
                cp.start(priority=h)

        for cp, _ in weight_copies(e_cur, wslot):
            cp.wait()

        @pl.when(e_next >= 0)
        def _():
            for cp, h in weight_copies(e_next, 1 - wslot):
                cp.start(priority=h)

        wgb_ref[...] = wfg_ref[wslot].astype(BF16)
        wub_ref[...] = wfu_ref[wslot].astype(BF16)
        wdb_ref[...] = wfd_ref[wslot].astype(BF16)

    @pl.when(i == 0)
    def _():
        start_gather(0, 0)
        start_gather(1, 1)
        yo0_ref[...] = jnp.zeros_like(yo0_ref)
        fill_rows = MOE_BLOCK * tpr
        spare0 = out_hbm.shape[0] - 2 * fill_rows
        fills = [pltpu.make_async_copy(yo0_ref, out_hbm.at[pl.ds(spare0 + h * fill_rows, fill_rows)], sem_out.at[0])
                 for h in range(2)]
        for cp in fills:
            cp.start()
        for cp in fills:
            cp.wait()
        wait_gather(0)
        start_gather(2, 2)
        compute(0)

    res = lax.rem(i, 3)
    for q in range(3):
        prv, nxt = (q + 2) % 3, (q + 1) % 3

        @pl.when((i > 0) & (i < n_act) & (res == q))
        def _(q=q, prv=prv):
            wait_gather(q)

            @pl.when(i >= 3)
            def _():
                wait_scatter(q)

            start_gather(jnp.minimum(i + 2, last), prv)
            start_scatter(i - 1, prv)
            compute(q)

    for q in range(3):
        prv, nxt = (q + 2) % 3, (q + 1) % 3

        @pl.when((i == n_act - 1) & (res == q))
        def _(q=q, prv=prv, nxt=nxt):
            start_scatter(i, q)

            @pl.when(i >= 2)
            def _():
                wait_scatter(nxt)

            wait_scatter(prv)
            wait_scatter(q)
            wait_gather(nxt)
            wait_gather(prv)


def _experts(plan, meta, rows, x_tm, wg, wu, wd, layer):
    d = wg.shape[2]
    tpr = d // LANES
    n = x_tm.shape[0] // tpr
    nb = plan.shape[0] // 4
    buf = pltpu.VMEM((MOE_BLOCK * tpr, LANES), F32)
    hbm = pl.BlockSpec(memory_space=pl.ANY)
    return pl.pallas_call(
        functools.partial(_expert_body, tok_bits=_tok_bits(n), n_blocks=nb, layer=layer),
        grid_spec=pltpu.PrefetchScalarGridSpec(
            num_scalar_prefetch=3, grid=(nb,),
            in_specs=[hbm, hbm, hbm, hbm],
            out_specs=hbm,
            scratch_shapes=[buf] * 6 + [
                pltpu.VMEM((2, d, D_EXPERT), F32), pltpu.VMEM((2, d, D_EXPERT), F32), pltpu.VMEM((2, D_EXPERT, d), F32),
                pltpu.VMEM((d, D_EXPERT), BF16), pltpu.VMEM((d, D_EXPERT), BF16), pltpu.VMEM((D_EXPERT, d), BF16),
                pltpu.SemaphoreType.DMA((3,)), pltpu.SemaphoreType.DMA((3,)), pltpu.SemaphoreType.DMA((2,))]),
        out_shape=jax.ShapeDtypeStruct(((TOP_K * n + 2 * MOE_BLOCK) * tpr, LANES), F32),
        compiler_params=_params("arbitrary"),
        name="moe_experts",
    )(plan, meta, rows, x_tm, wg, wu, wd)


def _tok_bits(n):
    return max((n - 1).bit_length(), 1)


def _dispatch(idx):
    n = idx.shape[1]
    a_tot = TOP_K * n
    e_flat = idx.reshape(a_tot)
    onehot = (e_flat[:, None] == jnp.arange(N_EXPERTS, dtype=jnp.int32)[None, :]).astype(F32)
    chunk = 128
    oh = onehot.reshape(a_tot // chunk, chunk, N_EXPERTS)
    tri = (jnp.arange(chunk)[None, :] <= jnp.arange(chunk)[:, None]).astype(F32)
    within = jnp.einsum('ts,csn->ctn', tri, oh)
    tot = within[:, -1, :]
    csum = (within + (jnp.cumsum(tot, axis=0) - tot)[:, None, :]).reshape(a_tot, N_EXPERTS)
    rank = jnp.sum(onehot * csum, axis=1).astype(jnp.int32) - 1
    counts = csum[-1].astype(jnp.int32)
    padded = (counts + MOE_BLOCK - 1) // MOE_BLOCK * MOE_BLOCK
    pad_end = jnp.cumsum(padded)
    pad_start = pad_end - padded
    dest = pad_start[e_flat] + rank
    nb = (a_tot + N_EXPERTS * (MOE_BLOCK - 1) + MOE_BLOCK - 1) // MOE_BLOCK
    p_rows = nb * MOE_BLOCK
    bits = _tok_bits(n)
    a_ids = jnp.arange(a_tot, dtype=jnp.int32)
    pr = jnp.arange(p_rows, dtype=jnp.int32)
    spare = a_tot + ((pr // MOE_BLOCK) % 2) * MOE_BLOCK + pr % MOE_BLOCK
    rows = (spare << bits).at[dest].set((a_ids % n) | (a_ids << bits))
    blk_start = jnp.arange(nb, dtype=jnp.int32) * MOE_BLOCK
    blk_e = jnp.sum((pad_end[None, :] <= blk_start[:, None]).astype(jnp.int32), axis=1)
    blk_e = jnp.minimum(blk_e, N_EXPERTS - 1)
    n_act = jnp.minimum(pad_end[-1] // MOE_BLOCK + 1, nb).astype(jnp.int32)
    steps = jnp.arange(nb, dtype=jnp.int32)
    first = (steps < n_act) & ((steps == 0) | (blk_e != jnp.roll(blk_e, 1)))
    wslot = (jnp.cumsum(first.astype(jnp.int32)) - 1) & 1
    next_first = lax.cummin(jnp.where(first, steps, nb), reverse=True)
    next_first = jnp.concatenate([next_first[1:], jnp.full((1,), nb, jnp.int32)])
    e_next = jnp.where(next_first < nb, blk_e[jnp.minimum(next_first, nb - 1)], -1)
    plan = jnp.concatenate([blk_e, first.astype(jnp.int32), wslot, e_next]).astype(jnp.int32)
    return plan, n_act.reshape(1), rows


def _final_body(x_ref, ya_ref, yb_ref, ga_ref, gb_ref, g_ref, b_ref, p_ref, wp_ref, wg_ref, o_ref, ob_ref, *, alpha):
    rows = x_ref.shape[0]
    ffn = _load_token_major(ya_ref, rows) * ga_ref[...] + _load_token_major(yb_ref, rows) * gb_ref[...]
    x2 = _layernorm(alpha * x_ref[...] + ffn, g_ref[...], b_ref[...])
    gate = jnp.dot(x2.astype(BF16), wg_ref[...], preferred_element_type=F32)
    proj = jnp.dot(p_ref[...].astype(BF16), wp_ref[...], preferred_element_type=F32)
    out = x2 + _sigmoid(gate) * proj
    o_ref[...] = out
    ob_ref[...] = out.astype(BF16)


def _final(x, y2, gates, g, b, p, wp, wg, layer, alpha, tm=256):
    n, d = x.shape
    nblk = n // tm
    row = pl.BlockSpec((tm, d), lambda i: (i, 0))
    return pl.pallas_call(
        functools.partial(_final_body, alpha=alpha),
        grid=(nblk,),
        in_specs=[row, pl.BlockSpec((tm * (d // LANES), LANES), lambda i: (i, 0)),
                  pl.BlockSpec((tm * (d // LANES), LANES), lambda i: (i + nblk, 0)),
                  pl.BlockSpec((tm, 1), lambda i: (i, 0)), pl.BlockSpec((tm, 1), lambda i: (i + nblk, 0)),
                  _layer_spec(g, layer), _layer_spec(b, layer),
                  pl.BlockSpec((None, tm, PLE_DIM), lambda i: (layer, i, 0)),
                  _layer_spec(wp, layer), _layer_spec(wg, layer)],
        out_specs=[row, row],
        out_shape=[jax.ShapeDtypeStruct((n, d), F32), jax.ShapeDtypeStruct((n, d), BF16)],
        compiler_params=_params("parallel"),
        name="moe_combine_ln_ple",
    )(x, y2, y2, gates, gates, g, b, p, wp, wg)


def _rot_cols(w):
    half = MLA_ROPE_DIM // 2
    return jnp.concatenate([-w[..., half:], w[..., :half]], axis=-1)


def _pad_lanes(w, width=128):
    return jnp.pad(w, [(0, 0)] * (w.ndim - 1) + [(0, width - w.shape[-1])])


def _rope_tables(s):
    half = MLA_ROPE_DIM // 2
    inv = ROPE_THETA ** (-jnp.arange(half, dtype=F32) / half)
    ang = jnp.arange(s, dtype=F32)[:, None] * inv[None, :]
    cos = jnp.concatenate([jnp.cos(ang), jnp.cos(ang)], axis=-1)
    sin = jnp.concatenate([jnp.sin(ang), jnp.sin(ang)], axis=-1)
    return _pad_lanes(cos), _pad_lanes(sin)


def kernel(x, p, w_in, rwkv_mu, rwkv_w0, rwkv_w_up, rwkv_a0, rwkv_a_up, rwkv_g_up, rwkv_k_k, rwkv_k_a, rwkv_r_k, rwkv_gn_g, rwkv_gn_b, mla_qa_g, mla_w_uq, mla_kva_g, mla_w_ukv, w_out, ln1_g, ln1_b, router_w, router_b, moe_w_gate, moe_w_up, moe_w_down, ln2_g, ln2_b, ple_w_proj, ple_w_gate):
    batch, s, d = x.shape
    depth = w_in.shape[0]
    n = batch * s
    alpha = (2 * depth) ** 0.25
    assert s % (DIL_PATTERNS[-1][1] * DIL_BLOCK) == 0 and s % 512 == 0

    cos, sin = _rope_tables(s)
    slopes = jnp.exp2(-ALIBI_MAX_BIAS * jnp.arange(1, 13, dtype=F32) / 12)
    perm = np.array([4 * g + j for j in range(EXPERTS_PER_GROUP) for g in range(N_EXPERT_GROUPS)])
    router_wt = router_w.T[perm]
    router_bt = router_b[perm].reshape(N_EXPERTS, 1)

    row = lambda t: t.reshape(depth, 1, -1)
    hpar = lambda t: t.reshape(depth, RWKV_PAIRS, 1, PAIR_LANES)
    kva0 = RWKV_IN_DIM + ZO_DIM
    w_kr = w_in[:, :, kva0 + MLA_KV_LORA:]
    w_tail = jnp.concatenate([w_in[:, :, kva0:kva0 + MLA_KV_LORA], _pad_lanes(w_kr), _pad_lanes(_rot_cols(w_kr))],
                             axis=2).astype(BF16)
    w_up_pad = jnp.pad(rwkv_w_up, ((0, 0), (0, 64), (0, 0)))
    w_up_hi = w_up_pad.astype(BF16)
    w_up_parts = jnp.stack([w_up_hi, (w_up_pad - w_up_hi.astype(F32)).astype(BF16)], axis=1)
    a_up_pad = jnp.pad(rwkv_a_up, ((0, 0), (64, 0), (0, 0))).astype(BF16)
    rwkv_params = (row(rwkv_mu), row(rwkv_w0), row(rwkv_a0), row(rwkv_k_k), row(rwkv_k_a), w_up_parts, a_up_pad,
                   rwkv_g_up.astype(BF16), hpar(rwkv_r_k), hpar(rwkv_gn_g), hpar(rwkv_gn_b))
    wq = mla_w_uq.reshape(depth, MLA_Q_LORA, MLA_HEADS, MLA_QK_DIM)
    wq_rope = wq[..., MLA_NOPE_DIM:]
    wq_all = jnp.concatenate(
        [wq[..., :MLA_NOPE_DIM].reshape(depth, MLA_Q_LORA, -1),
         jnp.concatenate([_pad_lanes(wq_rope), _pad_lanes(_rot_cols(wq_rope))], axis=-1).reshape(depth, MLA_Q_LORA, -1)],
        axis=2).astype(BF16)
    w_ukv_b, w_out_b = mla_w_ukv.astype(BF16), w_out.astype(BF16)
    ple_proj_b, ple_gate_b = ple_w_proj.astype(BF16), ple_w_gate.astype(BF16)
    qa_g, kva_g = row(mla_qa_g), row(mla_kva_g)
    ln1 = (row(ln1_g), row(ln1_b))
    ln2 = (row(ln2_g), row(ln2_b))
    p_rows = p.reshape(depth, n, PLE_DIM)

    xf = x.reshape(n, d)
    xb = xf.astype(BF16)
    for li in range(depth):
        z_r = _in_proj(xb, w_in, li, 0, RWKV_IN_DIM, F32)
        z_o = _in_proj(xb, w_in, li, RWKV_IN_DIM, ZO_DIM, BF16)
        z_t = _matmul(xb, w_tail, li, BF16, 1024, ZT_DIM)

        y_a = _rwkv_mix(z_r.reshape(batch, s, RWKV_IN_DIM), li, *rwkv_params).reshape(n, RWKV_DIM)

        prior = _dilated_group(slopes, z_o, 0, batch, s) + _dilated_group(slopes, z_o, 1, batch, s)
        y_b = _dilated_group(slopes, z_o, 2, batch, s, prior=prior)

        q_c = _mla_q(z_o, qa_g, wq_all, li, cos, sin, batch)
        k_c, v_c = _mla_kv(z_t, kva_g, w_ukv_b, li, cos, sin, batch)
        y_c = _flash(q_c, k_c, v_c).reshape(n, MLA_OUT_DIM)

        x1, x1_tm = _out_ln(y_a, y_b, y_c, w_out_b, xf, *ln1, li, alpha)

        idx, gate = _router(x1, router_wt, router_bt)
        plan, meta, rows = _dispatch(idx)
        y2 = _experts(plan, meta, rows, x1_tm, moe_w_gate, moe_w_up, moe_w_down, li)

        xf, xb = _final(x1, y2, gate.reshape(TOP_K * n, 1), *ln2, p_rows, ple_proj_b, ple_gate_b, li, alpha)
    return xf.reshape(batch, s, d)
```
